```python
import jax, jax.numpy as jnp
from jax import lax
import numpy as np

D_MODEL = 2048
BATCH = 8
SEQ = 8192
DEPTH = 2

GRID_W = 64
CTX_LEN = 256
D_LRU = 1024
N_LRU_HEADS = 4
LRU_HEAD_DIM = D_LRU // N_LRU_HEADS
RG_C = 8.0
CONV_SHORT = 4
SHORT_PAD_L = 2
SHORT_PAD_R = 1
D_CONV = 1024
CONV_K = 31
D_MIX = D_LRU + D_CONV
D_IN = 2 * D_LRU + 2 * D_CONV
D_FF = 5632
N_MOD = 9
ALPHA = (2 * DEPTH) ** 0.25
BETA = (8 * DEPTH) ** -0.25
ADA_SCALE = 0.5
EPS = 1e-6

kernel_name = 'hybrid_rglru_conformer_dit_block'


def _layernorm(x, g, b):
    xf = x.astype(jnp.float32)
    mu = jnp.mean(xf, axis=-1, keepdims=True)
    var = jnp.mean(jnp.square(xf - mu), axis=-1, keepdims=True)
    y = (xf - mu) * lax.rsqrt(var + EPS) * g.astype(jnp.float32) + b.astype(jnp.float32)
    return y.astype(x.dtype)


def _modulate(s, shift, scale):
    return s * (1 + scale) + shift


def _swiglu(h, w_in, w_out):
    gate, up = jnp.split(h @ w_in, 2, axis=-1)
    return (jax.nn.silu(gate) * up) @ w_out


def _half_ffn(s, shift, scale, gate, w1, w2, g, b):
    return _layernorm(ALPHA * s + 0.5 * gate * _swiglu(_modulate(s, shift, scale), w1, w2), g, b)


def _dwconv(x, w, b, pad_left, pad_right):
    y = lax.conv_general_dilated(
        x, w.astype(x.dtype)[:, None, :], window_strides=(1,),
        padding=[(pad_left, pad_right)], dimension_numbers=('NWC', 'WIO', 'NWC'),
        feature_group_count=x.shape[-1])
    return y + b


def _lin_combine(left, right):
    a1, b1 = left
    a2, b2 = right
    return a1 * a2, a2 * b1 + b2


def _rglru(xc, w_r, b_r, w_i, b_i, lam, h0, reverse):
    bsz, t, c = xc.shape
    xf = xc.astype(jnp.float32)
    xh = xf.reshape(bsz, t, N_LRU_HEADS, LRU_HEAD_DIM)
    r = jax.nn.sigmoid(jnp.einsum('bthi,hij->bthj', xh, w_r.astype(jnp.float32)).reshape(bsz, t, c)
                       + b_r.astype(jnp.float32))
    i = jax.nn.sigmoid(jnp.einsum('bthi,hij->bthj', xh, w_i.astype(jnp.float32)).reshape(bsz, t, c)
                       + b_i.astype(jnp.float32))
    log_a = -RG_C * r * jax.nn.softplus(-lam.astype(jnp.float32))
    a = jnp.exp(log_a)
    b = jnp.sqrt(-jnp.expm1(2.0 * log_a)) * (i * xf)
    if h0 is not None:
        idx = t - 1 if reverse else 0
        b = b.at[:, idx].add(a[:, idx] * h0)
    _, h = lax.associative_scan(_lin_combine, (a, b), axis=1, reverse=reverse)
    return h


def _conv_module(v, gate, w, b, g, bb, n_seg):
    bsz, t, c = v.shape
    u = (v * jax.nn.sigmoid(gate)).reshape(bsz * n_seg, t // n_seg, c)
    u = _dwconv(u, w, b, CONV_K // 2, CONV_K // 2).reshape(bsz, t, c)
    return jax.nn.silu(_layernorm(u, g, bb))


def _mixer(h_lat, h_ctx, rows, w_in, conv4_w, conv4_b, w_rg, b_rg, w_ig, b_ig, lam,
           conv31_w, conv31_b, cln_g, cln_b, w_out, b_out, ctx_out):
    splits = [D_LRU, 2 * D_LRU, 2 * D_LRU + D_CONV]
    xr_l, gr_l, cv_l, cg_l = jnp.split(h_lat @ w_in, splits, axis=-1)
    if ctx_out:
        xr_c, gr_c, cv_c, cg_c = jnp.split(h_ctx @ w_in, splits, axis=-1)
    else:
        xr_c = h_ctx @ w_in[:, :D_LRU]
    xr_l = _dwconv(xr_l, conv4_w, conv4_b, SHORT_PAD_L, SHORT_PAD_R)
    xr_c = _dwconv(xr_c, conv4_w, conv4_b, SHORT_PAD_L, SHORT_PAD_R)
    rec_l = []
    rec_c = []
    for d, rev in ((0, False), (1, True)):
        h_c = _rglru(xr_c, w_rg[d], b_rg[d], w_ig[d], b_ig[d], lam[d], None, rev)
        h0 = h_c[:, 0] if rev else h_c[:, -1]
        h_l = _rglru(xr_l, w_rg[d], b_rg[d], w_ig[d], b_ig[d], lam[d], h0, rev)
        rec_l.append(h_l)
        rec_c.append(h_c)
    y_rec_l = (rec_l[0] + rec_l[1]).astype(h_lat.dtype) * jax.nn.gelu(gr_l)
    y_conv_l = _conv_module(cv_l, cg_l, conv31_w, conv31_b, cln_g, cln_b, rows)
    y_lat = jnp.concatenate([y_rec_l, y_conv_l], axis=-1) @ w_out + b_out
    if not ctx_out:
        return y_lat, None
    y_rec_c = (rec_c[0] + rec_c[1]).astype(h_ctx.dtype) * jax.nn.gelu(gr_c)
    y_conv_c = _conv_module(cv_c, cg_c, conv31_w, conv31_b, cln_g, cln_b, 1)
    y_ctx = jnp.concatenate([y_rec_c, y_conv_c], axis=-1) @ w_out + b_out
    return y_lat, y_ctx


def _fwd_setup_inputs(seed: int = 0) -> dict:
    key = jax.random.key(seed)
    ks = jax.random.split(key, 32)

    def nrm(k, shape, scale):
        return jax.random.normal(k, shape, jnp.float32) * scale

    x = nrm(ks[0], (BATCH, SEQ, D_MODEL), 1.0)
    c = nrm(ks[1], (BATCH, D_MODEL), 1.0)
    ctx = nrm(ks[2], (BATCH, CTX_LEN, D_MODEL), 1.0)
    c_ctx = nrm(ks[3], (D_MODEL,), 1.0)
    w_ada = nrm(ks[4], (DEPTH, D_MODEL, N_MOD * D_MODEL), ADA_SCALE * D_MODEL ** -0.5)
    b_ada = nrm(ks[5], (DEPTH, N_MOD * D_MODEL), 0.02)
    ln_g = 1.0 + nrm(ks[6], (DEPTH, 3, D_MODEL), 0.02)
    ln_b = nrm(ks[7], (DEPTH, 3, D_MODEL), 0.02)
    ff1_in = nrm(ks[8], (DEPTH, D_MODEL, 2 * D_FF), D_MODEL ** -0.5)
    ff1_out = nrm(ks[9], (DEPTH, D_FF, D_MODEL), BETA * D_FF ** -0.5)
    ff2_in = nrm(ks[10], (DEPTH, D_MODEL, 2 * D_FF), D_MODEL ** -0.5)
    ff2_out = nrm(ks[11], (DEPTH, D_FF, D_MODEL), BETA * D_FF ** -0.5)
    w_in = nrm(ks[12], (DEPTH, D_MODEL, D_IN), D_MODEL ** -0.5)
    conv4_w = nrm(ks[13], (DEPTH, CONV_SHORT, D_LRU), CONV_SHORT ** -0.5)
    conv4_b = nrm(ks[14], (DEPTH, D_LRU), 0.02)
    w_rg = nrm(ks[15], (DEPTH, 2, N_LRU_HEADS, LRU_HEAD_DIM, LRU_HEAD_DIM), LRU_HEAD_DIM ** -0.5)
    b_rg = nrm(ks[16], (DEPTH, 2, D_LRU), 0.02)
    w_ig = nrm(ks[17], (DEPTH, 2, N_LRU_HEADS, LRU_HEAD_DIM, LRU_HEAD_DIM), LRU_HEAD_DIM ** -0.5)
    b_ig = nrm(ks[18], (DEPTH, 2, D_LRU), 0.02)
    a_c = jax.random.uniform(ks[19], (DEPTH, 2, D_LRU), jnp.float32, minval=0.9, maxval=0.999)
    a_base = a_c ** (1.0 / RG_C)
    lam = jnp.log(a_base) - jnp.log1p(-a_base)
    conv31_w = nrm(ks[20], (DEPTH, CONV_K, D_CONV), CONV_K ** -0.5)
    conv31_b = nrm(ks[21], (DEPTH, D_CONV), 0.02)
    cln_g = 1.0 + nrm(ks[22], (DEPTH, D_CONV), 0.02)
    cln_b = nrm(ks[23], (DEPTH, D_CONV), 0.02)
    w_out = nrm(ks[24], (DEPTH, D_MIX, D_MODEL), BETA * D_MIX ** -0.5)
    b_out = nrm(ks[25], (DEPTH, D_MODEL), 0.02)
    return {'x': x, 'c': c, 'ctx': ctx, 'c_ctx': c_ctx, 'w_ada': w_ada, 'b_ada': b_ada,
            'ln_g': ln_g, 'ln_b': ln_b, 'ff1_in': ff1_in, 'ff1_out': ff1_out,
            'ff2_in': ff2_in, 'ff2_out': ff2_out, 'w_in': w_in, 'conv4_w': conv4_w,
            'conv4_b': conv4_b, 'w_rg': w_rg, 'b_rg': b_rg, 'w_ig': w_ig, 'b_ig': b_ig,
            'lam': lam, 'conv31_w': conv31_w, 'conv31_b': conv31_b, 'cln_g': cln_g,
            'cln_b': cln_b, 'w_out': w_out, 'b_out': b_out}


def _fwd_reference(x, c, ctx, c_ctx, w_ada, b_ada, ln_g, ln_b, ff1_in, ff1_out, ff2_in, ff2_out,
              w_in, conv4_w, conv4_b, w_rg, b_rg, w_ig, b_ig, lam, conv31_w, conv31_b,
              cln_g, cln_b, w_out, b_out):
    rows = x.shape[1] // GRID_W
    for l in range(DEPTH):
        last = l == DEPTH - 1
        m = jnp.split((jax.nn.silu(c) @ w_ada[l] + b_ada[l])[:, None, :], N_MOD, axis=-1)
        mc = jnp.split((jax.nn.silu(c_ctx) @ w_ada[l] + b_ada[l])[None, None, :], N_MOD, axis=-1)
        x = _half_ffn(x, m[0], m[1], m[2], ff1_in[l], ff1_out[l], ln_g[l, 0], ln_b[l, 0])
        ctx = _half_ffn(ctx, mc[0], mc[1], mc[2], ff1_in[l], ff1_out[l], ln_g[l, 0], ln_b[l, 0])
        y_lat, y_ctx = _mixer(_modulate(x, m[3], m[4]), _modulate(ctx, mc[3], mc[4]), rows,
                              w_in[l], conv4_w[l], conv4_b[l], w_rg[l], b_rg[l], w_ig[l], b_ig[l],
                              lam[l], conv31_w[l], conv31_b[l], cln_g[l], cln_b[l], w_out[l],
                              b_out[l], not last)
        x = _layernorm(ALPHA * x + m[5] * y_lat, ln_g[l, 1], ln_b[l, 1])
        if not last:
            ctx = _layernorm(ALPHA * ctx + mc[5] * y_ctx, ln_g[l, 1], ln_b[l, 1])
            ctx = _half_ffn(ctx, mc[6], mc[7], mc[8], ff2_in[l], ff2_out[l], ln_g[l, 2], ln_b[l, 2])
        x = _half_ffn(x, m[6], m[7], m[8], ff2_in[l], ff2_out[l], ln_g[l, 2], ln_b[l, 2])
    return x


import jax as _jax
import jax.numpy as _jnp

TWIN_FORMAT = 'train_step'
FWD_PARAMS = ['x', 'c', 'ctx', 'c_ctx', 'w_ada', 'b_ada', 'ln_g', 'ln_b', 'ff1_in', 'ff1_out', 'ff2_in', 'ff2_out', 'w_in', 'conv4_w', 'conv4_b', 'w_rg', 'b_rg', 'w_ig', 'b_ig', 'lam', 'conv31_w', 'conv31_b', 'cln_g', 'cln_b', 'w_out', 'b_out']
TWIN_WEIGHTS = ['c_ctx', 'w_ada', 'b_ada', 'ln_g', 'ln_b', 'ff1_in', 'ff1_out', 'ff2_in', 'ff2_out', 'w_in', 'conv4_w', 'conv4_b', 'w_rg', 'b_rg', 'w_ig', 'b_ig', 'lam', 'conv31_w', 'conv31_b', 'cln_g', 'cln_b', 'w_out', 'b_out']
TWIN_DIFF_INPUT = 'x'
TWIN_INPUTS = ['x', 'c', 'ctx', 'c_ctx', 'w_ada', 'b_ada', 'ln_g', 'ln_b', 'ff1_in', 'ff1_out', 'ff2_in', 'ff2_out', 'w_in', 'conv4_w', 'conv4_b', 'w_rg', 'b_rg', 'w_ig', 'b_ig', 'lam', 'conv31_w', 'conv31_b', 'cln_g', 'cln_b', 'w_out', 'b_out', 'loss_target', 'm_c_ctx', 'm_w_ada', 'm_b_ada', 'm_ln_g', 'm_ln_b', 'm_ff1_in', 'm_ff1_out', 'm_ff2_in', 'm_ff2_out', 'm_w_in', 'm_conv4_w', 'm_conv4_b', 'm_w_rg', 'm_b_rg', 'm_w_ig', 'm_b_ig', 'm_lam', 'm_conv31_w', 'm_conv31_b', 'm_cln_g', 'm_cln_b', 'm_w_out', 'm_b_out', 'v_c_ctx', 'v_w_ada', 'v_b_ada', 'v_ln_g', 'v_ln_b', 'v_ff1_in', 'v_ff1_out', 'v_ff2_in', 'v_ff2_out', 'v_w_in', 'v_conv4_w', 'v_conv4_b', 'v_w_rg', 'v_b_rg', 'v_w_ig', 'v_b_ig', 'v_lam', 'v_conv31_w', 'v_conv31_b', 'v_cln_g', 'v_cln_b', 'v_w_out', 'v_b_out']
TWIN_OUTPUTS = ['loss', 'grad_x', 'grad_c_ctx', 'grad_w_ada', 'grad_b_ada', 'grad_ln_g', 'grad_ln_b', 'grad_ff1_in', 'grad_ff1_out', 'grad_ff2_in', 'grad_ff2_out', 'grad_w_in', 'grad_conv4_w', 'grad_conv4_b', 'grad_w_rg', 'grad_b_rg', 'grad_w_ig', 'grad_b_ig', 'grad_lam', 'grad_conv31_w', 'grad_conv31_b', 'grad_cln_g', 'grad_cln_b', 'grad_w_out', 'grad_b_out', 'delta_c_ctx', 'delta_w_ada', 'delta_b_ada', 'delta_ln_g', 'delta_ln_b', 'delta_ff1_in', 'delta_ff1_out', 'delta_ff2_in', 'delta_ff2_out', 'delta_w_in', 'delta_conv4_w', 'delta_conv4_b', 'delta_w_rg', 'delta_b_rg', 'delta_w_ig', 'delta_b_ig', 'delta_lam', 'delta_conv31_w', 'delta_conv31_b', 'delta_cln_g', 'delta_cln_b', 'delta_w_out', 'delta_b_out', 'new_m_c_ctx', 'new_m_w_ada', 'new_m_b_ada', 'new_m_ln_g', 'new_m_ln_b', 'new_m_ff1_in', 'new_m_ff1_out', 'new_m_ff2_in', 'new_m_ff2_out', 'new_m_w_in', 'new_m_conv4_w', 'new_m_conv4_b', 'new_m_w_rg', 'new_m_b_rg', 'new_m_w_ig', 'new_m_b_ig', 'new_m_lam', 'new_m_conv31_w', 'new_m_conv31_b', 'new_m_cln_g', 'new_m_cln_b', 'new_m_w_out', 'new_m_b_out', 'new_v_c_ctx', 'new_v_w_ada', 'new_v_b_ada', 'new_v_ln_g', 'new_v_ln_b', 'new_v_ff1_in', 'new_v_ff1_out', 'new_v_ff2_in', 'new_v_ff2_out', 'new_v_w_in', 'new_v_conv4_w', 'new_v_conv4_b', 'new_v_w_rg', 'new_v_b_rg', 'new_v_w_ig', 'new_v_b_ig', 'new_v_lam', 'new_v_conv31_w', 'new_v_conv31_b', 'new_v_cln_g', 'new_v_cln_b', 'new_v_w_out', 'new_v_b_out']
TWIN_LEAF_KINDS = {'loss': 'loss', 'grad_x': 'grad_x', 'grad_c_ctx': 'grad_w', 'grad_w_ada': 'grad_w', 'grad_b_ada': 'grad_w', 'grad_ln_g': 'grad_w', 'grad_ln_b': 'grad_w', 'grad_ff1_in': 'grad_w', 'grad_ff1_out': 'grad_w', 'grad_ff2_in': 'grad_w', 'grad_ff2_out': 'grad_w', 'grad_w_in': 'grad_w', 'grad_conv4_w': 'grad_w', 'grad_conv4_b': 'grad_w', 'grad_w_rg': 'grad_w', 'grad_b_rg': 'grad_w', 'grad_w_ig': 'grad_w', 'grad_b_ig': 'grad_w', 'grad_lam': 'grad_w', 'grad_conv31_w': 'grad_w', 'grad_conv31_b': 'grad_w', 'grad_cln_g': 'grad_w', 'grad_cln_b': 'grad_w', 'grad_w_out': 'grad_w', 'grad_b_out': 'grad_w', 'delta_c_ctx': 'delta_w', 'delta_w_ada': 'delta_w', 'delta_b_ada': 'delta_w', 'delta_ln_g': 'delta_w', 'delta_ln_b': 'delta_w', 'delta_ff1_in': 'delta_w', 'delta_ff1_out': 'delta_w', 'delta_ff2_in': 'delta_w', 'delta_ff2_out': 'delta_w', 'delta_w_in': 'delta_w', 'delta_conv4_w': 'delta_w', 'delta_conv4_b': 'delta_w', 'delta_w_rg': 'delta_w', 'delta_b_rg': 'delta_w', 'delta_w_ig': 'delta_w', 'delta_b_ig': 'delta_w', 'delta_lam': 'delta_w', 'delta_conv31_w': 'delta_w', 'delta_conv31_b': 'delta_w', 'delta_cln_g': 'delta_w', 'delta_cln_b': 'delta_w', 'delta_w_out': 'delta_w', 'delta_b_out': 'delta_w', 'new_m_c_ctx': 'new_m', 'new_m_w_ada': 'new_m', 'new_m_b_ada': 'new_m', 'new_m_ln_g': 'new_m', 'new_m_ln_b': 'new_m', 'new_m_ff1_in': 'new_m', 'new_m_ff1_out': 'new_m', 'new_m_ff2_in': 'new_m', 'new_m_ff2_out': 'new_m', 'new_m_w_in': 'new_m', 'new_m_conv4_w': 'new_m', 'new_m_conv4_b': 'new_m', 'new_m_w_rg': 'new_m', 'new_m_b_rg': 'new_m', 'new_m_w_ig': 'new_m', 'new_m_b_ig': 'new_m', 'new_m_lam': 'new_m', 'new_m_conv31_w': 'new_m', 'new_m_conv31_b': 'new_m', 'new_m_cln_g': 'new_m', 'new_m_cln_b': 'new_m', 'new_m_w_out': 'new_m', 'new_m_b_out': 'new_m', 'new_v_c_ctx': 'new_v', 'new_v_w_ada': 'new_v', 'new_v_b_ada': 'new_v', 'new_v_ln_g': 'new_v', 'new_v_ln_b': 'new_v', 'new_v_ff1_in': 'new_v', 'new_v_ff1_out': 'new_v', 'new_v_ff2_in': 'new_v', 'new_v_ff2_out': 'new_v', 'new_v_w_in': 'new_v', 'new_v_conv4_w': 'new_v', 'new_v_conv4_b': 'new_v', 'new_v_w_rg': 'new_v', 'new_v_b_rg': 'new_v', 'new_v_w_ig': 'new_v', 'new_v_b_ig': 'new_v', 'new_v_lam': 'new_v', 'new_v_conv31_w': 'new_v', 'new_v_conv31_b': 'new_v', 'new_v_cln_g': 'new_v', 'new_v_cln_b': 'new_v', 'new_v_w_out': 'new_v', 'new_v_b_out': 'new_v'}


def _forward(args):
    return _fwd_reference(*[args[k] for k in FWD_PARAMS])


def _output_shape():
    def fwd():
        inp = _fwd_setup_inputs(0)
        return _fwd_reference(*[inp[k] for k in FWD_PARAMS])
    out = _jax.eval_shape(fwd)
    return out.shape, out.dtype

N_MICROBATCH = 1
ADAM_LR = 0.001
ADAM_B1 = 0.9
ADAM_B2 = 0.999
ADAM_EPS = 1e-08
ADAM_WD = 0.01
ADAM_STEP = 10
PER_EXAMPLE_BATCH_AXIS = {'x': 0, 'c': 0, 'ctx': 0, 'loss_target': 0}
SHARED_INPUTS = []
_WEIGHT_DTYPES = {'c_ctx': _jnp.float32, 'w_ada': _jnp.float32, 'b_ada': _jnp.float32, 'ln_g': _jnp.float32, 'ln_b': _jnp.float32, 'ff1_in': _jnp.float32, 'ff1_out': _jnp.float32, 'ff2_in': _jnp.float32, 'ff2_out': _jnp.float32, 'w_in': _jnp.float32, 'conv4_w': _jnp.float32, 'conv4_b': _jnp.float32, 'w_rg': _jnp.float32, 'b_rg': _jnp.float32, 'w_ig': _jnp.float32, 'b_ig': _jnp.float32, 'lam': _jnp.float32, 'conv31_w': _jnp.float32, 'conv31_b': _jnp.float32, 'cln_g': _jnp.float32, 'cln_b': _jnp.float32, 'w_out': _jnp.float32, 'b_out': _jnp.float32}
MOMENT_SCALE = {'c_ctx': 1.062134e-02, 'w_ada': 3.370231e-02, 'b_ada': 5.599142e-02, 'ln_g': 1.312258e+01, 'ln_b': 9.103241e-01, 'ff1_in': 2.938855e-03, 'ff1_out': 9.617194e-03, 'ff2_in': 2.921745e-03, 'ff2_out': 9.557552e-03, 'w_in': 3.192157e-02, 'conv4_w': 4.459740e-02, 'conv4_b': 1.303132e-01, 'w_rg': 2.236751e-03, 'b_rg': 3.559320e-03, 'w_ig': 4.477846e-03, 'b_ig': 8.735804e-03, 'lam': 9.021209e-03, 'conv31_w': 8.223191e-03, 'conv31_b': 2.303414e-02, 'cln_g': 1.157975e-02, 'cln_b': 1.257019e-02, 'w_out': 6.892845e-02, 'b_out': 4.783651e-02}


def _to_microbatches(a, axis):
    t = _jnp.moveaxis(a, axis, 0)
    t = t.reshape((N_MICROBATCH, t.shape[0] // N_MICROBATCH) + t.shape[1:])
    return _jnp.moveaxis(t, 1, axis + 1)


def setup_inputs(seed: int = 0) -> dict:
    inp = _fwd_setup_inputs(seed)
    key = _jax.random.fold_in(_jax.random.key(seed), 7919)
    shape, _ = _output_shape()
    out = dict(inp)
    out["loss_target"] = _jax.random.normal(_jax.random.fold_in(key, 0), shape, _jnp.float32)
    for i, name in enumerate(TWIN_WEIGHTS):
        w = inp[name].astype(_jnp.float32)
        if MOMENT_SCALE is None:
            s = _jnp.sqrt(_jnp.mean(_jnp.square(w)) + 1e-30)
        else:
            s = MOMENT_SCALE[name]
        km, kv = _jax.random.split(_jax.random.fold_in(key, i + 1))
        out[name] = w
        out["m_" + name] = s * _jax.random.normal(km, w.shape, _jnp.float32)
        out["v_" + name] = (s * s) * _jax.random.uniform(kv, w.shape, _jnp.float32, 0.5, 1.5)
    if N_MICROBATCH > 1:
        for name, axis in PER_EXAMPLE_BATCH_AXIS.items():
            out[name] = _to_microbatches(out[name], axis)
    return {'x': out['x'], 'c': out['c'], 'ctx': out['ctx'], 'c_ctx': out['c_ctx'], 'w_ada': out['w_ada'], 'b_ada': out['b_ada'], 'ln_g': out['ln_g'], 'ln_b': out['ln_b'], 'ff1_in': out['ff1_in'], 'ff1_out': out['ff1_out'], 'ff2_in': out['ff2_in'], 'ff2_out': out['ff2_out'], 'w_in': out['w_in'], 'conv4_w': out['conv4_w'], 'conv4_b': out['conv4_b'], 'w_rg': out['w_rg'], 'b_rg': out['b_rg'], 'w_ig': out['w_ig'], 'b_ig': out['b_ig'], 'lam': out['lam'], 'conv31_w': out['conv31_w'], 'conv31_b': out['conv31_b'], 'cln_g': out['cln_g'], 'cln_b': out['cln_b'], 'w_out': out['w_out'], 'b_out': out['b_out'], 'loss_target': out['loss_target'], 'm_c_ctx': out['m_c_ctx'], 'm_w_ada': out['m_w_ada'], 'm_b_ada': out['m_b_ada'], 'm_ln_g': out['m_ln_g'], 'm_ln_b': out['m_ln_b'], 'm_ff1_in': out['m_ff1_in'], 'm_ff1_out': out['m_ff1_out'], 'm_ff2_in': out['m_ff2_in'], 'm_ff2_out': out['m_ff2_out'], 'm_w_in': out['m_w_in'], 'm_conv4_w': out['m_conv4_w'], 'm_conv4_b': out['m_conv4_b'], 'm_w_rg': out['m_w_rg'], 'm_b_rg': out['m_b_rg'], 'm_w_ig': out['m_w_ig'], 'm_b_ig': out['m_b_ig'], 'm_lam': out['m_lam'], 'm_conv31_w': out['m_conv31_w'], 'm_conv31_b': out['m_conv31_b'], 'm_cln_g': out['m_cln_g'], 'm_cln_b': out['m_cln_b'], 'm_w_out': out['m_w_out'], 'm_b_out': out['m_b_out'], 'v_c_ctx': out['v_c_ctx'], 'v_w_ada': out['v_w_ada'], 'v_b_ada': out['v_b_ada'], 'v_ln_g': out['v_ln_g'], 'v_ln_b': out['v_ln_b'], 'v_ff1_in': out['v_ff1_in'], 'v_ff1_out': out['v_ff1_out'], 'v_ff2_in': out['v_ff2_in'], 'v_ff2_out': out['v_ff2_out'], 'v_w_in': out['v_w_in'], 'v_conv4_w': out['v_conv4_w'], 'v_conv4_b': out['v_conv4_b'], 'v_w_rg': out['v_w_rg'], 'v_b_rg': out['v_b_rg'], 'v_w_ig': out['v_w_ig'], 'v_b_ig': out['v_b_ig'], 'v_lam': out['v_lam'], 'v_conv31_w': out['v_conv31_w'], 'v_conv31_b': out['v_conv31_b'], 'v_cln_g': out['v_cln_g'], 'v_cln_b': out['v_cln_b'], 'v_w_out': out['v_w_out'], 'v_b_out': out['v_b_out']}


def _loss(weights, diff, rest, loss_target):
    with _jax.named_scope("forward"):
        args = {**rest, TWIN_DIFF_INPUT: diff, **{k: w.astype(_WEIGHT_DTYPES[k]) for k, w in weights.items()}}
        y = _forward(args)
    with _jax.named_scope("loss_head"):
        err = _jnp.square(y.astype(_jnp.float32) - loss_target)
        return 0.5 * _jnp.sum(_jnp.mean(err, axis=-1)) if err.ndim else 0.5 * err


def _adamw(w, g, m, v):
    m = ADAM_B1 * m + (1.0 - ADAM_B1) * g
    v = ADAM_B2 * v + (1.0 - ADAM_B2) * _jnp.square(g)
    m_hat = m / (1.0 - ADAM_B1 ** ADAM_STEP)
    v_hat = v / (1.0 - ADAM_B2 ** ADAM_STEP)
    delta = -ADAM_LR * (m_hat / (_jnp.sqrt(v_hat) + ADAM_EPS) + ADAM_WD * w)
    return delta, m, v


def reference(x, c, ctx, c_ctx, w_ada, b_ada, ln_g, ln_b, ff1_in, ff1_out, ff2_in, ff2_out, w_in, conv4_w, conv4_b, w_rg, b_rg, w_ig, b_ig, lam, conv31_w, conv31_b, cln_g, cln_b, w_out, b_out, loss_target, m_c_ctx, m_w_ada, m_b_ada, m_ln_g, m_ln_b, m_ff1_in, m_ff1_out, m_ff2_in, m_ff2_out, m_w_in, m_conv4_w, m_conv4_b, m_w_rg, m_b_rg, m_w_ig, m_b_ig, m_lam, m_conv31_w, m_conv31_b, m_cln_g, m_cln_b, m_w_out, m_b_out, v_c_ctx, v_w_ada, v_b_ada, v_ln_g, v_ln_b, v_ff1_in, v_ff1_out, v_ff2_in, v_ff2_out, v_w_in, v_conv4_w, v_conv4_b, v_w_rg, v_b_rg, v_w_ig, v_b_ig, v_lam, v_conv31_w, v_conv31_b, v_cln_g, v_cln_b, v_w_out, v_b_out):
    given = dict(x=x, c=c, ctx=ctx, c_ctx=c_ctx, w_ada=w_ada, b_ada=b_ada, ln_g=ln_g, ln_b=ln_b, ff1_in=ff1_in, ff1_out=ff1_out, ff2_in=ff2_in, ff2_out=ff2_out, w_in=w_in, conv4_w=conv4_w, conv4_b=conv4_b, w_rg=w_rg, b_rg=b_rg, w_ig=w_ig, b_ig=b_ig, lam=lam, conv31_w=conv31_w, conv31_b=conv31_b, cln_g=cln_g, cln_b=cln_b, w_out=w_out, b_out=b_out, loss_target=loss_target, m_c_ctx=m_c_ctx, m_w_ada=m_w_ada, m_b_ada=m_b_ada, m_ln_g=m_ln_g, m_ln_b=m_ln_b, m_ff1_in=m_ff1_in, m_ff1_out=m_ff1_out, m_ff2_in=m_ff2_in, m_ff2_out=m_ff2_out, m_w_in=m_w_in, m_conv4_w=m_conv4_w, m_conv4_b=m_conv4_b, m_w_rg=m_w_rg, m_b_rg=m_b_rg, m_w_ig=m_w_ig, m_b_ig=m_b_ig, m_lam=m_lam, m_conv31_w=m_conv31_w, m_conv31_b=m_conv31_b, m_cln_g=m_cln_g, m_cln_b=m_cln_b, m_w_out=m_w_out, m_b_out=m_b_out, v_c_ctx=v_c_ctx, v_w_ada=v_w_ada, v_b_ada=v_b_ada, v_ln_g=v_ln_g, v_ln_b=v_ln_b, v_ff1_in=v_ff1_in, v_ff1_out=v_ff1_out, v_ff2_in=v_ff2_in, v_ff2_out=v_ff2_out, v_w_in=v_w_in, v_conv4_w=v_conv4_w, v_conv4_b=v_conv4_b, v_w_rg=v_w_rg, v_b_rg=v_b_rg, v_w_ig=v_w_ig, v_b_ig=v_b_ig, v_lam=v_lam, v_conv31_w=v_conv31_w, v_conv31_b=v_conv31_b, v_cln_g=v_cln_g, v_cln_b=v_cln_b, v_w_out=v_w_out, v_b_out=v_b_out)
    weights = {n: given[n] for n in TWIN_WEIGHTS}
    shared = {n: given[n] for n in SHARED_INPUTS}
    per_example = {n: given[n] for n in ['x', 'c', 'ctx']}
    grad_fn = _jax.value_and_grad(_loss, argnums=(0, 1))

    def one_microbatch(ex, loss_target):
        ex = dict(ex)
        diff = ex.pop(TWIN_DIFF_INPUT)
        return grad_fn(weights, diff, {**shared, **ex}, loss_target)

    if N_MICROBATCH == 1:
        loss, (grad_w, grad_x) = one_microbatch(per_example, given["loss_target"])
    else:
        def body(carry, xs):
            loss_sum, grad_sum = carry
            l_k, (gw_k, gx_k) = one_microbatch(xs[0], xs[1])
            with _jax.named_scope("update"):
                return (loss_sum + l_k, _jax.tree.map(_jnp.add, grad_sum, gw_k)), gx_k

        init = (_jnp.zeros((), _jnp.float32), _jax.tree.map(_jnp.zeros_like, weights))
        (loss, grad_w), grad_x = _jax.lax.scan(body, init, (per_example, given["loss_target"]))
    with _jax.named_scope("update"):
        delta_w, new_m, new_v = {}, {}, {}
        for n in TWIN_WEIGHTS:
            delta_w[n], new_m[n], new_v[n] = _adamw(weights[n], grad_w[n], given["m_" + n], given["v_" + n])
    return (loss, grad_x, *[grad_w[n] for n in TWIN_WEIGHTS], *[delta_w[n] for n in TWIN_WEIGHTS],
            *[new_m[n] for n in TWIN_WEIGHTS], *[new_v[n] for n in TWIN_WEIGHTS])
```

```python
import functools

import jax
import jax.numpy as jnp
from jax import lax
from jax.experimental import pallas as pl
from jax.experimental.pallas import tpu as pltpu

F32 = jnp.float32
MXU_DTYPE = jnp.bfloat16
GRID_W = 64
RG_C = 8.0
LN_EPS = 1e-6
ADAM_LR, ADAM_B1, ADAM_B2, ADAM_EPS, ADAM_WD, ADAM_STEP = 0.001, 0.9, 0.999, 1e-08, 0.01, 10
N_CHIPS = 4
N_DEV = 8
VMEM_LIMIT_BYTES = 56 * 1024 * 1024
TM_IN, TN_IN = 768, 256
TM_OUT, TK_OUT = 384, 1408
TM_NT, TN_NT = 384, 1408
TM_DX, TK_DX = 384, 1408
TK_WG, TB_WG = 768, 1408
ADAM_BLOCK_ELEMS = 256 * 1024
MESH = pl.DeviceIdType.MESH


def _pick(total, target, mult=128):
    for d in range(min(total, target), 0, -1):
        if total % d == 0 and d % mult == 0:
            return d
    return total


def _params(sem=None):
    kw = dict(vmem_limit_bytes=VMEM_LIMIT_BYTES)
    if sem is not None:
        kw["dimension_semantics"] = sem
    return pltpu.CompilerParams(**kw)


def _sigmoid(x):
    return 1.0 / (1.0 + jnp.exp(-x))


def _gelu(x):
    k = 0.7978845608028654
    t = jnp.tanh(k * (x + 0.044715 * (x * x * x)))
    return 0.5 * x * (1.0 + t)


def _gelu_grad(x):
    k = 0.7978845608028654
    t = jnp.tanh(k * (x + 0.044715 * (x * x * x)))
    return 0.5 * (1.0 + t) + 0.5 * x * (1.0 - t * t) * (k * (1.0 + 3.0 * 0.044715 * x * x))


def _log1p(e):
    u = 1.0 + e
    return jnp.where(u == 1.0, e, jnp.log(u) * (e / jnp.where(u == 1.0, 1.0, u - 1.0)))


def _softplus(y):
    return jnp.maximum(y, 0.0) + _log1p(jnp.exp(-jnp.abs(y)))


def _expm1(x):
    series = x * (1.0 + x * (0.5 + x * (1.0 / 6.0 + x * (1.0 / 24.0 + x * (1.0 / 120.0 + x * (1.0 / 720.0))))))
    return jnp.where(jnp.abs(x) < 0.1, series, jnp.exp(x) - 1.0)


def _rows(i, tm):
    return i * tm + lax.broadcasted_iota(jnp.int32, (tm, 1), 0)


def _sel(isctx, ref):
    return jnp.where(isctx, ref[0:1, :], ref[1:2, :])


def _colsum(v):
    return jnp.sum(v, axis=0, keepdims=True)


def _in_proj(xhat, g_in, b_in, shift, scale, w, l, *, tc, swiglu, name):
    T, D = xhat.shape
    S, _, _, Ns = w.shape
    tm, tn = _pick(T, TM_IN, 8), _pick(Ns, TN_IN)
    nps = Ns // tn
    nI = T // tm
    nN = (S // 2 if swiglu else S) * nps

    def body(x_ref, g_ref, b_ref, sh_ref, sc_ref, w_ref, h_out, *rest):
        i, n = pl.program_id(0), pl.program_id(1)
        h_scr = rest[-1]

        @pl.when(n == 0)
        def _():
            s = x_ref[...] * g_ref[...] + b_ref[...]
            isctx = _rows(i, tm) < tc
            h = (s * (1.0 + _sel(isctx, sc_ref)) + _sel(isctx, sh_ref)).astype(MXU_DTYPE)
            h_scr[...] = h
            h_out[...] = h

        h = h_scr[...]
        if swiglu:
            gu_out, act_out = rest[0], rest[1]
            gt = jnp.dot(h, w_ref[0], preferred_element_type=F32)
            ut = jnp.dot(h, w_ref[1], preferred_element_type=F32)
            gu_out[0] = gt.astype(MXU_DTYPE)
            gu_out[1] = ut.astype(MXU_DTYPE)
            act_out[...] = ((gt * _sigmoid(gt)) * ut).astype(MXU_DTYPE)
        else:
            rest[0][...] = jnp.dot(h, w_ref[...], preferred_element_type=F32)

    vec = pl.BlockSpec((1, D), lambda i, n: (0, 0))
    vec2 = pl.BlockSpec((2, D), lambda i, n: (0, 0))
    in_specs = [pl.BlockSpec((tm, D), lambda i, n: (i, 0)), vec, vec, vec2, vec2]
    h_shape = jax.ShapeDtypeStruct((T, D), MXU_DTYPE)
    h_spec = pl.BlockSpec((tm, D), lambda i, n: (i, 0))
    if swiglu:
        F = (S // 2) * Ns
        wv = w.reshape(2, S // 2, *w.shape[1:])
        in_specs.append(pl.BlockSpec((2, None, None, D, tn), lambda i, n: (0, n // nps, l, 0, n % nps)))
        out_shape = (h_shape, jax.ShapeDtypeStruct((2, T, F), MXU_DTYPE), jax.ShapeDtypeStruct((T, F), MXU_DTYPE))
        out_specs = (h_spec, pl.BlockSpec((2, tm, tn), lambda i, n: (0, i, n)), pl.BlockSpec((tm, tn), lambda i, n: (i, n)))
    else:
        wv = w
        in_specs.append(pl.BlockSpec((None, None, D, tn), lambda i, n: (n // nps, l, 0, n % nps)))
        out_shape = (h_shape, jax.ShapeDtypeStruct((S, T, Ns), F32))
        out_specs = (h_spec, pl.BlockSpec((None, tm, tn), lambda i, n: (n // nps, i, n % nps)))
    return pl.pallas_call(
        body, name=name, grid=(nI, nN), in_specs=in_specs, out_specs=out_specs, out_shape=out_shape,
        scratch_shapes=[pltpu.VMEM((tm, D), MXU_DTYPE)], compiler_params=_params(("arbitrary", "arbitrary")),
    )(xhat, g_in, b_in, shift, scale, wv)


def _out_proj_ln(a, w, l, bias, xin, g_in, b_in, gvec, gscale, alpha, *, tc, name):
    T, K = a.shape
    S, _, Ks, D = w.shape
    tm, tk = _pick(T, TM_OUT, 8), _pick(Ks, TK_OUT)
    kps = Ks // tk
    nK = S * kps

    def body(a_ref, w_ref, bias_ref, xin_ref, gi_ref, bi_ref, gv_ref, xh_out, rstd_out, f_out, acc):
        i, k = pl.program_id(0), pl.program_id(1)

        @pl.when(k == 0)
        def _():
            acc[...] = jnp.zeros_like(acc)

        acc[...] += jnp.dot(a_ref[...], w_ref[...], preferred_element_type=F32)

        @pl.when(k == nK - 1)
        def _():
            f = acc[...] + bias_ref[...]
            f_out[...] = f.astype(MXU_DTYPE)
            s = xin_ref[...] * gi_ref[...] + bi_ref[...]
            gv = gscale * _sel(_rows(i, tm) < tc, gv_ref)
            r = alpha * s + gv * f
            mu = jnp.mean(r, axis=-1, keepdims=True)
            d = r - mu
            var = jnp.mean(d * d, axis=-1, keepdims=True)
            rstd = lax.rsqrt(var + LN_EPS)
            xh_out[...] = d * rstd
            rstd_out[...] = rstd

    vec = pl.BlockSpec((1, D), lambda i, k: (0, 0))
    row = pl.BlockSpec((tm, D), lambda i, k: (i, 0))
    return pl.pallas_call(
        body, name=name, grid=(T // tm, nK),
        in_specs=[pl.BlockSpec((tm, tk), lambda i, k: (i, k)),
                  pl.BlockSpec((None, None, tk, D), lambda i, k: (k // kps, l, k % kps, 0)),
                  vec, row, vec, vec, pl.BlockSpec((2, D), lambda i, k: (0, 0))],
        out_specs=(row, pl.BlockSpec((tm, 1), lambda i, k: (i, 0)), row),
        out_shape=(jax.ShapeDtypeStruct((T, D), F32), jax.ShapeDtypeStruct((T, 1), F32),
                   jax.ShapeDtypeStruct((T, D), MXU_DTYPE)),
        scratch_shapes=[pltpu.VMEM((tm, D), F32)], compiler_params=_params(("arbitrary", "arbitrary")),
    )(a, w, bias, xin, g_in, b_in, gvec)


def _ln_bwd(ds, xhat, rstd, g_ln, f, gvec, gscale, alpha, *, tc, name):
    T, D = ds.shape
    tm = _pick(T, TM_OUT, 8)

    def body(ds_ref, xh_ref, rs_ref, gl_ref, f_ref, gv_ref, dy_out, dres_out, dgl_out, dbl_out, dgv_out, dbias_out):
        i = pl.program_id(0)

        @pl.when(i == 0)
        def _():
            dgl_out[...] = jnp.zeros_like(dgl_out)
            dbl_out[...] = jnp.zeros_like(dbl_out)
            dgv_out[...] = jnp.zeros_like(dgv_out)
            dbias_out[...] = jnp.zeros_like(dbias_out)

        dsv, xh = ds_ref[...], xh_ref[...]
        dgl_out[...] += _colsum(dsv * xh)
        dbl_out[...] += _colsum(dsv)
        dxh = dsv * gl_ref[...]
        m1 = jnp.mean(dxh, axis=-1, keepdims=True)
        m2 = jnp.mean(dxh * xh, axis=-1, keepdims=True)
        dr = rs_ref[...] * (dxh - m1 - xh * m2)
        dres_out[...] = alpha * dr
        isctx = _rows(i, tm) < tc
        dyv = (gscale * _sel(isctx, gv_ref)) * dr
        dy_out[...] = dyv.astype(MXU_DTYPE)
        dbias_out[...] += _colsum(dyv)
        p = gscale * (dr * f_ref[...].astype(F32))
        dgv_out[0:1, :] += _colsum(jnp.where(isctx, p, 0.0))
        dgv_out[1:2, :] += _colsum(jnp.where(isctx, 0.0, p))

    vec = pl.BlockSpec((1, D), lambda i: (0, 0))
    vec2 = pl.BlockSpec((2, D), lambda i: (0, 0))
    row = pl.BlockSpec((tm, D), lambda i: (i, 0))
    return pl.pallas_call(
        body, name=name, grid=(T // tm,),
        in_specs=[row, row, pl.BlockSpec((tm, 1), lambda i: (i, 0)), vec, row, vec2],
        out_specs=(row, row, vec, vec, vec2, vec),
        out_shape=(jax.ShapeDtypeStruct((T, D), MXU_DTYPE), jax.ShapeDtypeStruct((T, D), F32),
                   jax.ShapeDtypeStruct((1, D), F32), jax.ShapeDtypeStruct((1, D), F32),
                   jax.ShapeDtypeStruct((2, D), F32), jax.ShapeDtypeStruct((1, D), F32)),
        compiler_params=_params(("arbitrary",)),
    )(ds, xhat, rstd, g_ln, f, gvec)


def _nt(dy, w, l, gate_up, *, name):
    T, D = dy.shape
    S, _, Ks, _ = w.shape
    tm, tn = _pick(T, TM_NT, 8), _pick(Ks, TN_NT)
    nps = Ks // tn
    nN = S * nps
    F = S * Ks

    def body(dy_ref, w_ref, *rest):
        d = lax.dot_general(dy_ref[...], w_ref[...], (((1,), (1,)), ((), ())), preferred_element_type=F32)
        if gate_up is None:
            rest[0][...] = d
        else:
            gu_ref, dg_out = rest
            g, u = gu_ref[0].astype(F32), gu_ref[1].astype(F32)
            sg = _sigmoid(g)
            dg_out[0] = (d * u * (sg * (1.0 + g * (1.0 - sg)))).astype(MXU_DTYPE)
            dg_out[1] = (d * (g * sg)).astype(MXU_DTYPE)

    in_specs = [pl.BlockSpec((tm, D), lambda i, n: (i, 0)),
                pl.BlockSpec((None, None, tn, D), lambda i, n: (n // nps, l, n % nps, 0))]
    args = [dy, w]
    if gate_up is None:
        out_shape = jax.ShapeDtypeStruct((T, F), F32)
        out_specs = pl.BlockSpec((tm, tn), lambda i, n: (i, n))
    else:
        Ns = 2 * F // S
        q = Ns // tn
        in_specs.append(pl.BlockSpec((2, tm, tn), lambda i, n: (0, i, n)))
        args.append(gate_up)
        out_shape = jax.ShapeDtypeStruct((2, S // 2, T, Ns), MXU_DTYPE)
        out_specs = pl.BlockSpec((2, None, tm, tn), lambda i, n: (0, n // q, i, n % q))
    out = pl.pallas_call(
        body, name=name, grid=(T // tm, nN), in_specs=in_specs, out_specs=out_specs, out_shape=out_shape,
        compiler_params=_params(("arbitrary", "arbitrary")),
    )(*args)
    return out if gate_up is None else out.reshape(S, T, out.shape[-1])


def _dx_modbwd(dg, w, l, dres, xhat_in, g_in, b_in, scale, *, tc, name):
    S, T, Ns = dg.shape
    D = w.shape[2]
    tm, tk = _pick(T, TM_DX, 8), _pick(Ns, TK_DX)
    kps = Ns // tk
    nK = S * kps

    def body(dg_ref, w_ref, dres_ref, xin_ref, gi_ref, bi_ref, sc_ref, ds_out, dsc_out, dsh_out, acc):
        i, k = pl.program_id(0), pl.program_id(1)

        @pl.when((i == 0) & (k == 0))
        def _():
            dsc_out[...] = jnp.zeros_like(dsc_out)
            dsh_out[...] = jnp.zeros_like(dsh_out)

        @pl.when(k == 0)
        def _():
            acc[...] = jnp.zeros_like(acc)

        acc[...] += lax.dot_general(dg_ref[...], w_ref[...], (((1,), (1,)), ((), ())), preferred_element_type=F32)

        @pl.when(k == nK - 1)
        def _():
            dh = acc[...]
            isctx = _rows(i, tm) < tc
            ds_out[...] = dres_ref[...] + dh * (1.0 + _sel(isctx, sc_ref))
            s = xin_ref[...] * gi_ref[...] + bi_ref[...]
            p = dh * s
            dsc_out[0:1, :] += _colsum(jnp.where(isctx, p, 0.0))
            dsc_out[1:2, :] += _colsum(jnp.where(isctx, 0.0, p))
            dsh_out[0:1, :] += _colsum(jnp.where(isctx, dh, 0.0))
            dsh_out[1:2, :] += _colsum(jnp.where(isctx, 0.0, dh))

    vec = pl.BlockSpec((1, D), lambda i, k: (0, 0))
    vec2 = pl.BlockSpec((2, D), lambda i, k: (0, 0))
    row = pl.BlockSpec((tm, D), lambda i, k: (i, 0))
    return pl.pallas_call(
        body, name=name, grid=(T // tm, nK),
        in_specs=[pl.BlockSpec((None, tm, tk), lambda i, k: (k // kps, i, k % kps)),
                  pl.BlockSpec((None, None, D, tk), lambda i, k: (k // kps, l, 0, k % kps)),
                  row, row, vec, vec, vec2],
        out_specs=(row, vec2, vec2),
        out_shape=(jax.ShapeDtypeStruct((T, D), F32), jax.ShapeDtypeStruct((2, D), F32), jax.ShapeDtypeStruct((2, D), F32)),
        scratch_shapes=[pltpu.VMEM((tm, D), F32)], compiler_params=_params(("arbitrary", "arbitrary")),
    )(dg, w, dres, xhat_in, g_in, b_in, scale)


def _wgrad(a, b, l, n_layers, prev, *, cols_sharded, name):
    T = a.shape[0]
    tk = _pick(T, TK_WG, 8)
    if cols_sharded:
        S, _, Ns = b.shape
        D = a.shape[1]
        tb = _pick(Ns, TB_WG)
        grid = (S, Ns // tb, T // tk)
        in_specs = [pl.BlockSpec((tk, D), lambda s, j, k: (k, 0)), pl.BlockSpec((None, tk, tb), lambda s, j, k: (s, k, j))]
        out_shape = jax.ShapeDtypeStruct((n_layers, S, D, Ns), F32)
        out_specs = pl.BlockSpec((None, None, D, tb), lambda s, j, k: (l, s, 0, j))
    else:
        D = b.shape[1]
        S = N_CHIPS
        Ks = a.shape[1] // S
        ta = _pick(Ks, TB_WG)
        q = Ks // ta
        grid = (S, q, T // tk)
        in_specs = [pl.BlockSpec((tk, ta), lambda s, j, k: (k, s * q + j)), pl.BlockSpec((tk, D), lambda s, j, k: (k, 0))]
        out_shape = jax.ShapeDtypeStruct((n_layers, S, Ks, D), F32)
        out_specs = pl.BlockSpec((None, None, ta, D), lambda s, j, k: (l, s, j, 0))

    def body(a_ref, b_ref, *rest):
        o_ref = rest[-1]

        @pl.when(pl.program_id(2) == 0)
        def _():
            o_ref[...] = jnp.zeros_like(o_ref)

        o_ref[...] += lax.dot_general(a_ref[...], b_ref[...], (((0,), (0,)), ((), ())), preferred_element_type=F32)

    args, aliases = [a, b], {}
    if prev is not None:
        in_specs.append(pl.BlockSpec(memory_space=pl.ANY))
        args.append(prev)
        aliases = {2: 0}
    return pl.pallas_call(
        body, name=name, grid=grid, in_specs=in_specs, out_specs=out_specs, out_shape=out_shape,
        input_output_aliases=aliases, compiler_params=_params(("arbitrary", "arbitrary", "arbitrary")),
    )(*args)


def _halo_specs(tms, T, C, slot=None):
    r8 = tms // 8
    last8 = T // 8 - 1
    if slot is None:
        return (pl.BlockSpec((8, C), lambda i: (jnp.maximum(i * r8 - 1, 0), 0)),
                pl.BlockSpec((tms, C), lambda i: (i, 0)),
                pl.BlockSpec((8, C), lambda i: (jnp.minimum((i + 1) * r8, last8), 0)))
    return (pl.BlockSpec((None, 8, C), lambda i: (slot, jnp.maximum(i * r8 - 1, 0), 0)),
            pl.BlockSpec((None, tms, C), lambda i: (slot, i, 0)),
            pl.BlockSpec((None, 8, C), lambda i: (slot, jnp.minimum((i + 1) * r8, last8), 0)))


def _extended(prev_ref, cur, next_ref, i, n_tiles):
    first = (i == 0) | (i == 1)
    last = (i == 0) | (i == n_tiles - 1)
    pv = jnp.where(first, 0.0, prev_ref[...])
    nx = jnp.where(last, 0.0, next_ref[...])
    return jnp.concatenate([pv, cur, nx], axis=0)


def _shifted(ext, o, tms):
    n = tms + 16
    return pltpu.roll(ext, (-o) % n, 0)[8:8 + tms]


def _conv4_fwd(z, w4, b4, *, tms, name):
    S, T, C = z.shape
    nT = T // tms

    def body(p_ref, c_ref, n_ref, w_ref, b_ref, o_ref):
        i = pl.program_id(0)
        ext = _extended(p_ref, c_ref[...], n_ref, i, nT)
        acc = jnp.zeros((tms, C), F32) + b_ref[...]
        for k in range(4):
            acc = acc + w_ref[k:k + 1, :] * _shifted(ext, k - 2, tms)
        o_ref[...] = acc

    return pl.pallas_call(
        body, name=name, grid=(nT,),
        in_specs=[*_halo_specs(tms, T, C, 0), pl.BlockSpec((4, C), lambda i: (0, 0)), pl.BlockSpec((1, C), lambda i: (0, 0))],
        out_specs=pl.BlockSpec((tms, C), lambda i: (i, 0)), out_shape=jax.ShapeDtypeStruct((T, C), F32),
        compiler_params=_params(("arbitrary",)),
    )(z, z, z, w4, b4)


def _conv4_bwd(dxa, dxb, z, w4, dz, *, tms, name):
    S, T, C = z.shape
    nT = T // tms

    def body(pa, ca, na, pb, cb, nb, px, cx, nx, w_ref, dz_in, dz_out, dw_out, db_out):
        i = pl.program_id(0)

        @pl.when(i == 0)
        def _():
            dw_out[...] = jnp.zeros_like(dw_out)
            db_out[...] = jnp.zeros_like(db_out)

        dcur = ca[...] + cb[...]
        first = (i == 0) | (i == 1)
        last = (i == 0) | (i == nT - 1)
        dext = jnp.concatenate([jnp.where(first, 0.0, pa[...] + pb[...]), dcur, jnp.where(last, 0.0, na[...] + nb[...])], axis=0)
        xext = _extended(px, cx[...], nx, i, nT)
        acc = jnp.zeros((tms, C), F32)
        for k in range(4):
            acc = acc + w_ref[k:k + 1, :] * _shifted(dext, 2 - k, tms)
            dw_out[k:k + 1, :] += _colsum(dcur * _shifted(xext, k - 2, tms))
        db_out[...] += _colsum(dcur)
        dz_out[...] = acc.astype(MXU_DTYPE)

    h = _halo_specs(tms, T, C)
    return pl.pallas_call(
        body, name=name, grid=(nT,),
        in_specs=[*h, *h, *_halo_specs(tms, T, C, 0), pl.BlockSpec((4, C), lambda i: (0, 0)), pl.BlockSpec(memory_space=pl.ANY)],
        out_specs=(pl.BlockSpec((None, tms, C), lambda i: (0, i, 0)), pl.BlockSpec((4, C), lambda i: (0, 0)),
                   pl.BlockSpec((1, C), lambda i: (0, 0))),
        out_shape=(jax.ShapeDtypeStruct(dz.shape, dz.dtype), jax.ShapeDtypeStruct((4, C), F32), jax.ShapeDtypeStruct((1, C), F32)),
        input_output_aliases={10: 0}, compiler_params=_params(("arbitrary",)),
    )(dxa, dxa, dxa, dxb, dxb, dxb, z, z, z, w4, dz)


def _scan_tile(a, b, rev, n):
    rows = lax.broadcasted_iota(jnp.int32, (n, 1), 0)
    sft = 1
    while sft < n:
        if rev:
            a_sh, b_sh, valid = pltpu.roll(a, n - sft, 0), pltpu.roll(b, n - sft, 0), rows < n - sft
        else:
            a_sh, b_sh, valid = pltpu.roll(a, sft, 0), pltpu.roll(b, sft, 0), rows >= sft
        b = a * jnp.where(valid, b_sh, 0.0) + b
        a = a * jnp.where(valid, a_sh, 1.0)
        sft *= 2
    return a, b


def _scan_order(i, rev, nT):
    if not rev:
        return i
    return jnp.where(i == 0, 0, nT - i)


def _gate_pre(xc, w_ref, hd):
    parts = [jnp.dot(xc[:, h * hd:(h + 1) * hd].astype(MXU_DTYPE), w_ref[h], preferred_element_type=F32)
             for h in range(w_ref.shape[0])]
    return jnp.concatenate(parts, axis=1)


def _lru_fwd(xc, wr, wi, br, bi, lam, *, rev, tms, name):
    T, C = xc.shape
    nH, hd, _ = wr.shape
    nT = T // tms

    def body(xc_ref, wr_ref, wi_ref, br_ref, bi_ref, lam_ref, h_out, r_out, i_out, carry):
        @pl.when(pl.program_id(0) == 0)
        def _():
            carry[...] = jnp.zeros_like(carry)

        x = xc_ref[...]
        r = _sigmoid(_gate_pre(x, wr_ref, hd) + br_ref[...])
        ig = _sigmoid(_gate_pre(x, wi_ref, hd) + bi_ref[...])
        log_a = (-RG_C * r) * _softplus(-lam_ref[...])
        a = jnp.exp(log_a)
        b = jnp.sqrt(-_expm1(2.0 * log_a)) * (ig * x)
        A, B = _scan_tile(a, b, rev, tms)
        h = A * carry[...] + B
        carry[...] = h[0:1, :] if rev else h[tms - 1:tms, :]
        h_out[...] = h
        r_out[...] = r
        i_out[...] = ig

    tile = pl.BlockSpec((tms, C), lambda i: (_scan_order(i, rev, nT), 0))
    wspec = pl.BlockSpec((nH, hd, hd), lambda i: (0, 0, 0))
    vec = pl.BlockSpec((1, C), lambda i: (0, 0))
    shp = jax.ShapeDtypeStruct((T, C), F32)
    return pl.pallas_call(
        body, name=name, grid=(nT,), in_specs=[tile, wspec, wspec, vec, vec, vec], out_specs=(tile, tile, tile),
        out_shape=(shp, shp, shp), scratch_shapes=[pltpu.VMEM((1, C), F32)], compiler_params=_params(("arbitrary",)),
    )(xc, wr, wi, br, bi, lam)


def _lru_bwd(dh, h, r, ig, xc, wr, wi, lam, *, rev, tms, name):
    T, C = xc.shape
    nH, hd, _ = wr.shape
    nT = T // tms
    r8 = tms // 8

    def tile_of(ip):
        return _scan_order(nT - 1 - ip, rev, nT)

    def halo_of(ip):
        i = nT - 1 - ip
        if not rev:
            return jnp.maximum(i * r8 - 1, 0)
        return jnp.where(i <= 1, 0, (nT - i + 1) * r8)

    def body(dh_ref, h_ref, hh_ref, r_ref, i_ref, xc_ref, wr_ref, wi_ref, lam_ref,
             dxc_out, dwr_out, dwi_out, dbr_out, dbi_out, dlam_out, ucarry):
        ip = pl.program_id(0)
        pos = nT - 1 - ip

        @pl.when(ip == 0)
        def _():
            ucarry[...] = jnp.zeros_like(ucarry)
            for o in (dwr_out, dwi_out, dbr_out, dbi_out, dlam_out):
                o[...] = jnp.zeros_like(o)

        rows = lax.broadcasted_iota(jnp.int32, (tms, 1), 0)
        x, rr, ii, hh, dhv = xc_ref[...], r_ref[...], i_ref[...], h_ref[...], dh_ref[...]
        sp = _softplus(-lam_ref[...])
        log_a = (-RG_C * rr) * sp
        a = jnp.exp(log_a)
        a2 = jnp.exp(2.0 * log_a)
        s = jnp.sqrt(-_expm1(2.0 * log_a))
        A, B = _scan_tile(a, a * dhv, not rev, tms)
        u_in = ucarry[...]
        u = A * u_in + B
        if rev:
            u_next = jnp.where(rows == 0, u_in, pltpu.roll(u, 1, 0))
            ucarry[...] = u[tms - 1:tms, :]
            h_halo = jnp.where(pos == 0, 0.0, hh_ref[0:1, :])
            h_prev = jnp.where(rows == tms - 1, h_halo, pltpu.roll(hh, tms - 1, 0))
        else:
            u_next = jnp.where(rows == tms - 1, u_in, pltpu.roll(u, tms - 1, 0))
            ucarry[...] = u[0:1, :]
            h_halo = jnp.where(pos == 0, 0.0, hh_ref[7:8, :])
            h_prev = jnp.where(rows == 0, h_halo, pltpu.roll(hh, 1, 0))
        g = dhv + u_next
        da = g * h_prev
        gated = ii * x
        dgated = g * s
        dlog_a = da * a - (g * gated) * a2 / s
        di = dgated * x
        dr = dlog_a * (-RG_C * sp)
        dlam_out[...] += _colsum(dlog_a * (-RG_C * rr)) * (-_sigmoid(-lam_ref[...]))
        dpr = dr * rr * (1.0 - rr)
        dpi = di * ii * (1.0 - ii)
        dbr_out[...] += _colsum(dpr)
        dbi_out[...] += _colsum(dpi)
        parts = []
        for hx in range(nH):
            sl = slice(hx * hd, (hx + 1) * hd)
            xh, dprh, dpih = x[:, sl].astype(MXU_DTYPE), dpr[:, sl].astype(MXU_DTYPE), dpi[:, sl].astype(MXU_DTYPE)
            nt = (((1,), (1,)), ((), ()))
            tn = (((0,), (0,)), ((), ()))
            parts.append(lax.dot_general(dprh, wr_ref[hx], nt, preferred_element_type=F32)
                         + lax.dot_general(dpih, wi_ref[hx], nt, preferred_element_type=F32))
            dwr_out[hx] += lax.dot_general(xh, dprh, tn, preferred_element_type=F32)
            dwi_out[hx] += lax.dot_general(xh, dpih, tn, preferred_element_type=F32)
        dxc_out[...] = dgated * ii + jnp.concatenate(parts, axis=1)

    tile = pl.BlockSpec((tms, C), lambda ip: (tile_of(ip), 0))
    wspec = pl.BlockSpec((nH, hd, hd), lambda ip: (0, 0, 0))
    vec = pl.BlockSpec((1, C), lambda ip: (0, 0))
    wshape = jax.ShapeDtypeStruct((nH, hd, hd), F32)
    vshape = jax.ShapeDtypeStruct((1, C), F32)
    return pl.pallas_call(
        body, name=name, grid=(nT,),
        in_specs=[tile, tile, pl.BlockSpec((8, C), lambda ip: (halo_of(ip), 0)), tile, tile, tile, wspec, wspec, vec],
        out_specs=(tile, wspec, wspec, vec, vec, vec),
        out_shape=(jax.ShapeDtypeStruct((T, C), F32), wshape, wshape, vshape, vshape, vshape),
        scratch_shapes=[pltpu.VMEM((1, C), F32)], compiler_params=_params(("arbitrary",)),
    )(dh, h, h, r, ig, xc, wr, wi, lam)


def _seg_pos(i, tms):
    seg = jnp.where(i == 0, tms, GRID_W)
    rows = lax.broadcasted_iota(jnp.int32, (tms, 1), 0)
    return rows & (seg - 1), seg


def _convmod_fwd(hf, hb, z, w31, b31, clg, clb, *, tms, name):
    S, T, C = z.shape
    K = w31.shape[0]
    nT = T // tms

    def body(hf_ref, hb_ref, gr_ref, cv_ref, cg_ref, w_ref, b_ref, g_ref, bb_ref, y_out, uc_out):
        i = pl.program_id(0)
        y_out[:, 0:C] = ((hf_ref[...] + hb_ref[...]) * _gelu(gr_ref[...])).astype(MXU_DTYPE)
        u = cv_ref[...] * _sigmoid(cg_ref[...])
        pos, seg = _seg_pos(i, tms)
        acc = jnp.zeros((tms, C), F32) + b_ref[...]
        for k in range(K):
            o = k - K // 2
            valid = (pos + o >= 0) & (pos + o < seg)
            acc = acc + w_ref[k:k + 1, :] * jnp.where(valid, pltpu.roll(u, (-o) % tms, 0), 0.0)
        uc_out[...] = acc
        mu = jnp.mean(acc, axis=-1, keepdims=True)
        d = acc - mu
        var = jnp.mean(d * d, axis=-1, keepdims=True)
        yl = d * lax.rsqrt(var + LN_EPS) * g_ref[...] + bb_ref[...]
        y_out[:, C:2 * C] = (yl * _sigmoid(yl)).astype(MXU_DTYPE)

    tile = pl.BlockSpec((tms, C), lambda i: (i, 0))
    vec = pl.BlockSpec((1, C), lambda i: (0, 0))
    zs = [pl.BlockSpec((None, tms, C), functools.partial(lambda i, s: (s, i, 0), s=s)) for s in (1, 2, 3)]
    return pl.pallas_call(
        body, name=name, grid=(nT,),
        in_specs=[tile, tile, *zs, pl.BlockSpec((K, C), lambda i: (0, 0)), vec, vec, vec],
        out_specs=(pl.BlockSpec((tms, 2 * C), lambda i: (i, 0)), tile),
        out_shape=(jax.ShapeDtypeStruct((T, 2 * C), MXU_DTYPE), jax.ShapeDtypeStruct((T, C), F32)),
        compiler_params=_params(("arbitrary",)),
    )(hf, hb, z, z, z, w31, b31, clg, clb)


def _convmod_bwd(dymix, hf, hb, z, uc, w31, clg, clb, *, tms, name):
    S, T, C = z.shape
    K = w31.shape[0]
    nT = T // tms

    def body(dy_ref, hf_ref, hb_ref, gr_ref, cv_ref, cg_ref, uc_ref, w_ref, g_ref, bb_ref,
             dhs_out, dz_out, dw_out, db_out, dg_out, dbb_out):
        i = pl.program_id(0)

        @pl.when(i == 0)
        def _():
            for o in (dw_out, db_out, dg_out, dbb_out):
                o[...] = jnp.zeros_like(o)

        dyr, dyc = dy_ref[:, 0:C], dy_ref[:, C:2 * C]
        gr = gr_ref[...]
        dhs_out[...] = dyr * _gelu(gr)
        dz_out[0] = jnp.zeros((tms, C), MXU_DTYPE)
        dz_out[1] = (dyr * (hf_ref[...] + hb_ref[...]) * _gelu_grad(gr)).astype(MXU_DTYPE)
        ucv = uc_ref[...]
        mu = jnp.mean(ucv, axis=-1, keepdims=True)
        d = ucv - mu
        var = jnp.mean(d * d, axis=-1, keepdims=True)
        rstd = lax.rsqrt(var + LN_EPS)
        xh = d * rstd
        yl = xh * g_ref[...] + bb_ref[...]
        sg = _sigmoid(yl)
        dyl = dyc * (sg * (1.0 + yl * (1.0 - sg)))
        dg_out[...] += _colsum(dyl * xh)
        dbb_out[...] += _colsum(dyl)
        dxh = dyl * g_ref[...]
        m1 = jnp.mean(dxh, axis=-1, keepdims=True)
        m2 = jnp.mean(dxh * xh, axis=-1, keepdims=True)
        duc = rstd * (dxh - m1 - xh * m2)
        db_out[...] += _colsum(duc)
        cv, sc = cv_ref[...], _sigmoid(cg_ref[...])
        u = cv * sc
        pos, seg = _seg_pos(i, tms)
        du = jnp.zeros((tms, C), F32)
        for k in range(K):
            o = k - K // 2
            fwd_ok = (pos + o >= 0) & (pos + o < seg)
            bwd_ok = (pos - o >= 0) & (pos - o < seg)
            du = du + w_ref[k:k + 1, :] * jnp.where(bwd_ok, pltpu.roll(duc, o % tms, 0), 0.0)
            dw_out[k:k + 1, :] += _colsum(duc * jnp.where(fwd_ok, pltpu.roll(u, (-o) % tms, 0), 0.0))
        dz_out[2] = (du * sc).astype(MXU_DTYPE)
        dz_out[3] = (du * cv * sc * (1.0 - sc)).astype(MXU_DTYPE)

    tile = pl.BlockSpec((tms, C), lambda i: (i, 0))
    vec = pl.BlockSpec((1, C), lambda i: (0, 0))
    kc = pl.BlockSpec((K, C), lambda i: (0, 0))
    zs = [pl.BlockSpec((None, tms, C), functools.partial(lambda i, s: (s, i, 0), s=s)) for s in (1, 2, 3)]
    vshape = jax.ShapeDtypeStruct((1, C), F32)
    return pl.pallas_call(
        body, name=name, grid=(nT,),
        in_specs=[pl.BlockSpec((tms, 2 * C), lambda i: (i, 0)), tile, tile, *zs, tile, kc, vec, vec],
        out_specs=(tile, pl.BlockSpec((S, tms, C), lambda i: (0, i, 0)), kc, vec, vec, vec),
        out_shape=(jax.ShapeDtypeStruct((T, C), F32), jax.ShapeDtypeStruct((S, T, C), MXU_DTYPE),
                   jax.ShapeDtypeStruct((K, C), F32), vshape, vshape, vshape),
        compiler_params=_params(("arbitrary",)),
    )(dymix, hf, hb, z, z, z, uc, w31, clg, clb)


def _loss_head(xhat, g, b, target, *, tc, tms, name):
    T, D = xhat.shape
    nT = T // tms
    nc = tc // tms

    def body(x_ref, g_ref, b_ref, t_ref, dy_out, loss_out):
        i = pl.program_id(0)

        @pl.when(i == 0)
        def _():
            loss_out[...] = jnp.zeros_like(loss_out)

        err = jnp.where(i < nc, 0.0, x_ref[...] * g_ref[...] + b_ref[...] - t_ref[...])
        dy_out[...] = err / D
        loss_out[...] += 0.5 * jnp.sum(jnp.sum(err * err, axis=-1, keepdims=True) / D)

    vec = pl.BlockSpec((1, D), lambda i: (0, 0))
    tile = pl.BlockSpec((tms, D), lambda i: (i, 0))
    return pl.pallas_call(
        body, name=name, grid=(nT,),
        in_specs=[tile, vec, vec, pl.BlockSpec((tms, D), lambda i: (jnp.maximum(i - nc, 0), 0))],
        out_specs=(tile, pl.BlockSpec((8, 128), lambda i: (0, 0))),
        out_shape=(jax.ShapeDtypeStruct((T, D), F32), jax.ShapeDtypeStruct((8, 128), F32)),
        compiler_params=_params(("arbitrary",)),
    )(xhat, g, b, target)


def _ada_fwd(s16, w_ada, b_cols, *, name):
    L, D, Na = w_ada.shape
    tn = _pick(Na, 512)

    def body(s_ref, w_ref, b_ref, o_ref):
        o_ref[...] = jnp.dot(s_ref[...], w_ref[...].astype(MXU_DTYPE), preferred_element_type=F32) + b_ref[...]

    return pl.pallas_call(
        body, name=name, grid=(L, Na // tn),
        in_specs=[pl.BlockSpec((16, D), lambda l, n: (0, 0)), pl.BlockSpec((None, D, tn), lambda l, n: (l, 0, n)),
                  pl.BlockSpec((None, 1, tn), lambda l, n: (l, 0, n))],
        out_specs=pl.BlockSpec((None, 16, tn), lambda l, n: (l, 0, n)),
        out_shape=jax.ShapeDtypeStruct((L, 16, Na), F32), compiler_params=_params(("arbitrary", "arbitrary")),
    )(s16, w_ada, b_cols.reshape(L, 1, Na))


def _ada_bwd(s16, dm16, w_ada, *, name):
    L, D, Na = w_ada.shape
    tn = _pick(Na, 512)

    def body(s_ref, dm_ref, w_ref, dw_out, ds_out):
        @pl.when((pl.program_id(0) == 0) & (pl.program_id(1) == 0))
        def _():
            ds_out[...] = jnp.zeros_like(ds_out)

        dm = dm_ref[...].astype(MXU_DTYPE)
        dw_out[...] = lax.dot_general(s_ref[...], dm, (((0,), (0,)), ((), ())), preferred_element_type=F32)
        ds_out[...] += lax.dot_general(dm, w_ref[...].astype(MXU_DTYPE), (((1,), (1,)), ((), ())), preferred_element_type=F32)

    return pl.pallas_call(
        body, name=name, grid=(L, Na // tn),
        in_specs=[pl.BlockSpec((16, D), lambda l, n: (0, 0)), pl.BlockSpec((None, 16, tn), lambda l, n: (l, 0, n)),
                  pl.BlockSpec((None, D, tn), lambda l, n: (l, 0, n))],
        out_specs=(pl.BlockSpec((None, D, tn), lambda l, n: (l, 0, n)), pl.BlockSpec((16, D), lambda l, n: (0, 0))),
        out_shape=(jax.ShapeDtypeStruct((L, D, Na), F32), jax.ShapeDtypeStruct((16, D), F32)),
        compiler_params=_params(("arbitrary", "arbitrary")),
    )(s16, dm16, w_ada)


def _silu_rows(cvec, *, name):
    R, D = cvec.shape

    def body(c_ref, s_out, ds_out):
        c = c_ref[...]
        sg = _sigmoid(c)
        s_out[...] = (c * sg).astype(MXU_DTYPE)
        ds_out[...] = sg * (1.0 + c * (1.0 - sg))

    return pl.pallas_call(
        body, name=name, out_shape=(jax.ShapeDtypeStruct((R, D), MXU_DTYPE), jax.ShapeDtypeStruct((R, D), F32)),
    )(cvec)


def _sum_leading(v, order, *, name, scale_by=None):
    N, R, C = v.shape
    tr = _pick(R, max(8, ADAM_BLOCK_ELEMS // C), 8)

    def body(v_ref, *rest):
        acc = v_ref[order[0]]
        for j in order[1:]:
            acc = acc + v_ref[j]
        if scale_by is not None:
            acc = acc * rest[0][...]
        rest[-1][...] = acc

    in_specs = [pl.BlockSpec((N, tr, C), lambda i: (0, i, 0))]
    args = [v]
    if scale_by is not None:
        in_specs.append(pl.BlockSpec((tr, C), lambda i: (i, 0)))
        args.append(scale_by)
    return pl.pallas_call(
        body, name=name, grid=(R // tr,), in_specs=in_specs, out_specs=pl.BlockSpec((tr, C), lambda i: (i, 0)),
        out_shape=jax.ShapeDtypeStruct((R, C), F32), compiler_params=_params(("arbitrary",)),
    )(*args)


def _add_own_layer(g, r1, c_arr, *, name):
    L, S, R, C = g.shape
    tr = _pick(R, max(8, ADAM_BLOCK_ELEMS // C), 8)

    def body(c_ref, g_ref, r_ref, o_ref):
        o_ref[...] = g_ref[...] + r_ref[...]

    grid_spec = pltpu.PrefetchScalarGridSpec(
        num_scalar_prefetch=1, grid=(S, R // tr),
        in_specs=[pl.BlockSpec((None, None, tr, C), lambda s, i, c_ref: (c_ref[0], s, i, 0)),
                  pl.BlockSpec((None, tr, C), lambda s, i, c_ref: (s, i, 0))],
        out_specs=pl.BlockSpec((None, tr, C), lambda s, i, c_ref: (s, i, 0)))
    return pl.pallas_call(
        body, name=name, grid_spec=grid_spec, out_shape=jax.ShapeDtypeStruct((S, R, C), F32),
        compiler_params=_params(("arbitrary", "arbitrary")),
    )(c_arr, g, r1)


def _adamw(w, g, m, v, *, name):
    shape = w.shape
    C = shape[-1]
    R = w.size // C
    w2, g2, m2, v2 = (t.reshape(R, C) for t in (w, g, m, v))
    tr = _pick(R, max(8, ADAM_BLOCK_ELEMS // C), 8)

    def body(w_ref, g_ref, m_ref, v_ref, d_out, m_out, v_out):
        gg = g_ref[...]
        mn = ADAM_B1 * m_ref[...] + (1.0 - ADAM_B1) * gg
        vn = ADAM_B2 * v_ref[...] + (1.0 - ADAM_B2) * (gg * gg)
        m_hat = mn / (1.0 - ADAM_B1 ** ADAM_STEP)
        v_hat = vn / (1.0 - ADAM_B2 ** ADAM_STEP)
        d_out[...] = -ADAM_LR * (m_hat / (jnp.sqrt(v_hat) + ADAM_EPS) + ADAM_WD * w_ref[...])
        m_out[...] = mn
        v_out[...] = vn

    blk = pl.BlockSpec((tr, C), lambda i: (i, 0))
    shp = jax.ShapeDtypeStruct((R, C), F32)
    d, mn, vn = pl.pallas_call(
        body, name=name, grid=(R // tr,), in_specs=[blk] * 4, out_specs=(blk,) * 3, out_shape=(shp,) * 3,
        compiler_params=_params(("arbitrary",)),
    )(w2, g2, m2, v2)
    return d.reshape(shape), mn.reshape(shape), vn.reshape(shape)


def _place():
    x, y, c = lax.axis_index("x"), lax.axis_index("y"), lax.axis_index("c")
    return x, y, c, [(1 - x, y), (x, 1 - y), (1 - x, 1 - y)]


def _allgather_small(v, *, name):
    m_per, n = v.shape

    def body(x_ref, out_ref, send_sems, recv_sems, local_sem):
        x, y, c, chips = _place()
        me, sibling = (x, y, c), (x, y, 1 - c)

        def rows(px, py, pc):
            return out_ref.at[pl.ds((4 * px + 2 * py + pc) * m_per, m_per), :]

        def copy(k, block, to, src=None):
            return pltpu.make_async_remote_copy(
                src_ref=rows(*block) if src is None else src, dst_ref=rows(*block),
                send_sem=send_sems.at[k], recv_sem=recv_sems.at[k], device_id=to, device_id_type=MESH)

        mine = pltpu.make_async_copy(x_ref, rows(*me), local_sem)
        mine.start()
        first = [copy(0, me, sibling, src=x_ref)]
        first += [copy(1 + j, me, (*chip, c), src=x_ref) for j, chip in enumerate(chips)]
        for cp in first:
            cp.start()
        passed = [copy(4 + j, (*chip, c), sibling) for j, chip in enumerate(chips)]
        for j, chip in enumerate(chips):
            copy(1 + j, (*chip, c), me).wait_recv()
            passed[j].start()
        copy(0, sibling, me).wait_recv()
        for j, chip in enumerate(chips):
            copy(4 + j, (*chip, 1 - c), me).wait_recv()
        for cp in first + passed:
            cp.wait_send()
        mine.wait()

    return pl.pallas_call(
        body, name=name, out_shape=jax.ShapeDtypeStruct((N_DEV * m_per, n), v.dtype),
        in_specs=[pl.BlockSpec(memory_space=pltpu.VMEM)], out_specs=pl.BlockSpec(memory_space=pltpu.VMEM),
        scratch_shapes=[pltpu.SemaphoreType.DMA((7,)), pltpu.SemaphoreType.DMA((7,)), pltpu.SemaphoreType.DMA],
        compiler_params=pltpu.CompilerParams(vmem_limit_bytes=VMEM_LIMIT_BYTES),
    )(v)


def _gather_weight(w, *, name):
    shard = w.shape[1:]

    def body(w_ref, out_ref, send_sems, recv_sems, local_sem):
        x, y, c, chips = _place()
        sibling = (x, y, 1 - c)

        def slot(px, py, pc):
            return out_ref.at[2 * px + py, pc]

        def copy(k, block, to, src=None):
            return pltpu.make_async_remote_copy(
                src_ref=slot(*block) if src is None else src, dst_ref=slot(*block),
                send_sem=send_sems.at[k], recv_sem=recv_sems.at[k], device_id=to, device_id_type=MESH)

        mine = pltpu.make_async_copy(w_ref.at[c], slot(x, y, c), local_sem)
        mine.start()
        first = [copy(0, (x, y, c), sibling, src=w_ref.at[c])]
        first += [copy(1 + j, (x, y, c), (*chip, c), src=w_ref.at[c]) for j, chip in enumerate(chips)]
        for cp in first:
            cp.start()
        passed = [copy(4 + j, (*chip, c), sibling) for j, chip in enumerate(chips)]
        for j, chip in enumerate(chips):
            copy(1 + j, (*chip, c), (x, y, c)).wait_recv()
            passed[j].start()
        copy(0, (x, y, 1 - c), (x, y, c)).wait_recv()
        for j, chip in enumerate(chips):
            copy(4 + j, (*chip, 1 - c), (x, y, c)).wait_recv()
        for cp in first + passed:
            cp.wait_send()
        mine.wait()

    return pl.pallas_call(
        body, name=name, out_shape=jax.ShapeDtypeStruct((N_CHIPS, *w.shape), w.dtype),
        in_specs=[pl.BlockSpec(memory_space=pl.ANY)], out_specs=pl.BlockSpec(memory_space=pl.ANY),
        scratch_shapes=[pltpu.SemaphoreType.DMA((7,)), pltpu.SemaphoreType.DMA((7,)), pltpu.SemaphoreType.DMA],
    )(w)


def _send_other_layer(g, *, name):
    def body(g_ref, out_ref, send_sem, recv_sem):
        x, y, c, _ = _place()
        cp = pltpu.make_async_remote_copy(src_ref=g_ref.at[1 - c], dst_ref=out_ref, send_sem=send_sem, recv_sem=recv_sem,
                                          device_id=(x, y, 1 - c), device_id_type=MESH)
        cp.start()
        cp.wait()

    return pl.pallas_call(
        body, name=name, out_shape=jax.ShapeDtypeStruct(g.shape[1:], g.dtype),
        in_specs=[pl.BlockSpec(memory_space=pl.ANY)], out_specs=pl.BlockSpec(memory_space=pl.ANY),
        scratch_shapes=[pltpu.SemaphoreType.DMA, pltpu.SemaphoreType.DMA],
    )(g)


def _scatter_chips(p, *, name):
    def body(p_ref, out_ref, send_sems, recv_sems, local_sem):
        x, y, c, chips = _place()
        me = 2 * x + y
        mine = pltpu.make_async_copy(p_ref.at[me], out_ref.at[me], local_sem)
        mine.start()
        sends = [pltpu.make_async_remote_copy(
            src_ref=p_ref.at[2 * px + py], dst_ref=out_ref.at[me], send_sem=send_sems.at[j], recv_sem=recv_sems.at[j],
            device_id=(px, py, c), device_id_type=MESH) for j, (px, py) in enumerate(chips)]
        for cp in sends:
            cp.start()
        for j, (px, py) in enumerate(chips):
            pltpu.make_async_remote_copy(
                src_ref=p_ref.at[me], dst_ref=out_ref.at[2 * px + py], send_sem=send_sems.at[j], recv_sem=recv_sems.at[j],
                device_id=(px, py, c), device_id_type=MESH).wait_recv()
        for cp in sends:
            cp.wait_send()
        mine.wait()

    return pl.pallas_call(
        body, name=name, out_shape=jax.ShapeDtypeStruct(p.shape, p.dtype),
        in_specs=[pl.BlockSpec(memory_space=pl.ANY)], out_specs=pl.BlockSpec(memory_space=pl.ANY),
        scratch_shapes=[pltpu.SemaphoreType.DMA((3,)), pltpu.SemaphoreType.DMA((3,)), pltpu.SemaphoreType.DMA],
    )(p)


def _share_layers(q, *, name):
    def body(q_ref, out_ref, send_sem, recv_sem, local_sem):
        x, y, c, _ = _place()
        mine = pltpu.make_async_copy(q_ref, out_ref.at[c], local_sem)
        mine.start()
        pltpu.make_async_remote_copy(src_ref=q_ref, dst_ref=out_ref.at[c], send_sem=send_sem, recv_sem=recv_sem,
                                     device_id=(x, y, 1 - c), device_id_type=MESH).start()
        pltpu.make_async_remote_copy(src_ref=q_ref, dst_ref=out_ref.at[1 - c], send_sem=send_sem, recv_sem=recv_sem,
                                     device_id=(x, y, 1 - c), device_id_type=MESH).wait()
        mine.wait()

    return pl.pallas_call(
        body, name=name, out_shape=jax.ShapeDtypeStruct((2, *q.shape), q.dtype),
        in_specs=[pl.BlockSpec(memory_space=pl.ANY)], out_specs=pl.BlockSpec(memory_space=pl.ANY),
        scratch_shapes=[pltpu.SemaphoreType.DMA, pltpu.SemaphoreType.DMA, pltpu.SemaphoreType.DMA],
    )(q)


def _reduce_grad(g, c_arr, tag):
    L, S = g.shape[:2]
    shard = g.shape[2:]
    C = shard[-1]
    R = 1
    for d in shard[:-1]:
        R *= d
    r1 = _send_other_layer(g, name=f"grad_swap_{tag}")
    p = _add_own_layer(g.reshape(L, S, R, C), r1.reshape(S, R, C), c_arr, name=f"grad_pair_{tag}")
    r2 = _scatter_chips(p, name=f"grad_scatter_{tag}")
    q = _sum_leading(r2, tuple(range(S)), name=f"grad_sum_{tag}")
    return _share_layers(q, name=f"grad_share_{tag}").reshape(L, *shard)


def kernel(x, c, ctx, c_ctx, w_ada, b_ada, ln_g, ln_b, ff1_in, ff1_out, ff2_in, ff2_out, w_in, conv4_w, conv4_b, w_rg, b_rg, w_ig, b_ig, lam, conv31_w, conv31_b, cln_g, cln_b, w_out, b_out, loss_target, m_c_ctx, m_w_ada, m_b_ada, m_ln_g, m_ln_b, m_ff1_in, m_ff1_out, m_ff2_in, m_ff2_out, m_w_in, m_conv4_w, m_conv4_b, m_w_rg, m_b_rg, m_w_ig, m_b_ig, m_lam, m_conv31_w, m_conv31_b, m_cln_g, m_cln_b, m_w_out, m_b_out, v_c_ctx, v_w_ada, v_b_ada, v_ln_g, v_ln_b, v_ff1_in, v_ff1_out, v_ff2_in, v_ff2_out, v_w_in, v_conv4_w, v_conv4_b, v_w_rg, v_b_rg, v_w_ig, v_b_ig, v_lam, v_conv31_w, v_conv31_b, v_cln_g, v_cln_b, v_w_out, v_b_out):
    weights = dict(c_ctx=c_ctx, w_ada=w_ada, b_ada=b_ada, ln_g=ln_g, ln_b=ln_b, ff1_in=ff1_in, ff1_out=ff1_out,
                   ff2_in=ff2_in, ff2_out=ff2_out, w_in=w_in, conv4_w=conv4_w, conv4_b=conv4_b, w_rg=w_rg, b_rg=b_rg,
                   w_ig=w_ig, b_ig=b_ig, lam=lam, conv31_w=conv31_w, conv31_b=conv31_b, cln_g=cln_g, cln_b=cln_b,
                   w_out=w_out, b_out=b_out)
    m_in = dict(c_ctx=m_c_ctx, w_ada=m_w_ada, b_ada=m_b_ada, ln_g=m_ln_g, ln_b=m_ln_b, ff1_in=m_ff1_in, ff1_out=m_ff1_out,
                ff2_in=m_ff2_in, ff2_out=m_ff2_out, w_in=m_w_in, conv4_w=m_conv4_w, conv4_b=m_conv4_b, w_rg=m_w_rg,
                b_rg=m_b_rg, w_ig=m_w_ig, b_ig=m_b_ig, lam=m_lam, conv31_w=m_conv31_w, conv31_b=m_conv31_b,
                cln_g=m_cln_g, cln_b=m_cln_b, w_out=m_w_out, b_out=m_b_out)
    v_in = dict(c_ctx=v_c_ctx, w_ada=v_w_ada, b_ada=v_b_ada, ln_g=v_ln_g, ln_b=v_ln_b, ff1_in=v_ff1_in, ff1_out=v_ff1_out,
                ff2_in=v_ff2_in, ff2_out=v_ff2_out, w_in=v_w_in, conv4_w=v_conv4_w, conv4_b=v_conv4_b, w_rg=v_w_rg,
                b_rg=v_b_rg, w_ig=v_w_ig, b_ig=v_b_ig, lam=v_lam, conv31_w=v_conv31_w, conv31_b=v_conv31_b,
                cln_g=v_cln_g, cln_b=v_cln_b, w_out=v_w_out, b_out=v_b_out)
    order = list(weights)

    ax, ay, ac = lax.axis_index("x"), lax.axis_index("y"), lax.axis_index("c")
    chip = 2 * ax + ay
    dev = 4 * ax + 2 * ay + ac
    c_arr = jnp.reshape(ac, (1,)).astype(jnp.int32)

    L, D, Na = w_ada.shape
    Tl, Tc = x.shape[1], ctx.shape[1]
    T = Tc + Tl
    tms = Tc
    C = conv4_w.shape[2] * N_CHIPS
    nH, hds, hd = w_rg.shape[2], w_rg.shape[3], w_rg.shape[4]
    K31 = conv31_w.shape[1]
    assert L == 2 and Tl % tms == 0 and tms % GRID_W == 0 and tms % 8 == 0 and tms & (tms - 1) == 0
    assert hds * N_CHIPS == hd and nH * hd == C and D == 2 * C
    alpha = (2 * L) ** 0.25
    n_mod = N_CHIPS * Na // D

    def shard_cols(full, width):
        return lax.dynamic_slice_in_dim(full, chip * width, width, axis=full.ndim - 1)

    c8 = jnp.zeros((8, D), F32).at[0].set(c[0]).at[1].set(c_ctx)
    c_all = _allgather_small(c8, name="gather_cond").reshape(N_DEV, 8, D)
    c16 = jnp.concatenate([c_all[:, 0], c_ctx[None], jnp.zeros((7, D), F32)], axis=0)
    s16, ds16 = _silu_rows(c16, name="silu_cond")
    mod_part = _ada_fwd(s16, w_ada, shard_cols(b_ada, Na), name="ada_fwd")
    mod_all = _allgather_small(mod_part.reshape(L * 16, Na), name="gather_mod").reshape(N_DEV, L, 16, Na)
    mod_full = jnp.transpose(mod_all[0::2], (1, 2, 0, 3)).reshape(L, 16, N_CHIPS * Na)
    mod_rows = jnp.stack([mod_full[:, 8], lax.dynamic_index_in_dim(mod_full, dev, axis=1, keepdims=False)], axis=1)
    mod = mod_rows.reshape(L, 2, n_mod, D)

    def mvec(l, k):
        return mod[l, :, k, :]

    gw = {n: _gather_weight(weights[n].astype(MXU_DTYPE), name=f"gather_{n}")
          for n in ("ff1_in", "ff1_out", "ff2_in", "ff2_out", "w_in", "w_out", "w_rg", "w_ig")}

    def full_gate(g):
        return jnp.transpose(g, (1, 2, 3, 0, 4, 5)).reshape(L, 2, nH, hd, hd)

    wr_full, wi_full = full_gate(gw["w_rg"]), full_gate(gw["w_ig"])

    small_sharded = ("ln_g", "ln_b", "conv4_w", "b_rg", "b_ig", "lam", "conv31_w")
    pieces = {n: weights[n].reshape(-1, weights[n].shape[-1]) for n in small_sharded}
    widths = {n: p.shape[1] for n, p in pieces.items()}
    rows_of = {n: p.shape[0] for n, p in pieces.items()}
    wcat = max(widths.values())
    cat = jnp.concatenate([jnp.pad(p, ((0, 0), (0, wcat - p.shape[1]))) for p in pieces.values()], axis=0)
    rpad = -cat.shape[0] % 8
    cat_all = _allgather_small(jnp.pad(cat, ((0, rpad), (0, 0))), name="gather_small").reshape(N_DEV, -1, wcat)
    full_small, r0 = {}, 0
    for n in small_sharded:
        blk = cat_all[0::2, r0:r0 + rows_of[n], :widths[n]]
        full_small[n] = jnp.transpose(blk, (1, 0, 2)).reshape(rows_of[n], N_CHIPS * widths[n])
        r0 += rows_of[n]
    ln_g_f = full_small["ln_g"].reshape(L, 3, 1, D)
    ln_b_f = full_small["ln_b"].reshape(L, 3, 1, D)
    conv4_w_f = full_small["conv4_w"].reshape(L, 4, C)
    b_rg_f = full_small["b_rg"].reshape(L, 2, 1, C)
    b_ig_f = full_small["b_ig"].reshape(L, 2, 1, C)
    lam_f = full_small["lam"].reshape(L, 2, 1, C)
    conv31_w_f = full_small["conv31_w"].reshape(L, K31, C)

    ones, zeros = jnp.ones((1, D), F32), jnp.zeros((1, D), F32)

    s0 = jnp.concatenate([ctx[0], x[0]], axis=0)
    cur = (s0, ones, zeros)
    saved = []
    for l in range(L):
        sv = {"in": cur}
        h1, gu1, act1 = _in_proj(*cur, mvec(l, 0), mvec(l, 1), gw["ff1_in"], l, tc=Tc, swiglu=True, name=f"ffn1_in_{l}")
        xh1, rs1, f1 = _out_proj_ln(act1, gw["ff1_out"], l, zeros, *cur, mvec(l, 2), 0.5, alpha, tc=Tc, name=f"ffn1_out_{l}")
        sv.update(h1=h1, gu1=gu1, act1=act1, xh1=xh1, rs1=rs1, f1=f1)
        cur1 = (xh1, ln_g_f[l, 0], ln_b_f[l, 0])
        h2, z = _in_proj(*cur1, mvec(l, 3), mvec(l, 4), gw["w_in"], l, tc=Tc, swiglu=False, name=f"mix_in_{l}")
        xc = _conv4_fwd(z, conv4_w_f[l], conv4_b[l][None], tms=tms, name=f"conv4_{l}")
        rec = []
        for d in range(2):
            rec.append(_lru_fwd(xc, wr_full[l, d], wi_full[l, d], b_rg_f[l, d], b_ig_f[l, d], lam_f[l, d],
                                rev=bool(d), tms=tms, name=f"lru_{l}_{d}"))
        ymix, uc = _convmod_fwd(rec[0][0], rec[1][0], z, conv31_w_f[l], conv31_b[l][None], cln_g[l][None], cln_b[l][None],
                                tms=tms, name=f"convmod_{l}")
        xh2, rs2, f2 = _out_proj_ln(ymix, gw["w_out"], l, b_out[l][None], *cur1, mvec(l, 5), 1.0, alpha, tc=Tc, name=f"mix_out_{l}")
        sv.update(h2=h2, z=z, xc=xc, rec=rec, ymix=ymix, uc=uc, xh2=xh2, rs2=rs2, f2=f2)
        cur2 = (xh2, ln_g_f[l, 1], ln_b_f[l, 1])
        h3, gu3, act3 = _in_proj(*cur2, mvec(l, 6), mvec(l, 7), gw["ff2_in"], l, tc=Tc, swiglu=True, name=f"ffn2_in_{l}")
        xh3, rs3, f3 = _out_proj_ln(act3, gw["ff2_out"], l, zeros, *cur2, mvec(l, 8), 0.5, alpha, tc=Tc, name=f"ffn2_out_{l}")
        sv.update(h3=h3, gu3=gu3, act3=act3, xh3=xh3, rs3=rs3, f3=f3)
        cur = (xh3, ln_g_f[l, 2], ln_b_f[l, 2])
        saved.append(sv)

    ds, loss_blk = _loss_head(*cur, loss_target[0], tc=Tc, tms=tms, name="loss_head")

    big = {n: None for n in ("ff1_in", "ff1_out", "ff2_in", "ff2_out", "w_in", "w_out")}
    dmod = [[None] * n_mod for _ in range(L)]
    d_ln_g = [[None] * 3 for _ in range(L)]
    d_ln_b = [[None] * 3 for _ in range(L)]
    small = {n: [None] * L for n in ("conv4_w", "conv4_b", "conv31_w", "conv31_b", "cln_g", "cln_b", "b_out")}
    gate_w = {n: [[None, None] for _ in range(L)] for n in ("w_rg", "w_ig", "b_rg", "b_ig", "lam")}

    def ffn_bwd(ds, l, k, names, sv_in, sfx):
        sv = saved[l]
        dy, dres, d_ln_g[l][k], d_ln_b[l][k], dmod[l][3 * k + 2], _ = _ln_bwd(
            ds, sv["xh" + sfx], sv["rs" + sfx], ln_g_f[l, k], sv["f" + sfx], mvec(l, 3 * k + 2), 0.5, alpha,
            tc=Tc, name=f"ffn{sfx}_ln_bwd_{l}")
        dg = _nt(dy, gw[names[1]], l, sv["gu" + sfx], name=f"ffn{sfx}_dact_{l}")
        big[names[1]] = _wgrad(sv["act" + sfx], dy, l, L, big[names[1]], cols_sharded=False, name=f"ffn{sfx}_wgrad_out_{l}")
        ds_new, dmod[l][3 * k + 1], dmod[l][3 * k] = _dx_modbwd(
            dg, gw[names[0]], l, dres, *sv_in, mvec(l, 3 * k + 1), tc=Tc, name=f"ffn{sfx}_dx_{l}")
        big[names[0]] = _wgrad(sv["h" + sfx], dg, l, L, big[names[0]], cols_sharded=True, name=f"ffn{sfx}_wgrad_in_{l}")
        return ds_new

    for l in reversed(range(L)):
        sv = saved[l]
        cur1 = (sv["xh1"], ln_g_f[l, 0], ln_b_f[l, 0])
        cur2 = (sv["xh2"], ln_g_f[l, 1], ln_b_f[l, 1])
        ds = ffn_bwd(ds, l, 2, ("ff2_in", "ff2_out"), cur2, "3")
        dy, dres, d_ln_g[l][1], d_ln_b[l][1], dmod[l][5], small["b_out"][l] = _ln_bwd(
            ds, sv["xh2"], sv["rs2"], ln_g_f[l, 1], sv["f2"], mvec(l, 5), 1.0, alpha, tc=Tc, name=f"mix_ln_bwd_{l}")
        dymix = _nt(dy, gw["w_out"], l, None, name=f"mix_dy_{l}")
        big["w_out"] = _wgrad(sv["ymix"], dy, l, L, big["w_out"], cols_sharded=False, name=f"mix_wgrad_out_{l}")
        dhs, dz, small["conv31_w"][l], small["conv31_b"][l], small["cln_g"][l], small["cln_b"][l] = _convmod_bwd(
            dymix, sv["rec"][0][0], sv["rec"][1][0], sv["z"], sv["uc"], conv31_w_f[l], cln_g[l][None], cln_b[l][None],
            tms=tms, name=f"convmod_bwd_{l}")
        dxc = []
        for d in range(2):
            hd_, rd_, id_ = sv["rec"][d]
            o = _lru_bwd(dhs, hd_, rd_, id_, sv["xc"], wr_full[l, d], wi_full[l, d], lam_f[l, d],
                         rev=bool(d), tms=tms, name=f"lru_bwd_{l}_{d}")
            dxc.append(o[0])
            for n, val in zip(("w_rg", "w_ig", "b_rg", "b_ig", "lam"), o[1:]):
                gate_w[n][l][d] = val
        dz, small["conv4_w"][l], small["conv4_b"][l] = _conv4_bwd(dxc[0], dxc[1], sv["z"], conv4_w_f[l], dz, tms=tms, name=f"conv4_bwd_{l}")
        ds, dmod[l][4], dmod[l][3] = _dx_modbwd(dz, gw["w_in"], l, dres, *cur1, mvec(l, 4), tc=Tc, name=f"mix_dx_{l}")
        big["w_in"] = _wgrad(sv["h2"], dz, l, L, big["w_in"], cols_sharded=True, name=f"mix_wgrad_in_{l}")
        ds = ffn_bwd(ds, l, 0, ("ff1_in", "ff1_out"), sv["in"], "1")

    grad_x = ds[Tc:][None]

    def gate_slots(g):
        g = g.reshape(L, 2, nH, N_CHIPS, hds, hd)
        return jnp.transpose(g, (0, 3, 1, 2, 4, 5)).reshape(L, N_CHIPS, 2 * nH * hds, hd)

    big["w_rg"] = gate_slots(jnp.stack([jnp.stack(gate_w["w_rg"][l]) for l in range(L)]))
    big["w_ig"] = gate_slots(jnp.stack([jnp.stack(gate_w["w_ig"][l]) for l in range(L)]))

    dmod_arr = jnp.stack([jnp.stack(dmod[l], axis=1) for l in range(L)])
    dm_ctx = dmod_arr[:, 0].reshape(L, n_mod * D)
    dm_lat = dmod_arr[:, 1].reshape(L, n_mod * D)
    summed = {
        "loss": loss_blk[0:1, 0:1],
        "dm_ctx": dm_ctx,
        "ln_g": jnp.stack([jnp.concatenate(d_ln_g[l], axis=0) for l in range(L)]),
        "ln_b": jnp.stack([jnp.concatenate(d_ln_b[l], axis=0) for l in range(L)]),
        "conv4_w": jnp.stack(small["conv4_w"]),
        "conv4_b": jnp.concatenate(small["conv4_b"], axis=0),
        "b_rg": jnp.stack([jnp.concatenate(gate_w["b_rg"][l], axis=0) for l in range(L)]),
        "b_ig": jnp.stack([jnp.concatenate(gate_w["b_ig"][l], axis=0) for l in range(L)]),
        "lam": jnp.stack([jnp.concatenate(gate_w["lam"][l], axis=0) for l in range(L)]),
        "conv31_w": jnp.stack(small["conv31_w"]),
        "conv31_b": jnp.concatenate(small["conv31_b"], axis=0),
        "cln_g": jnp.concatenate(small["cln_g"], axis=0),
        "cln_b": jnp.concatenate(small["cln_b"], axis=0),
        "b_out": jnp.concatenate(small["b_out"], axis=0),
        "dm_lat": dm_lat,
    }
    flat = jnp.concatenate([v.reshape(-1) for v in summed.values()])
    n_flat = flat.shape[0]
    n_rows = -(-n_flat // 128)
    n_rows += -n_rows % 8
    vec = jnp.pad(flat, (0, n_rows * 128 - n_flat)).reshape(n_rows, 128)
    vec_all = _allgather_small(vec, name="gather_small_grads").reshape(N_DEV, n_rows, 128)
    vec_sum = _sum_leading(vec_all, tuple(range(N_DEV)), name="sum_small_grads").reshape(-1)
    tot, off = {}, 0
    for n, v in summed.items():
        tot[n] = vec_sum[off:off + v.size].reshape(v.shape)
        if n == "dm_lat":
            dm_lat_all = vec_all.reshape(N_DEV, -1)[:, off:off + v.size].reshape(N_DEV, L, n_mod * D)
        off += v.size
    loss = tot["loss"].reshape(())

    dm16 = jnp.concatenate([jnp.transpose(dm_lat_all, (1, 0, 2)), tot["dm_ctx"][:, None], jnp.zeros((L, 7, n_mod * D), F32)], axis=1)
    g_w_ada, ds16_part = _ada_bwd(s16, shard_cols(dm16, Na), w_ada, name="ada_bwd")
    ds_all = _allgather_small(ds16_part[8:16], name="gather_dcond").reshape(N_DEV, 8, D)
    g_c_ctx = _sum_leading(ds_all[:, 0:1], (0, 2, 4, 6), name="sum_dcond", scale_by=ds16[8:9]).reshape(D)
    g_b_ada = _sum_leading(jnp.stack([tot["dm_lat"], tot["dm_ctx"]]), (0, 1), name="sum_b_ada")

    grads = {"c_ctx": g_c_ctx, "w_ada": g_w_ada, "b_ada": g_b_ada}
    for n in ("ff1_in", "ff1_out", "ff2_in", "ff2_out", "w_in", "w_out"):
        grads[n] = _reduce_grad(big[n], c_arr, n)
    for n in ("w_rg", "w_ig"):
        grads[n] = _reduce_grad(big[n], c_arr, n).reshape(L, 2, nH, hds, hd)
    for n in ("ln_g", "ln_b", "conv4_w", "b_rg", "b_ig", "lam", "conv31_w"):
        grads[n] = shard_cols(tot[n], weights[n].shape[-1])
    for n in ("conv4_b", "conv31_b", "cln_g", "cln_b", "b_out"):
        grads[n] = tot[n]

    delta, new_m, new_v = {}, {}, {}
    for n in order:
        delta[n], new_m[n], new_v[n] = _adamw(weights[n], grads[n], m_in[n], v_in[n], name=f"adamw_{n}")
    return (loss, grad_x, *[grads[n] for n in order], *[delta[n] for n in order],
            *[new_m[n] for n in order], *[new_v[n] for n in order])
```

```python
import functools

import jax
import jax.numpy as jnp
from jax import lax
from jax.experimental import pallas as pl
from jax.experimental.pallas import tpu as pltpu

F32 = jnp.float32
MXU_DTYPE = jnp.bfloat16
GRID_W = 64
RG_C = 8.0
LN_EPS = 1e-6
ADAM_LR, ADAM_B1, ADAM_B2, ADAM_EPS, ADAM_WD, ADAM_STEP = 0.001, 0.9, 0.999, 1e-08, 0.01, 10
N_CHIPS = 4
N_DEV = 8
VMEM_LIMIT_BYTES = 56 * 1024 * 1024
TM_IN, TN_IN = 768, 256
TM_OUT, TK_OUT = 384, 1408
TM_NT, TN_NT = 384, 1408
TM_DX, TK_DX = 384, 1408
TK_WG, TB_WG = 768, 1408
ADAM_BLOCK_ELEMS = 256 * 1024
MESH = pl.DeviceIdType.MESH


def _pick(total, target, mult=128):
    for d in range(min(total, target), 0, -1):
        if total % d == 0 and d % mult == 0:
            return d
    return total


def _params(sem=None):
    kw = dict(vmem_limit_bytes=VMEM_LIMIT_BYTES)
    if sem is not None:
        kw["dimension_semantics"] = sem
    return pltpu.CompilerParams(**kw)


def _sigmoid(x):
    return 1.0 / (1.0 + jnp.exp(-x))


def _gelu(x):
    k = 0.7978845608028654
    t = jnp.tanh(k * (x + 0.044715 * (x * x * x)))
    return 0.5 * x * (1.0 + t)


def _gelu_grad(x):
    k = 0.7978845608028654
    t = jnp.tanh(k * (x + 0.044715 * (x * x * x)))
    return 0.5 * (1.0 + t) + 0.5 * x * (1.0 - t * t) * (k * (1.0 + 3.0 * 0.044715 * x * x))


def _log1p(e):
    u = 1.0 + e
    return jnp.where(u == 1.0, e, jnp.log(u) * (e / jnp.where(u == 1.0, 1.0, u - 1.0)))


def _softplus(y):
    return jnp.maximum(y, 0.0) + _log1p(jnp.exp(-jnp.abs(y)))


def _expm1(x):
    series = x * (1.0 + x * (0.5 + x * (1.0 / 6.0 + x * (1.0 / 24.0 + x * (1.0 / 120.0 + x * (1.0 / 720.0))))))
    return jnp.where(jnp.abs(x) < 0.1, series, jnp.exp(x) - 1.0)


def _rows(i, tm):
    return i * tm + lax.broadcasted_iota(jnp.int32, (tm, 1), 0)


def _sel(isctx, ref):
    return jnp.where(isctx, ref[0:1, :], ref[1:2, :])


def _colsum(v):
    return jnp.sum(v, axis=0, keepdims=True)


def _in_proj(xhat, g_in, b_in, shift, scale, w, *, tc, swiglu, name):
    T, D = xhat.shape
    S, _, Ns = w.shape
    tm, tn = _pick(T, TM_IN, 8), _pick(Ns, TN_IN)
    nps = Ns // tn
    nI = T // tm
    nN = (S // 2 if swiglu else S) * nps

    def body(x_ref, g_ref, b_ref, sh_ref, sc_ref, w_ref, h_out, *rest):
        i, n = pl.program_id(0), pl.program_id(1)
        h_scr = rest[-1]

        @pl.when(n == 0)
        def _():
            s = x_ref[...] * g_ref[...] + b_ref[...]
            isctx = _rows(i, tm) < tc
            h = (s * (1.0 + _sel(isctx, sc_ref)) + _sel(isctx, sh_ref)).astype(MXU_DTYPE)
            h_scr[...] = h
            h_out[...] = h

        h = h_scr[...]
        if swiglu:
            gu_out, act_out = rest[0], rest[1]
            gt = jnp.dot(h, w_ref[0], preferred_element_type=F32)
            ut = jnp.dot(h, w_ref[1], preferred_element_type=F32)
            gu_out[0] = gt.astype(MXU_DTYPE)
            gu_out[1] = ut.astype(MXU_DTYPE)
            act_out[...] = ((gt * _sigmoid(gt)) * ut).astype(MXU_DTYPE)
        else:
            rest[0][...] = jnp.dot(h, w_ref[...], preferred_element_type=F32)

    vec = pl.BlockSpec((1, D), lambda i, n: (0, 0))
    vec2 = pl.BlockSpec((2, D), lambda i, n: (0, 0))
    in_specs = [pl.BlockSpec((tm, D), lambda i, n: (i, 0)), vec, vec, vec2, vec2]
    h_shape = jax.ShapeDtypeStruct((T, D), MXU_DTYPE)
    h_spec = pl.BlockSpec((tm, D), lambda i, n: (i, 0))
    if swiglu:
        F = (S // 2) * Ns
        wv = w.reshape(2, S // 2, *w.shape[1:])
        in_specs.append(pl.BlockSpec((2, None, D, tn), lambda i, n: (0, n // nps, 0, n % nps)))
        out_shape = (h_shape, jax.ShapeDtypeStruct((2, T, F), MXU_DTYPE), jax.ShapeDtypeStruct((T, F), MXU_DTYPE))
        out_specs = (h_spec, pl.BlockSpec((2, tm, tn), lambda i, n: (0, i, n)), pl.BlockSpec((tm, tn), lambda i, n: (i, n)))
    else:
        wv = w
        in_specs.append(pl.BlockSpec((None, D, tn), lambda i, n: (n // nps, 0, n % nps)))
        out_shape = (h_shape, jax.ShapeDtypeStruct((S, T, Ns), F32))
        out_specs = (h_spec, pl.BlockSpec((None, tm, tn), lambda i, n: (n // nps, i, n % nps)))
    return pl.pallas_call(
        body, name=name, grid=(nI, nN), in_specs=in_specs, out_specs=out_specs, out_shape=out_shape,
        scratch_shapes=[pltpu.VMEM((tm, D), MXU_DTYPE)], compiler_params=_params(("arbitrary", "arbitrary")),
    )(xhat, g_in, b_in, shift, scale, wv)


def _out_proj_ln(a, w, bias, xin, g_in, b_in, gvec, gscale, alpha, *, tc, name):
    T, K = a.shape
    S, Ks, D = w.shape
    tm, tk = _pick(T, TM_OUT, 8), _pick(Ks, TK_OUT)
    kps = Ks // tk
    nK = S * kps

    def body(a_ref, w_ref, bias_ref, xin_ref, gi_ref, bi_ref, gv_ref, xh_out, rstd_out, f_out, acc):
        i, k = pl.program_id(0), pl.program_id(1)

        @pl.when(k == 0)
        def _():
            acc[...] = jnp.zeros_like(acc)

        acc[...] += jnp.dot(a_ref[...], w_ref[...], preferred_element_type=F32)

        @pl.when(k == nK - 1)
        def _():
            f = acc[...] + bias_ref[...]
            f_out[...] = f.astype(MXU_DTYPE)
            s = xin_ref[...] * gi_ref[...] + bi_ref[...]
            gv = gscale * _sel(_rows(i, tm) < tc, gv_ref)
            r = alpha * s + gv * f
            mu = jnp.mean(r, axis=-1, keepdims=True)
            d = r - mu
            var = jnp.mean(d * d, axis=-1, keepdims=True)
            rstd = lax.rsqrt(var + LN_EPS)
            xh_out[...] = d * rstd
            rstd_out[...] = rstd

    vec = pl.BlockSpec((1, D), lambda i, k: (0, 0))
    row = pl.BlockSpec((tm, D), lambda i, k: (i, 0))
    return pl.pallas_call(
        body, name=name, grid=(T // tm, nK),
        in_specs=[pl.BlockSpec((tm, tk), lambda i, k: (i, k)),
                  pl.BlockSpec((None, tk, D), lambda i, k: (k // kps, k % kps, 0)),
                  vec, row, vec, vec, pl.BlockSpec((2, D), lambda i, k: (0, 0))],
        out_specs=(row, pl.BlockSpec((tm, 1), lambda i, k: (i, 0)), row),
        out_shape=(jax.ShapeDtypeStruct((T, D), F32), jax.ShapeDtypeStruct((T, 1), F32),
                   jax.ShapeDtypeStruct((T, D), MXU_DTYPE)),
        scratch_shapes=[pltpu.VMEM((tm, D), F32)], compiler_params=_params(("arbitrary", "arbitrary")),
    )(a, w, bias, xin, g_in, b_in, gvec)


def _ln_bwd(ds, xhat, rstd, g_ln, f, gvec, gscale, alpha, after, *, tc, name):
    T, D = ds.shape
    tm = _pick(T, TM_OUT, 8)

    def body(ds_ref, xh_ref, rs_ref, gl_ref, f_ref, gv_ref, *rest):
        dy_out, dres_out, dgl_out, dbl_out, dgv_out, dbias_out = rest[-6:]
        i = pl.program_id(0)

        @pl.when(i == 0)
        def _():
            dgl_out[...] = jnp.zeros_like(dgl_out)
            dbl_out[...] = jnp.zeros_like(dbl_out)
            dgv_out[...] = jnp.zeros_like(dgv_out)
            dbias_out[...] = jnp.zeros_like(dbias_out)

        dsv, xh = ds_ref[...], xh_ref[...]
        dgl_out[...] += _colsum(dsv * xh)
        dbl_out[...] += _colsum(dsv)
        dxh = dsv * gl_ref[...]
        m1 = jnp.mean(dxh, axis=-1, keepdims=True)
        m2 = jnp.mean(dxh * xh, axis=-1, keepdims=True)
        dr = rs_ref[...] * (dxh - m1 - xh * m2)
        dres_out[...] = alpha * dr
        isctx = _rows(i, tm) < tc
        dyv = (gscale * _sel(isctx, gv_ref)) * dr
        dy_out[...] = dyv.astype(MXU_DTYPE)
        dbias_out[...] += _colsum(dyv)
        p = gscale * (dr * f_ref[...].astype(F32))
        dgv_out[0:1, :] += _colsum(jnp.where(isctx, p, 0.0))
        dgv_out[1:2, :] += _colsum(jnp.where(isctx, 0.0, p))

    vec = pl.BlockSpec((1, D), lambda i: (0, 0))
    vec2 = pl.BlockSpec((2, D), lambda i: (0, 0))
    row = pl.BlockSpec((tm, D), lambda i: (i, 0))
    in_specs = [row, row, pl.BlockSpec((tm, 1), lambda i: (i, 0)), vec, row, vec2]
    args = [ds, xhat, rstd, g_ln, f, gvec]
    if after is not None:
        in_specs.append(pl.BlockSpec(memory_space=pl.ANY))
        args.append(after)
    return pl.pallas_call(
        body, name=name, grid=(T // tm,), in_specs=in_specs,
        out_specs=(row, row, vec, vec, vec2, vec),
        out_shape=(jax.ShapeDtypeStruct((T, D), MXU_DTYPE), jax.ShapeDtypeStruct((T, D), F32),
                   jax.ShapeDtypeStruct((1, D), F32), jax.ShapeDtypeStruct((1, D), F32),
                   jax.ShapeDtypeStruct((2, D), F32), jax.ShapeDtypeStruct((1, D), F32)),
        compiler_params=_params(("arbitrary",)),
    )(*args)


def _nt(dy, w, gate_up, *, name):
    T, D = dy.shape
    S, Ks, _ = w.shape
    tm, tn = _pick(T, TM_NT, 8), _pick(Ks, TN_NT)
    nps = Ks // tn
    nN = S * nps
    F = S * Ks

    def body(dy_ref, w_ref, *rest):
        d = lax.dot_general(dy_ref[...], w_ref[...], (((1,), (1,)), ((), ())), preferred_element_type=F32)
        if gate_up is None:
            rest[0][...] = d
        else:
            gu_ref, dg_out = rest
            g, u = gu_ref[0].astype(F32), gu_ref[1].astype(F32)
            sg = _sigmoid(g)
            dg_out[0] = (d * u * (sg * (1.0 + g * (1.0 - sg)))).astype(MXU_DTYPE)
            dg_out[1] = (d * (g * sg)).astype(MXU_DTYPE)

    in_specs = [pl.BlockSpec((tm, D), lambda i, n: (i, 0)),
                pl.BlockSpec((None, tn, D), lambda i, n: (n // nps, n % nps, 0))]
    args = [dy, w]
    if gate_up is None:
        out_shape = jax.ShapeDtypeStruct((T, F), F32)
        out_specs = pl.BlockSpec((tm, tn), lambda i, n: (i, n))
    else:
        Ns = 2 * F // S
        q = Ns // tn
        in_specs.append(pl.BlockSpec((2, tm, tn), lambda i, n: (0, i, n)))
        args.append(gate_up)
        out_shape = jax.ShapeDtypeStruct((2, S // 2, T, Ns), MXU_DTYPE)
        out_specs = pl.BlockSpec((2, None, tm, tn), lambda i, n: (0, n // q, i, n % q))
    out = pl.pallas_call(
        body, name=name, grid=(T // tm, nN), in_specs=in_specs, out_specs=out_specs, out_shape=out_shape,
        compiler_params=_params(("arbitrary", "arbitrary")),
    )(*args)
    return out if gate_up is None else out.reshape(S, T, out.shape[-1])


def _dx_modbwd(dg, w, dres, xhat_in, g_in, b_in, scale, *, tc, name):
    S, T, Ns = dg.shape
    D = w.shape[1]
    tm, tk = _pick(T, TM_DX, 8), _pick(Ns, TK_DX)
    kps = Ns // tk
    nK = S * kps

    def body(dg_ref, w_ref, dres_ref, xin_ref, gi_ref, bi_ref, sc_ref, ds_out, dsc_out, dsh_out, acc):
        i, k = pl.program_id(0), pl.program_id(1)

        @pl.when((i == 0) & (k == 0))
        def _():
            dsc_out[...] = jnp.zeros_like(dsc_out)
            dsh_out[...] = jnp.zeros_like(dsh_out)

        @pl.when(k == 0)
        def _():
            acc[...] = jnp.zeros_like(acc)

        acc[...] += lax.dot_general(dg_ref[...], w_ref[...], (((1,), (1,)), ((), ())), preferred_element_type=F32)

        @pl.when(k == nK - 1)
        def _():
            dh = acc[...]
            isctx = _rows(i, tm) < tc
            ds_out[...] = dres_ref[...] + dh * (1.0 + _sel(isctx, sc_ref))
            s = xin_ref[...] * gi_ref[...] + bi_ref[...]
            p = dh * s
            dsc_out[0:1, :] += _colsum(jnp.where(isctx, p, 0.0))
            dsc_out[1:2, :] += _colsum(jnp.where(isctx, 0.0, p))
            dsh_out[0:1, :] += _colsum(jnp.where(isctx, dh, 0.0))
            dsh_out[1:2, :] += _colsum(jnp.where(isctx, 0.0, dh))

    vec = pl.BlockSpec((1, D), lambda i, k: (0, 0))
    vec2 = pl.BlockSpec((2, D), lambda i, k: (0, 0))
    row = pl.BlockSpec((tm, D), lambda i, k: (i, 0))
    return pl.pallas_call(
        body, name=name, grid=(T // tm, nK),
        in_specs=[pl.BlockSpec((None, tm, tk), lambda i, k: (k // kps, i, k % kps)),
                  pl.BlockSpec((None, D, tk), lambda i, k: (k // kps, 0, k % kps)),
                  row, row, vec, vec, vec2],
        out_specs=(row, vec2, vec2),
        out_shape=(jax.ShapeDtypeStruct((T, D), F32), jax.ShapeDtypeStruct((2, D), F32), jax.ShapeDtypeStruct((2, D), F32)),
        scratch_shapes=[pltpu.VMEM((tm, D), F32)], compiler_params=_params(("arbitrary", "arbitrary")),
    )(dg, w, dres, xhat_in, g_in, b_in, scale)


def _wgrad(a, b, *, cols_sharded, name):
    T = a.shape[0]
    tk = _pick(T, TK_WG, 8)
    if cols_sharded:
        S, _, Ns = b.shape
        D = a.shape[1]
        tb = _pick(Ns, TB_WG)
        grid = (S, Ns // tb, T // tk)
        in_specs = [pl.BlockSpec((tk, D), lambda s, j, k: (k, 0)), pl.BlockSpec((None, tk, tb), lambda s, j, k: (s, k, j))]
        out_shape = jax.ShapeDtypeStruct((S, D, Ns), F32)
        out_specs = pl.BlockSpec((None, D, tb), lambda s, j, k: (s, 0, j))
    else:
        D = b.shape[1]
        S = N_CHIPS
        Ks = a.shape[1] // S
        ta = _pick(Ks, TB_WG)
        q = Ks // ta
        grid = (S, q, T // tk)
        in_specs = [pl.BlockSpec((tk, ta), lambda s, j, k: (k, s * q + j)), pl.BlockSpec((tk, D), lambda s, j, k: (k, 0))]
        out_shape = jax.ShapeDtypeStruct((S, Ks, D), F32)
        out_specs = pl.BlockSpec((None, ta, D), lambda s, j, k: (s, j, 0))

    def body(a_ref, b_ref, o_ref):
        @pl.when(pl.program_id(2) == 0)
        def _():
            o_ref[...] = jnp.zeros_like(o_ref)

        o_ref[...] += lax.dot_general(a_ref[...], b_ref[...], (((0,), (0,)), ((), ())), preferred_element_type=F32)

    return pl.pallas_call(
        body, name=name, grid=grid, in_specs=in_specs, out_specs=out_specs, out_shape=out_shape,
        compiler_params=_params(("arbitrary", "arbitrary", "arbitrary")),
    )(a, b)


def _halo_specs(tms, T, C, slot=None):
    r8 = tms // 8
    last8 = T // 8 - 1
    if slot is None:
        return (pl.BlockSpec((8, C), lambda i: (jnp.maximum(i * r8 - 1, 0), 0)),
                pl.BlockSpec((tms, C), lambda i: (i, 0)),
                pl.BlockSpec((8, C), lambda i: (jnp.minimum((i + 1) * r8, last8), 0)))
    return (pl.BlockSpec((None, 8, C), lambda i: (slot, jnp.maximum(i * r8 - 1, 0), 0)),
            pl.BlockSpec((None, tms, C), lambda i: (slot, i, 0)),
            pl.BlockSpec((None, 8, C), lambda i: (slot, jnp.minimum((i + 1) * r8, last8), 0)))


def _extended(prev_ref, cur, next_ref, i, n_tiles):
    first = (i == 0) | (i == 1)
    last = (i == 0) | (i == n_tiles - 1)
    pv = jnp.where(first, 0.0, prev_ref[...])
    nx = jnp.where(last, 0.0, next_ref[...])
    return jnp.concatenate([pv, cur, nx], axis=0)


def _shifted(ext, o, tms):
    n = tms + 16
    return pltpu.roll(ext, (-o) % n, 0)[8:8 + tms]


def _conv4_fwd(z, w4, b4, *, tms, name):
    S, T, C = z.shape
    nT = T // tms

    def body(p_ref, c_ref, n_ref, w_ref, b_ref, o_ref):
        i = pl.program_id(0)
        ext = _extended(p_ref, c_ref[...], n_ref, i, nT)
        acc = jnp.zeros((tms, C), F32) + b_ref[...]
        for k in range(4):
            acc = acc + w_ref[k:k + 1, :] * _shifted(ext, k - 2, tms)
        o_ref[...] = acc

    return pl.pallas_call(
        body, name=name, grid=(nT,),
        in_specs=[*_halo_specs(tms, T, C, 0), pl.BlockSpec((4, C), lambda i: (0, 0)), pl.BlockSpec((1, C), lambda i: (0, 0))],
        out_specs=pl.BlockSpec((tms, C), lambda i: (i, 0)), out_shape=jax.ShapeDtypeStruct((T, C), F32),
        compiler_params=_params(("arbitrary",)),
    )(z, z, z, w4, b4)


def _conv4_bwd(dxa, dxb, z, w4, dz, *, tms, name):
    S, T, C = z.shape
    nT = T // tms

    def body(pa, ca, na, pb, cb, nb, px, cx, nx, w_ref, dz_in, dz_out, dw_out, db_out):
        i = pl.program_id(0)

        @pl.when(i == 0)
        def _():
            dw_out[...] = jnp.zeros_like(dw_out)
            db_out[...] = jnp.zeros_like(db_out)

        dcur = ca[...] + cb[...]
        first = (i == 0) | (i == 1)
        last = (i == 0) | (i == nT - 1)
        dext = jnp.concatenate([jnp.where(first, 0.0, pa[...] + pb[...]), dcur, jnp.where(last, 0.0, na[...] + nb[...])], axis=0)
        xext = _extended(px, cx[...], nx, i, nT)
        acc = jnp.zeros((tms, C), F32)
        for k in range(4):
            acc = acc + w_ref[k:k + 1, :] * _shifted(dext, 2 - k, tms)
            dw_out[k:k + 1, :] += _colsum(dcur * _shifted(xext, k - 2, tms))
        db_out[...] += _colsum(dcur)
        dz_out[...] = acc.astype(MXU_DTYPE)

    h = _halo_specs(tms, T, C)
    return pl.pallas_call(
        body, name=name, grid=(nT,),
        in_specs=[*h, *h, *_halo_specs(tms, T, C, 0), pl.BlockSpec((4, C), lambda i: (0, 0)), pl.BlockSpec(memory_space=pl.ANY)],
        out_specs=(pl.BlockSpec((None, tms, C), lambda i: (0, i, 0)), pl.BlockSpec((4, C), lambda i: (0, 0)),
                   pl.BlockSpec((1, C), lambda i: (0, 0))),
        out_shape=(jax.ShapeDtypeStruct(dz.shape, dz.dtype), jax.ShapeDtypeStruct((4, C), F32), jax.ShapeDtypeStruct((1, C), F32)),
        input_output_aliases={10: 0}, compiler_params=_params(("arbitrary",)),
    )(dxa, dxa, dxa, dxb, dxb, dxb, z, z, z, w4, dz)


def _scan_tile(a, b, rev, n):
    rows = lax.broadcasted_iota(jnp.int32, (n, 1), 0)
    sft = 1
    while sft < n:
        if rev:
            a_sh, b_sh, valid = pltpu.roll(a, n - sft, 0), pltpu.roll(b, n - sft, 0), rows < n - sft
        else:
            a_sh, b_sh, valid = pltpu.roll(a, sft, 0), pltpu.roll(b, sft, 0), rows >= sft
        b = a * jnp.where(valid, b_sh, 0.0) + b
        a = a * jnp.where(valid, a_sh, 1.0)
        sft *= 2
    return a, b


def _scan_order(i, rev, nT):
    if not rev:
        return i
    return jnp.where(i == 0, 0, nT - i)


def _gate_pre(xc, w_ref, hd):
    parts = [jnp.dot(xc[:, h * hd:(h + 1) * hd].astype(MXU_DTYPE), w_ref[h], preferred_element_type=F32)
             for h in range(w_ref.shape[0])]
    return jnp.concatenate(parts, axis=1)


def _lru_fwd(xc, wr, wi, br, bi, lam, *, rev, tms, name):
    T, C = xc.shape
    nH, hd, _ = wr.shape
    nT = T // tms

    def body(xc_ref, wr_ref, wi_ref, br_ref, bi_ref, lam_ref, h_out, r_out, i_out, carry):
        @pl.when(pl.program_id(0) == 0)
        def _():
            carry[...] = jnp.zeros_like(carry)

        x = xc_ref[...]
        r = _sigmoid(_gate_pre(x, wr_ref, hd) + br_ref[...])
        ig = _sigmoid(_gate_pre(x, wi_ref, hd) + bi_ref[...])
        log_a = (-RG_C * r) * _softplus(-lam_ref[...])
        a = jnp.exp(log_a)
        b = jnp.sqrt(-_expm1(2.0 * log_a)) * (ig * x)
        A, B = _scan_tile(a, b, rev, tms)
        h = A * carry[...] + B
        carry[...] = h[0:1, :] if rev else h[tms - 1:tms, :]
        h_out[...] = h
        r_out[...] = r
        i_out[...] = ig

    tile = pl.BlockSpec((tms, C), lambda i: (_scan_order(i, rev, nT), 0))
    wspec = pl.BlockSpec((nH, hd, hd), lambda i: (0, 0, 0))
    vec = pl.BlockSpec((1, C), lambda i: (0, 0))
    shp = jax.ShapeDtypeStruct((T, C), F32)
    return pl.pallas_call(
        body, name=name, grid=(nT,), in_specs=[tile, wspec, wspec, vec, vec, vec], out_specs=(tile, tile, tile),
        out_shape=(shp, shp, shp), scratch_shapes=[pltpu.VMEM((1, C), F32)], compiler_params=_params(("arbitrary",)),
    )(xc, wr, wi, br, bi, lam)


def _lru_bwd(dh, h, r, ig, xc, wr, wi, lam, *, rev, tms, name):
    T, C = xc.shape
    nH, hd, _ = wr.shape
    nT = T // tms
    r8 = tms // 8

    def tile_of(ip):
        return _scan_order(nT - 1 - ip, rev, nT)

    def halo_of(ip):
        i = nT - 1 - ip
        if not rev:
            return jnp.maximum(i * r8 - 1, 0)
        return jnp.where(i <= 1, 0, (nT - i + 1) * r8)

    def body(dh_ref, h_ref, hh_ref, r_ref, i_ref, xc_ref, wr_ref, wi_ref, lam_ref,
             dxc_out, dwr_out, dwi_out, dbr_out, dbi_out, dlam_out, ucarry):
        ip = pl.program_id(0)
        pos = nT - 1 - ip

        @pl.when(ip == 0)
        def _():
            ucarry[...] = jnp.zeros_like(ucarry)
            for o in (dwr_out, dwi_out, dbr_out, dbi_out, dlam_out):
                o[...] = jnp.zeros_like(o)

        rows = lax.broadcasted_iota(jnp.int32, (tms, 1), 0)
        x, rr, ii, hh, dhv = xc_ref[...], r_ref[...], i_ref[...], h_ref[...], dh_ref[...]
        sp = _softplus(-lam_ref[...])
        log_a = (-RG_C * rr) * sp
        a = jnp.exp(log_a)
        a2 = jnp.exp(2.0 * log_a)
        s = jnp.sqrt(-_expm1(2.0 * log_a))
        A, B = _scan_tile(a, a * dhv, not rev, tms)
        u_in = ucarry[...]
        u = A * u_in + B
        if rev:
            u_next = jnp.where(rows == 0, u_in, pltpu.roll(u, 1, 0))
            ucarry[...] = u[tms - 1:tms, :]
            h_halo = jnp.where(pos == 0, 0.0, hh_ref[0:1, :])
            h_prev = jnp.where(rows == tms - 1, h_halo, pltpu.roll(hh, tms - 1, 0))
        else:
            u_next = jnp.where(rows == tms - 1, u_in, pltpu.roll(u, tms - 1, 0))
            ucarry[...] = u[0:1, :]
            h_halo = jnp.where(pos == 0, 0.0, hh_ref[7:8, :])
            h_prev = jnp.where(rows == 0, h_halo, pltpu.roll(hh, 1, 0))
        g = dhv + u_next
        da = g * h_prev
        gated = ii * x
        dgated = g * s
        dlog_a = da * a - (g * gated) * a2 / s
        di = dgated * x
        dr = dlog_a * (-RG_C * sp)
        dlam_out[...] += _colsum(dlog_a * (-RG_C * rr)) * (-_sigmoid(-lam_ref[...]))
        dpr = dr * rr * (1.0 - rr)
        dpi = di * ii * (1.0 - ii)
        dbr_out[...] += _colsum(dpr)
        dbi_out[...] += _colsum(dpi)
        parts = []
        for hx in range(nH):
            sl = slice(hx * hd, (hx + 1) * hd)
            xh, dprh, dpih = x[:, sl].astype(MXU_DTYPE), dpr[:, sl].astype(MXU_DTYPE), dpi[:, sl].astype(MXU_DTYPE)
            nt = (((1,), (1,)), ((), ()))
            tn = (((0,), (0,)), ((), ()))
            parts.append(lax.dot_general(dprh, wr_ref[hx], nt, preferred_element_type=F32)
                         + lax.dot_general(dpih, wi_ref[hx], nt, preferred_element_type=F32))
            dwr_out[hx] += lax.dot_general(xh, dprh, tn, preferred_element_type=F32)
            dwi_out[hx] += lax.dot_general(xh, dpih, tn, preferred_element_type=F32)
        dxc_out[...] = dgated * ii + jnp.concatenate(parts, axis=1)

    tile = pl.BlockSpec((tms, C), lambda ip: (tile_of(ip), 0))
    wspec = pl.BlockSpec((nH, hd, hd), lambda ip: (0, 0, 0))
    vec = pl.BlockSpec((1, C), lambda ip: (0, 0))
    wshape = jax.ShapeDtypeStruct((nH, hd, hd), F32)
    vshape = jax.ShapeDtypeStruct((1, C), F32)
    return pl.pallas_call(
        body, name=name, grid=(nT,),
        in_specs=[tile, tile, pl.BlockSpec((8, C), lambda ip: (halo_of(ip), 0)), tile, tile, tile, wspec, wspec, vec],
        out_specs=(tile, wspec, wspec, vec, vec, vec),
        out_shape=(jax.ShapeDtypeStruct((T, C), F32), wshape, wshape, vshape, vshape, vshape),
        scratch_shapes=[pltpu.VMEM((1, C), F32)], compiler_params=_params(("arbitrary",)),
    )(dh, h, h, r, ig, xc, wr, wi, lam)


def _seg_pos(i, tms):
    seg = jnp.where(i == 0, tms, GRID_W)
    rows = lax.broadcasted_iota(jnp.int32, (tms, 1), 0)
    return rows & (seg - 1), seg


def _convmod_fwd(hf, hb, z, w31, b31, clg, clb, *, tms, name):
    S, T, C = z.shape
    K = w31.shape[0]
    nT = T // tms

    def body(hf_ref, hb_ref, gr_ref, cv_ref, cg_ref, w_ref, b_ref, g_ref, bb_ref, y_out, uc_out):
        i = pl.program_id(0)
        y_out[:, 0:C] = ((hf_ref[...] + hb_ref[...]) * _gelu(gr_ref[...])).astype(MXU_DTYPE)
        u = cv_ref[...] * _sigmoid(cg_ref[...])
        pos, seg = _seg_pos(i, tms)
        acc = jnp.zeros((tms, C), F32) + b_ref[...]
        for k in range(K):
            o = k - K // 2
            valid = (pos + o >= 0) & (pos + o < seg)
            acc = acc + w_ref[k:k + 1, :] * jnp.where(valid, pltpu.roll(u, (-o) % tms, 0), 0.0)
        uc_out[...] = acc
        mu = jnp.mean(acc, axis=-1, keepdims=True)
        d = acc - mu
        var = jnp.mean(d * d, axis=-1, keepdims=True)
        yl = d * lax.rsqrt(var + LN_EPS) * g_ref[...] + bb_ref[...]
        y_out[:, C:2 * C] = (yl * _sigmoid(yl)).astype(MXU_DTYPE)

    tile = pl.BlockSpec((tms, C), lambda i: (i, 0))
    vec = pl.BlockSpec((1, C), lambda i: (0, 0))
    zs = [pl.BlockSpec((None, tms, C), functools.partial(lambda i, s: (s, i, 0), s=s)) for s in (1, 2, 3)]
    return pl.pallas_call(
        body, name=name, grid=(nT,),
        in_specs=[tile, tile, *zs, pl.BlockSpec((K, C), lambda i: (0, 0)), vec, vec, vec],
        out_specs=(pl.BlockSpec((tms, 2 * C), lambda i: (i, 0)), tile),
        out_shape=(jax.ShapeDtypeStruct((T, 2 * C), MXU_DTYPE), jax.ShapeDtypeStruct((T, C), F32)),
        compiler_params=_params(("arbitrary",)),
    )(hf, hb, z, z, z, w31, b31, clg, clb)


def _convmod_bwd(dymix, hf, hb, z, uc, w31, clg, clb, *, tms, name):
    S, T, C = z.shape
    K = w31.shape[0]
    nT = T // tms

    def body(dy_ref, hf_ref, hb_ref, gr_ref, cv_ref, cg_ref, uc_ref, w_ref, g_ref, bb_ref,
             dhs_out, dz_out, dw_out, db_out, dg_out, dbb_out):
        i = pl.program_id(0)

        @pl.when(i == 0)
        def _():
            for o in (dw_out, db_out, dg_out, dbb_out):
                o[...] = jnp.zeros_like(o)

        dyr, dyc = dy_ref[:, 0:C], dy_ref[:, C:2 * C]
        gr = gr_ref[...]
        dhs_out[...] = dyr * _gelu(gr)
        dz_out[0] = jnp.zeros((tms, C), MXU_DTYPE)
        dz_out[1] = (dyr * (hf_ref[...] + hb_ref[...]) * _gelu_grad(gr)).astype(MXU_DTYPE)
        ucv = uc_ref[...]
        mu = jnp.mean(ucv, axis=-1, keepdims=True)
        d = ucv - mu
        var = jnp.mean(d * d, axis=-1, keepdims=True)
        rstd = lax.rsqrt(var + LN_EPS)
        xh = d * rstd
        yl = xh * g_ref[...] + bb_ref[...]
        sg = _sigmoid(yl)
        dyl = dyc * (sg * (1.0 + yl * (1.0 - sg)))
        dg_out[...] += _colsum(dyl * xh)
        dbb_out[...] += _colsum(dyl)
        dxh = dyl * g_ref[...]
        m1 = jnp.mean(dxh, axis=-1, keepdims=True)
        m2 = jnp.mean(dxh * xh, axis=-1, keepdims=True)
        duc = rstd * (dxh - m1 - xh * m2)
        db_out[...] += _colsum(duc)
        cv, sc = cv_ref[...], _sigmoid(cg_ref[...])
        u = cv * sc
        pos, seg = _seg_pos(i, tms)
        du = jnp.zeros((tms, C), F32)
        for k in range(K):
            o = k - K // 2
            fwd_ok = (pos + o >= 0) & (pos + o < seg)
            bwd_ok = (pos - o >= 0) & (pos - o < seg)
            du = du + w_ref[k:k + 1, :] * jnp.where(bwd_ok, pltpu.roll(duc, o % tms, 0), 0.0)
            dw_out[k:k + 1, :] += _colsum(duc * jnp.where(fwd_ok, pltpu.roll(u, (-o) % tms, 0), 0.0))
        dz_out[2] = (du * sc).astype(MXU_DTYPE)
        dz_out[3] = (du * cv * sc * (1.0 - sc)).astype(MXU_DTYPE)

    tile = pl.BlockSpec((tms, C), lambda i: (i, 0))
    vec = pl.BlockSpec((1, C), lambda i: (0, 0))
    kc = pl.BlockSpec((K, C), lambda i: (0, 0))
    zs = [pl.BlockSpec((None, tms, C), functools.partial(lambda i, s: (s, i, 0), s=s)) for s in (1, 2, 3)]
    vshape = jax.ShapeDtypeStruct((1, C), F32)
    return pl.pallas_call(
        body, name=name, grid=(nT,),
        in_specs=[pl.BlockSpec((tms, 2 * C), lambda i: (i, 0)), tile, tile, *zs, tile, kc, vec, vec],
        out_specs=(tile, pl.BlockSpec((S, tms, C), lambda i: (0, i, 0)), kc, vec, vec, vec),
        out_shape=(jax.ShapeDtypeStruct((T, C), F32), jax.ShapeDtypeStruct((S, T, C), MXU_DTYPE),
                   jax.ShapeDtypeStruct((K, C), F32), vshape, vshape, vshape),
        compiler_params=_params(("arbitrary",)),
    )(dymix, hf, hb, z, z, z, uc, w31, clg, clb)


def _loss_head(xhat, g, b, target, *, tc, tms, name):
    T, D = xhat.shape
    nT = T // tms
    nc = tc // tms

    def body(x_ref, g_ref, b_ref, t_ref, dy_out, loss_out):
        i = pl.program_id(0)

        @pl.when(i == 0)
        def _():
            loss_out[...] = jnp.zeros_like(loss_out)

        err = jnp.where(i < nc, 0.0, x_ref[...] * g_ref[...] + b_ref[...] - t_ref[...])
        dy_out[...] = err / D
        loss_out[...] += 0.5 * jnp.sum(jnp.sum(err * err, axis=-1, keepdims=True) / D)

    vec = pl.BlockSpec((1, D), lambda i: (0, 0))
    tile = pl.BlockSpec((tms, D), lambda i: (i, 0))
    return pl.pallas_call(
        body, name=name, grid=(nT,),
        in_specs=[tile, vec, vec, pl.BlockSpec((tms, D), lambda i: (jnp.maximum(i - nc, 0), 0))],
        out_specs=(tile, pl.BlockSpec((8, 128), lambda i: (0, 0))),
        out_shape=(jax.ShapeDtypeStruct((T, D), F32), jax.ShapeDtypeStruct((8, 128), F32)),
        compiler_params=_params(("arbitrary",)),
    )(xhat, g, b, target)


def _ada_fwd(s16, w_ada, b_cols, *, name):
    L, D, Na = w_ada.shape
    tn = _pick(Na, 512)

    def body(s_ref, w_ref, b_ref, o_ref):
        o_ref[...] = jnp.dot(s_ref[...], w_ref[...].astype(MXU_DTYPE), preferred_element_type=F32) + b_ref[...]

    return pl.pallas_call(
        body, name=name, grid=(L, Na // tn),
        in_specs=[pl.BlockSpec((16, D), lambda l, n: (0, 0)), pl.BlockSpec((None, D, tn), lambda l, n: (l, 0, n)),
                  pl.BlockSpec((None, 1, tn), lambda l, n: (l, 0, n))],
        out_specs=pl.BlockSpec((None, 16, tn), lambda l, n: (l, 0, n)),
        out_shape=jax.ShapeDtypeStruct((L, 16, Na), F32), compiler_params=_params(("arbitrary", "arbitrary")),
    )(s16, w_ada, b_cols.reshape(L, 1, Na))


def _ada_bwd(s16, dm16, w_ada, *, name):
    L, D, Na = w_ada.shape
    tn = _pick(Na, 512)

    def body(s_ref, dm_ref, w_ref, dw_out, ds_out):
        @pl.when((pl.program_id(0) == 0) & (pl.program_id(1) == 0))
        def _():
            ds_out[...] = jnp.zeros_like(ds_out)

        dm = dm_ref[...].astype(MXU_DTYPE)
        dw_out[...] = lax.dot_general(s_ref[...], dm, (((0,), (0,)), ((), ())), preferred_element_type=F32)
        ds_out[...] += lax.dot_general(dm, w_ref[...].astype(MXU_DTYPE), (((1,), (1,)), ((), ())), preferred_element_type=F32)

    return pl.pallas_call(
        body, name=name, grid=(L, Na // tn),
        in_specs=[pl.BlockSpec((16, D), lambda l, n: (0, 0)), pl.BlockSpec((None, 16, tn), lambda l, n: (l, 0, n)),
                  pl.BlockSpec((None, D, tn), lambda l, n: (l, 0, n))],
        out_specs=(pl.BlockSpec((None, D, tn), lambda l, n: (l, 0, n)), pl.BlockSpec((16, D), lambda l, n: (0, 0))),
        out_shape=(jax.ShapeDtypeStruct((L, D, Na), F32), jax.ShapeDtypeStruct((16, D), F32)),
        compiler_params=_params(("arbitrary", "arbitrary")),
    )(s16, dm16, w_ada)


def _silu_rows(cvec, *, name):
    R, D = cvec.shape

    def body(c_ref, s_out, ds_out):
        c = c_ref[...]
        sg = _sigmoid(c)
        s_out[...] = (c * sg).astype(MXU_DTYPE)
        ds_out[...] = sg * (1.0 + c * (1.0 - sg))

    return pl.pallas_call(
        body, name=name, out_shape=(jax.ShapeDtypeStruct((R, D), MXU_DTYPE), jax.ShapeDtypeStruct((R, D), F32)),
    )(cvec)


def _sum_leading(v, order, *, name, scale_by=None):
    N, R, C = v.shape
    tr = _pick(R, max(8, ADAM_BLOCK_ELEMS // C), 8)

    def body(v_ref, *rest):
        acc = v_ref[order[0]]
        for j in order[1:]:
            acc = acc + v_ref[j]
        if scale_by is not None:
            acc = acc * rest[0][...]
        rest[-1][...] = acc

    in_specs = [pl.BlockSpec((N, tr, C), lambda i: (0, i, 0))]
    args = [v]
    if scale_by is not None:
        in_specs.append(pl.BlockSpec((tr, C), lambda i: (i, 0)))
        args.append(scale_by)
    return pl.pallas_call(
        body, name=name, grid=(R // tr,), in_specs=in_specs, out_specs=pl.BlockSpec((tr, C), lambda i: (i, 0)),
        out_shape=jax.ShapeDtypeStruct((R, C), F32), compiler_params=_params(("arbitrary",)),
    )(*args)


def _sum_scattered(g, land, chip_arr, l, n_layers, prev, *, name):
    S, R, C = g.shape
    tr = _pick(R, max(8, ADAM_BLOCK_ELEMS // C), 8)

    def body(ch_ref, g_ref, a_ref, b_ref, c_ref, *rest):
        rest[-1][...] = ((g_ref[...] + a_ref[...]) + b_ref[...]) + c_ref[...]

    def slot(d):
        return pl.BlockSpec((None, tr, C), lambda i, ch: ((ch[0] + d) % S, i, 0))

    in_specs = [slot(0), slot(1), slot(2), slot(3)]
    args, aliases = [chip_arr, g, land, land, land], {}
    if prev is not None:
        in_specs.append(pl.BlockSpec(memory_space=pl.ANY))
        args.append(prev)
        aliases = {5: 0}
    grid_spec = pltpu.PrefetchScalarGridSpec(
        num_scalar_prefetch=1, grid=(R // tr,), in_specs=in_specs,
        out_specs=pl.BlockSpec((None, tr, C), lambda i, ch: (l, i, 0)))
    return pl.pallas_call(
        body, name=name, grid_spec=grid_spec, out_shape=jax.ShapeDtypeStruct((n_layers, R, C), F32),
        input_output_aliases=aliases, compiler_params=_params(("arbitrary",)),
    )(*args)


def _adamw(w, g, m, v, g_other=None, *, name):
    shape = w.shape
    C = shape[-1]
    R = w.size // C
    two = g_other is not None
    ins = [t.reshape(R, C) for t in ((w, g, m, v, g_other) if two else (w, g, m, v))]
    tr = _pick(R, max(8, ADAM_BLOCK_ELEMS // C), 8)

    def body(w_ref, g_ref, m_ref, v_ref, *rest):
        d_out, m_out, v_out = rest[-3:]
        gg = g_ref[...]
        if two:
            gg = gg + rest[0][...]
            rest[1][...] = gg
        mn = ADAM_B1 * m_ref[...] + (1.0 - ADAM_B1) * gg
        vn = ADAM_B2 * v_ref[...] + (1.0 - ADAM_B2) * (gg * gg)
        m_hat = mn / (1.0 - ADAM_B1 ** ADAM_STEP)
        v_hat = vn / (1.0 - ADAM_B2 ** ADAM_STEP)
        d_out[...] = -ADAM_LR * (m_hat / (jnp.sqrt(v_hat) + ADAM_EPS) + ADAM_WD * w_ref[...])
        m_out[...] = mn
        v_out[...] = vn

    blk = pl.BlockSpec((tr, C), lambda i: (i, 0))
    shp = jax.ShapeDtypeStruct((R, C), F32)
    n_out = 4 if two else 3
    outs = pl.pallas_call(
        body, name=name, grid=(R // tr,), in_specs=[blk] * len(ins), out_specs=(blk,) * n_out, out_shape=(shp,) * n_out,
        compiler_params=_params(("arbitrary",)),
    )(*ins)
    return tuple(o.reshape(shape) for o in outs)


def _place():
    x, y, c = lax.axis_index("x"), lax.axis_index("y"), lax.axis_index("c")
    return x, y, c, [(1 - x, y), (x, 1 - y), (1 - x, 1 - y)]


def _allgather_small(v, *, name):
    m_per, n = v.shape

    def body(x_ref, out_ref, send_sems, recv_sems, local_sem):
        x, y, c, chips = _place()
        me, sibling = (x, y, c), (x, y, 1 - c)

        def rows(px, py, pc):
            return out_ref.at[pl.ds((4 * px + 2 * py + pc) * m_per, m_per), :]

        def copy(k, block, to, src=None):
            return pltpu.make_async_remote_copy(
                src_ref=rows(*block) if src is None else src, dst_ref=rows(*block),
                send_sem=send_sems.at[k], recv_sem=recv_sems.at[k], device_id=to, device_id_type=MESH)

        mine = pltpu.make_async_copy(x_ref, rows(*me), local_sem)
        mine.start()
        first = [copy(0, me, sibling, src=x_ref)]
        first += [copy(1 + j, me, (*chip, c), src=x_ref) for j, chip in enumerate(chips)]
        for cp in first:
            cp.start()
        passed = [copy(4 + j, (*chip, c), sibling) for j, chip in enumerate(chips)]
        for j, chip in enumerate(chips):
            copy(1 + j, (*chip, c), me).wait_recv()
            passed[j].start()
        copy(0, sibling, me).wait_recv()
        for j, chip in enumerate(chips):
            copy(4 + j, (*chip, 1 - c), me).wait_recv()
        for cp in first + passed:
            cp.wait_send()
        mine.wait()

    return pl.pallas_call(
        body, name=name, out_shape=jax.ShapeDtypeStruct((N_DEV * m_per, n), v.dtype),
        in_specs=[pl.BlockSpec(memory_space=pltpu.VMEM)], out_specs=pl.BlockSpec(memory_space=pltpu.VMEM),
        scratch_shapes=[pltpu.SemaphoreType.DMA((7,)), pltpu.SemaphoreType.DMA((7,)), pltpu.SemaphoreType.DMA],
        compiler_params=pltpu.CompilerParams(vmem_limit_bytes=VMEM_LIMIT_BYTES),
    )(v)


_HBM = pl.BlockSpec(memory_space=pltpu.HBM)
_SEM = pl.BlockSpec(memory_space=pltpu.SEMAPHORE)
_ANY = pl.BlockSpec(memory_space=pl.ANY)
_TOKEN = jax.ShapeDtypeStruct((8, 128), F32)


def _in_hbm(a):
    return pltpu.with_memory_space_constraint(a, pltpu.HBM)


def _place_own(ws, *, name):
    n = len(ws)

    def body(*refs):
        me = 2 * lax.axis_index("x") + lax.axis_index("y")
        cps = [pltpu.make_async_copy(refs[k], refs[n + k].at[me], refs[2 * n].at[k]) for k in range(n)]
        for cp in cps:
            cp.start()
        for cp in cps:
            cp.wait()

    return pl.pallas_call(
        body, name=name, out_shape=[jax.ShapeDtypeStruct((N_CHIPS, *w.shape), w.dtype) for w in ws],
        in_specs=[_ANY] * n, out_specs=[_ANY] * n, scratch_shapes=[pltpu.SemaphoreType.DMA((n,))],
    )(*ws)


def _chip_copies(srcs, lands, sends, recvs, scatter):
    x, y, c, chips = _place()
    me = 2 * x + y
    return [pltpu.make_async_remote_copy(
        src_ref=srcs[k].at[2 * px + py] if scatter else srcs[k], dst_ref=lands[k].at[me],
        send_sem=sends[k].at[j], recv_sem=recvs[k].at[j], device_id=(px, py, c), device_id_type=MESH)
        for k in range(len(srcs)) for j, (px, py) in enumerate(chips)]


def _exchange_start(srcs, lands, after, *, scatter, name):
    n = len(srcs)

    def body(*refs):
        srcs_r, lands_r = refs[:n], refs[n:2 * n]
        outs = refs[-(4 * n + 1):]
        for cp in _chip_copies(srcs_r, lands_r, outs[:n], outs[n:2 * n], scatter):
            cp.start()
        outs[-1][...] = jnp.zeros_like(outs[-1])

    sem = pltpu.SemaphoreType.DMA((N_CHIPS - 1,))
    args = [_in_hbm(a) for a in (*srcs, *lands)]
    in_specs = [_HBM] * (2 * n)
    if after is not None:
        args.append(after)
        in_specs.append(_ANY)
    outs = pl.pallas_call(
        body, name=name,
        out_shape=(*[sem] * (2 * n), *[pltpu.HBM(a.shape, a.dtype) for a in (*srcs, *lands)], _TOKEN),
        in_specs=in_specs, out_specs=(*[_SEM] * (2 * n), *[_HBM] * (2 * n), pl.BlockSpec(memory_space=pltpu.VMEM)),
        input_output_aliases={k: 2 * n + k for k in range(2 * n)},
        compiler_params=pltpu.CompilerParams(has_side_effects=pltpu.SideEffectType.DATAFLOW_SIDE_EFFECTING),
    )(*args)
    return outs[:n], outs[n:2 * n], outs[2 * n:3 * n], outs[3 * n:4 * n], outs[4 * n]


def _exchange_wait(sends, recvs, srcs, lands, after, *, scatter, name):
    n = len(srcs)

    def body(*refs):
        srcs_r, lands_r, sends_r, recvs_r = refs[:n], refs[n:2 * n], refs[2 * n:3 * n], refs[3 * n:4 * n]
        x, y, c, chips = _place()
        for k in range(n):
            for j, (px, py) in enumerate(chips):
                slot = 2 * px + py
                cp = pltpu.make_async_remote_copy(
                    src_ref=srcs_r[k].at[slot] if scatter else srcs_r[k], dst_ref=lands_r[k].at[slot],
                    send_sem=sends_r[k].at[j], recv_sem=recvs_r[k].at[j], device_id=(px, py, c), device_id_type=MESH)
                cp.wait_send()
                cp.wait_recv()

    outs = pl.pallas_call(
        body, name=name, out_shape=[pltpu.HBM(a.shape, a.dtype) for a in (*srcs, *lands)],
        in_specs=[*[_HBM] * (2 * n), *[_SEM] * (2 * n), _ANY], out_specs=[_HBM] * (2 * n),
        input_output_aliases={k: k for k in range(2 * n)},
        compiler_params=pltpu.CompilerParams(has_side_effects=pltpu.SideEffectType.DATAFLOW_SIDE_EFFECTING),
    )(*srcs, *lands, *sends, *recvs, after)
    return outs[:n], outs[n:]


def _swap_sibling(ps, *, name):
    n = len(ps)

    def body(*refs):
        x, y, c, _ = _place()
        cps = [pltpu.make_async_remote_copy(src_ref=refs[k], dst_ref=refs[n + k], send_sem=refs[2 * n].at[k],
                                            recv_sem=refs[2 * n + 1].at[k], device_id=(x, y, 1 - c), device_id_type=MESH)
               for k in range(n)]
        for cp in cps:
            cp.start()
        for cp in cps:
            cp.wait()

    return pl.pallas_call(
        body, name=name, out_shape=[jax.ShapeDtypeStruct(p.shape, p.dtype) for p in ps],
        in_specs=[_ANY] * n, out_specs=[_ANY] * n,
        scratch_shapes=[pltpu.SemaphoreType.DMA((n,)), pltpu.SemaphoreType.DMA((n,))],
    )(*ps)


def kernel(x, c, ctx, c_ctx, w_ada, b_ada, ln_g, ln_b, ff1_in, ff1_out, ff2_in, ff2_out, w_in, conv4_w, conv4_b, w_rg, b_rg, w_ig, b_ig, lam, conv31_w, conv31_b, cln_g, cln_b, w_out, b_out, loss_target, m_c_ctx, m_w_ada, m_b_ada, m_ln_g, m_ln_b, m_ff1_in, m_ff1_out, m_ff2_in, m_ff2_out, m_w_in, m_conv4_w, m_conv4_b, m_w_rg, m_b_rg, m_w_ig, m_b_ig, m_lam, m_conv31_w, m_conv31_b, m_cln_g, m_cln_b, m_w_out, m_b_out, v_c_ctx, v_w_ada, v_b_ada, v_ln_g, v_ln_b, v_ff1_in, v_ff1_out, v_ff2_in, v_ff2_out, v_w_in, v_conv4_w, v_conv4_b, v_w_rg, v_b_rg, v_w_ig, v_b_ig, v_lam, v_conv31_w, v_conv31_b, v_cln_g, v_cln_b, v_w_out, v_b_out):
    weights = dict(c_ctx=c_ctx, w_ada=w_ada, b_ada=b_ada, ln_g=ln_g, ln_b=ln_b, ff1_in=ff1_in, ff1_out=ff1_out,
                   ff2_in=ff2_in, ff2_out=ff2_out, w_in=w_in, conv4_w=conv4_w, conv4_b=conv4_b, w_rg=w_rg, b_rg=b_rg,
                   w_ig=w_ig, b_ig=b_ig, lam=lam, conv31_w=conv31_w, conv31_b=conv31_b, cln_g=cln_g, cln_b=cln_b,
                   w_out=w_out, b_out=b_out)
    m_in = dict(c_ctx=m_c_ctx, w_ada=m_w_ada, b_ada=m_b_ada, ln_g=m_ln_g, ln_b=m_ln_b, ff1_in=m_ff1_in, ff1_out=m_ff1_out,
                ff2_in=m_ff2_in, ff2_out=m_ff2_out, w_in=m_w_in, conv4_w=m_conv4_w, conv4_b=m_conv4_b, w_rg=m_w_rg,
                b_rg=m_b_rg, w_ig=m_w_ig, b_ig=m_b_ig, lam=m_lam, conv31_w=m_conv31_w, conv31_b=m_conv31_b,
                cln_g=m_cln_g, cln_b=m_cln_b, w_out=m_w_out, b_out=m_b_out)
    v_in = dict(c_ctx=v_c_ctx, w_ada=v_w_ada, b_ada=v_b_ada, ln_g=v_ln_g, ln_b=v_ln_b, ff1_in=v_ff1_in, ff1_out=v_ff1_out,
                ff2_in=v_ff2_in, ff2_out=v_ff2_out, w_in=v_w_in, conv4_w=v_conv4_w, conv4_b=v_conv4_b, w_rg=v_w_rg,
                b_rg=v_b_rg, w_ig=v_w_ig, b_ig=v_b_ig, lam=v_lam, conv31_w=v_conv31_w, conv31_b=v_conv31_b,
                cln_g=v_cln_g, cln_b=v_cln_b, w_out=v_w_out, b_out=v_b_out)
    order = list(weights)

    ax, ay, ac = lax.axis_index("x"), lax.axis_index("y"), lax.axis_index("c")
    chip = 2 * ax + ay
    dev = 4 * ax + 2 * ay + ac
    chip_arr = jnp.reshape(chip, (1,)).astype(jnp.int32)

    L, D, Na = w_ada.shape
    Tl, Tc = x.shape[1], ctx.shape[1]
    T = Tc + Tl
    tms = Tc
    C = conv4_w.shape[2] * N_CHIPS
    nH, hds, hd = w_rg.shape[2], w_rg.shape[3], w_rg.shape[4]
    K31 = conv31_w.shape[1]
    assert L == 2 and Tl % tms == 0 and tms % GRID_W == 0 and tms % 8 == 0 and tms & (tms - 1) == 0
    assert hds * N_CHIPS == hd and nH * hd == C and D == 2 * C
    alpha = (2 * L) ** 0.25
    n_mod = N_CHIPS * Na // D

    def shard_cols(full, width):
        return lax.dynamic_slice_in_dim(full, chip * width, width, axis=full.ndim - 1)

    c8 = jnp.zeros((8, D), F32).at[0].set(c[0]).at[1].set(c_ctx)
    c_all = _allgather_small(c8, name="gather_cond").reshape(N_DEV, 8, D)
    c16 = jnp.concatenate([c_all[:, 0], c_ctx[None], jnp.zeros((7, D), F32)], axis=0)
    s16, ds16 = _silu_rows(c16, name="silu_cond")
    mod_part = _ada_fwd(s16, w_ada, shard_cols(b_ada, Na), name="ada_fwd")
    mod_all = _allgather_small(mod_part.reshape(L * 16, Na), name="gather_mod").reshape(N_DEV, L, 16, Na)
    mod_full = jnp.transpose(mod_all[0::2], (1, 2, 0, 3)).reshape(L, 16, N_CHIPS * Na)
    mod_rows = jnp.stack([mod_full[:, 8], lax.dynamic_index_in_dim(mod_full, dev, axis=1, keepdims=False)], axis=1)
    mod = mod_rows.reshape(L, 2, n_mod, D)

    def mvec(l, k):
        return mod[l, :, k, :]

    def full_gate(g):
        return jnp.transpose(g, (1, 2, 0, 3, 4)).reshape(2, nH, hd, hd)

    small_sharded = ("ln_g", "ln_b", "conv4_w", "b_rg", "b_ig", "lam", "conv31_w")
    pieces = {n: weights[n].reshape(-1, weights[n].shape[-1]) for n in small_sharded}
    widths = {n: p.shape[1] for n, p in pieces.items()}
    rows_of = {n: p.shape[0] for n, p in pieces.items()}
    wcat = max(widths.values())
    cat = jnp.concatenate([jnp.pad(p, ((0, 0), (0, wcat - p.shape[1]))) for p in pieces.values()], axis=0)
    rpad = -cat.shape[0] % 8
    cat_all = _allgather_small(jnp.pad(cat, ((0, rpad), (0, 0))), name="gather_small").reshape(N_DEV, -1, wcat)
    full_small, r0 = {}, 0
    for n in small_sharded:
        blk = cat_all[0::2, r0:r0 + rows_of[n], :widths[n]]
        full_small[n] = jnp.transpose(blk, (1, 0, 2)).reshape(rows_of[n], N_CHIPS * widths[n])
        r0 += rows_of[n]
    ln_g_f = full_small["ln_g"].reshape(L, 3, 1, D)
    ln_b_f = full_small["ln_b"].reshape(L, 3, 1, D)
    conv4_w_f = full_small["conv4_w"].reshape(L, 4, C)
    b_rg_f = full_small["b_rg"].reshape(L, 2, 1, C)
    b_ig_f = full_small["b_ig"].reshape(L, 2, 1, C)
    lam_f = full_small["lam"].reshape(L, 2, 1, C)
    conv31_w_f = full_small["conv31_w"].reshape(L, K31, C)

    ones, zeros = jnp.ones((1, D), F32), jnp.zeros((1, D), F32)

    big_names = ("ff1_in", "ff1_out", "w_in", "w_rg", "w_ig", "w_out", "ff2_in", "ff2_out")
    inflight, tok = [], mod[0, 0, 0, :1] + cat_all[0, 0, :1]
    for l in range(L):
        srcs = [weights[n][l].astype(MXU_DTYPE) for n in big_names]
        lands = _place_own(srcs, name=f"place_w_{l}")
        sends, recvs, srcs_t, lands_t, tok = _exchange_start(srcs, lands, tok, scatter=False, name=f"gather_start_{l}")
        inflight.append({n: (sends[k], recvs[k], srcs_t[k], lands_t[k]) for k, n in enumerate(big_names)})
    gw = [{} for _ in range(L)]

    def gathered(l, n, after):
        s, r, src, land = inflight[l][n]
        gw[l][n] = _exchange_wait([s], [r], [src], [land], after, scatter=False, name=f"gather_wait_{n}_{l}")[1][0]
        return gw[l][n]

    s0 = jnp.concatenate([ctx[0], x[0]], axis=0)
    cur = (s0, ones, zeros)
    saved = []
    for l in range(L):
        sv = {"in": cur}
        w = gathered(l, "ff1_in", tok if l == 0 else cur[0])
        h1, gu1, act1 = _in_proj(*cur, mvec(l, 0), mvec(l, 1), w, tc=Tc, swiglu=True, name=f"ffn1_in_{l}")
        xh1, rs1, f1 = _out_proj_ln(act1, gathered(l, "ff1_out", act1), zeros, *cur, mvec(l, 2), 0.5, alpha, tc=Tc, name=f"ffn1_out_{l}")
        sv.update(h1=h1, gu1=gu1, act1=act1, xh1=xh1, rs1=rs1, f1=f1)
        cur1 = (xh1, ln_g_f[l, 0], ln_b_f[l, 0])
        h2, z = _in_proj(*cur1, mvec(l, 3), mvec(l, 4), gathered(l, "w_in", xh1), tc=Tc, swiglu=False, name=f"mix_in_{l}")
        xc = _conv4_fwd(z, conv4_w_f[l], conv4_b[l][None], tms=tms, name=f"conv4_{l}")
        wr_l, wi_l = full_gate(gathered(l, "w_rg", xc)), full_gate(gathered(l, "w_ig", xc))
        sv.update(wr=wr_l, wi=wi_l)
        rec = []
        for d in range(2):
            rec.append(_lru_fwd(xc, wr_l[d], wi_l[d], b_rg_f[l, d], b_ig_f[l, d], lam_f[l, d],
                                rev=bool(d), tms=tms, name=f"lru_{l}_{d}"))
        ymix, uc = _convmod_fwd(rec[0][0], rec[1][0], z, conv31_w_f[l], conv31_b[l][None], cln_g[l][None], cln_b[l][None],
                                tms=tms, name=f"convmod_{l}")
        xh2, rs2, f2 = _out_proj_ln(ymix, gathered(l, "w_out", ymix), b_out[l][None], *cur1, mvec(l, 5), 1.0, alpha, tc=Tc, name=f"mix_out_{l}")
        sv.update(h2=h2, z=z, xc=xc, rec=rec, ymix=ymix, uc=uc, xh2=xh2, rs2=rs2, f2=f2)
        cur2 = (xh2, ln_g_f[l, 1], ln_b_f[l, 1])
        h3, gu3, act3 = _in_proj(*cur2, mvec(l, 6), mvec(l, 7), gathered(l, "ff2_in", xh2), tc=Tc, swiglu=True, name=f"ffn2_in_{l}")
        xh3, rs3, f3 = _out_proj_ln(act3, gathered(l, "ff2_out", act3), zeros, *cur2, mvec(l, 8), 0.5, alpha, tc=Tc, name=f"ffn2_out_{l}")
        sv.update(h3=h3, gu3=gu3, act3=act3, xh3=xh3, rs3=rs3, f3=f3)
        cur = (xh3, ln_g_f[l, 2], ln_b_f[l, 2])
        saved.append(sv)

    ds, loss_blk = _loss_head(*cur, loss_target[0], tc=Tc, tms=tms, name="loss_head")

    reduced = {n: None for n in big_names}
    pending = []

    def start_group(l, names, gs):
        gs = [g.reshape(N_CHIPS, -1, g.shape[-1]) for g in gs]
        lands = [lax.empty(g.shape, F32) for g in gs]
        sends, recvs, srcs_t, lands_t, token = _exchange_start(gs, lands, None, scatter=True, name=f"grad_start_{names[0]}_{l}")
        pending.append((sends, recvs, srcs_t, lands_t, names, l))
        return token

    def finish_group(after):
        sends, recvs, srcs_t, lands_t, names, l = pending.pop(0)
        gs, lands = _exchange_wait(sends, recvs, srcs_t, lands_t, after, scatter=True, name=f"grad_wait_{names[0]}_{l}")
        for k, n in enumerate(names):
            reduced[n] = _sum_scattered(gs[k], lands[k], chip_arr, l, L, reduced[n], name=f"grad_sum_{n}_{l}")

    dmod = [[None] * n_mod for _ in range(L)]
    d_ln_g = [[None] * 3 for _ in range(L)]
    d_ln_b = [[None] * 3 for _ in range(L)]
    small = {n: [None] * L for n in ("conv4_w", "conv4_b", "conv31_w", "conv31_b", "cln_g", "cln_b", "b_out")}
    gate_w = {n: [[None, None] for _ in range(L)] for n in ("w_rg", "w_ig", "b_rg", "b_ig", "lam")}

    def ffn_bwd(ds, l, k, names, sv_in, sfx, after):
        sv = saved[l]
        dy, dres, d_ln_g[l][k], d_ln_b[l][k], dmod[l][3 * k + 2], _ = _ln_bwd(
            ds, sv["xh" + sfx], sv["rs" + sfx], ln_g_f[l, k], sv["f" + sfx], mvec(l, 3 * k + 2), 0.5, alpha, after,
            tc=Tc, name=f"ffn{sfx}_ln_bwd_{l}")
        dg = _nt(dy, gw[l][names[1]], sv["gu" + sfx], name=f"ffn{sfx}_dact_{l}")
        g_out = _wgrad(sv["act" + sfx], dy, cols_sharded=False, name=f"ffn{sfx}_wgrad_out_{l}")
        ds_new, dmod[l][3 * k + 1], dmod[l][3 * k] = _dx_modbwd(
            dg, gw[l][names[0]], dres, *sv_in, mvec(l, 3 * k + 1), tc=Tc, name=f"ffn{sfx}_dx_{l}")
        g_in = _wgrad(sv["h" + sfx], dg, cols_sharded=True, name=f"ffn{sfx}_wgrad_in_{l}")
        return ds_new, start_group(l, (names[1], names[0]), (g_out, g_in))

    def gate_slots(g):
        g = g.reshape(2, nH, N_CHIPS, hds, hd)
        return jnp.transpose(g, (2, 0, 1, 3, 4)).reshape(N_CHIPS, 2 * nH * hds, hd)

    token = None
    for l in reversed(range(L)):
        sv = saved[l]
        cur1 = (sv["xh1"], ln_g_f[l, 0], ln_b_f[l, 0])
        cur2 = (sv["xh2"], ln_g_f[l, 1], ln_b_f[l, 1])
        ds, token = ffn_bwd(ds, l, 2, ("ff2_in", "ff2_out"), cur2, "3", token)
        while len(pending) > 1:
            finish_group(ds)
        dy, dres, d_ln_g[l][1], d_ln_b[l][1], dmod[l][5], small["b_out"][l] = _ln_bwd(
            ds, sv["xh2"], sv["rs2"], ln_g_f[l, 1], sv["f2"], mvec(l, 5), 1.0, alpha, token, tc=Tc, name=f"mix_ln_bwd_{l}")
        dymix = _nt(dy, gw[l]["w_out"], None, name=f"mix_dy_{l}")
        g_w_out = _wgrad(sv["ymix"], dy, cols_sharded=False, name=f"mix_wgrad_out_{l}")
        dhs, dz, small["conv31_w"][l], small["conv31_b"][l], small["cln_g"][l], small["cln_b"][l] = _convmod_bwd(
            dymix, sv["rec"][0][0], sv["rec"][1][0], sv["z"], sv["uc"], conv31_w_f[l], cln_g[l][None], cln_b[l][None],
            tms=tms, name=f"convmod_bwd_{l}")
        dxc = []
        for d in range(2):
            hd_, rd_, id_ = sv["rec"][d]
            o = _lru_bwd(dhs, hd_, rd_, id_, sv["xc"], sv["wr"][d], sv["wi"][d], lam_f[l, d],
                         rev=bool(d), tms=tms, name=f"lru_bwd_{l}_{d}")
            dxc.append(o[0])
            for n, val in zip(("w_rg", "w_ig", "b_rg", "b_ig", "lam"), o[1:]):
                gate_w[n][l][d] = val
        dz, small["conv4_w"][l], small["conv4_b"][l] = _conv4_bwd(dxc[0], dxc[1], sv["z"], conv4_w_f[l], dz, tms=tms, name=f"conv4_bwd_{l}")
        ds, dmod[l][4], dmod[l][3] = _dx_modbwd(dz, gw[l]["w_in"], dres, *cur1, mvec(l, 4), tc=Tc, name=f"mix_dx_{l}")
        g_w_in = _wgrad(sv["h2"], dz, cols_sharded=True, name=f"mix_wgrad_in_{l}")
        token = start_group(l, ("w_out", "w_in", "w_rg", "w_ig"),
                            (g_w_out, g_w_in, gate_slots(jnp.stack(gate_w["w_rg"][l])), gate_slots(jnp.stack(gate_w["w_ig"][l]))))
        while len(pending) > 1:
            finish_group(ds)
        ds, token = ffn_bwd(ds, l, 0, ("ff1_in", "ff1_out"), sv["in"], "1", token)
        while len(pending) > 1:
            finish_group(ds)
    while pending:
        finish_group(ds)

    grad_x = ds[Tc:][None]


    dmod_arr = jnp.stack([jnp.stack(dmod[l], axis=1) for l in range(L)])
    dm_ctx = dmod_arr[:, 0].reshape(L, n_mod * D)
    dm_lat = dmod_arr[:, 1].reshape(L, n_mod * D)
    summed = {
        "loss": loss_blk[0:1, 0:1],
        "dm_ctx": dm_ctx,
        "ln_g": jnp.stack([jnp.concatenate(d_ln_g[l], axis=0) for l in range(L)]),
        "ln_b": jnp.stack([jnp.concatenate(d_ln_b[l], axis=0) for l in range(L)]),
        "conv4_w": jnp.stack(small["conv4_w"]),
        "conv4_b": jnp.concatenate(small["conv4_b"], axis=0),
        "b_rg": jnp.stack([jnp.concatenate(gate_w["b_rg"][l], axis=0) for l in range(L)]),
        "b_ig": jnp.stack([jnp.concatenate(gate_w["b_ig"][l], axis=0) for l in range(L)]),
        "lam": jnp.stack([jnp.concatenate(gate_w["lam"][l], axis=0) for l in range(L)]),
        "conv31_w": jnp.stack(small["conv31_w"]),
        "conv31_b": jnp.concatenate(small["conv31_b"], axis=0),
        "cln_g": jnp.concatenate(small["cln_g"], axis=0),
        "cln_b": jnp.concatenate(small["cln_b"], axis=0),
        "b_out": jnp.concatenate(small["b_out"], axis=0),
        "dm_lat": dm_lat,
    }
    flat = jnp.concatenate([v.reshape(-1) for v in summed.values()])
    n_flat = flat.shape[0]
    n_rows = -(-n_flat // 128)
    n_rows += -n_rows % 8
    vec = jnp.pad(flat, (0, n_rows * 128 - n_flat)).reshape(n_rows, 128)
    vec_all = _allgather_small(vec, name="gather_small_grads").reshape(N_DEV, n_rows, 128)
    vec_sum = _sum_leading(vec_all, tuple(range(N_DEV)), name="sum_small_grads").reshape(-1)
    tot, off = {}, 0
    for n, v in summed.items():
        tot[n] = vec_sum[off:off + v.size].reshape(v.shape)
        if n == "dm_lat":
            dm_lat_all = vec_all.reshape(N_DEV, -1)[:, off:off + v.size].reshape(N_DEV, L, n_mod * D)
        off += v.size
    loss = tot["loss"].reshape(())

    dm16 = jnp.concatenate([jnp.transpose(dm_lat_all, (1, 0, 2)), tot["dm_ctx"][:, None], jnp.zeros((L, 7, n_mod * D), F32)], axis=1)
    g_w_ada, ds16_part = _ada_bwd(s16, shard_cols(dm16, Na), w_ada, name="ada_bwd")
    ds_all = _allgather_small(ds16_part[8:16], name="gather_dcond").reshape(N_DEV, 8, D)
    g_c_ctx = _sum_leading(ds_all[:, 0:1], (0, 2, 4, 6), name="sum_dcond", scale_by=ds16[8:9]).reshape(D)
    g_b_ada = _sum_leading(jnp.stack([tot["dm_lat"], tot["dm_ctx"]]), (0, 1), name="sum_b_ada")

    grads = {"c_ctx": g_c_ctx, "w_ada": g_w_ada, "b_ada": g_b_ada}
    for n in ("ln_g", "ln_b", "conv4_w", "b_rg", "b_ig", "lam", "conv31_w"):
        grads[n] = shard_cols(tot[n], weights[n].shape[-1])
    for n in ("conv4_b", "conv31_b", "cln_g", "cln_b", "b_out"):
        grads[n] = tot[n]
    others = dict(zip(big_names, _swap_sibling([reduced[n] for n in big_names], name="grad_swap")))

    delta, new_m, new_v = {}, {}, {}
    for n in order:
        if n in reduced:
            shp = weights[n].shape
            grads[n], delta[n], new_m[n], new_v[n] = _adamw(weights[n], reduced[n].reshape(shp), m_in[n], v_in[n],
                                                            others[n].reshape(shp), name=f"adamw_{n}")
        else:
            delta[n], new_m[n], new_v[n] = _adamw(weights[n], grads[n], m_in[n], v_in[n], name=f"adamw_{n}")
    return (loss, grad_x, *[grads[n] for n in order], *[delta[n] for n in order],
            *[new_m[n] for n in order], *[new_v[n] for n in order])
```

```python
import functools

import jax
import jax.numpy as jnp
from jax import lax
from jax.experimental import pallas as pl
from jax.experimental.pallas import tpu as pltpu

F32 = jnp.float32
MXU_DTYPE = jnp.bfloat16
GRID_W = 64
RG_C = 8.0
LN_EPS = 1e-6
ADAM_LR, ADAM_B1, ADAM_B2, ADAM_EPS, ADAM_WD, ADAM_STEP = 0.001, 0.9, 0.999, 1e-08, 0.01, 10
N_CHIPS = 4
N_DEV = 8
VMEM_LIMIT_BYTES = 56 * 1024 * 1024
TM_IN, TN_IN = 768, 256
TM_OUT, TK_OUT = 384, 1408
TM_NT, TN_NT = 384, 1408
TM_DX, TK_DX = 384, 1408
TK_WG, TB_WG = 768, 1408
ADAM_BLOCK_ELEMS = 256 * 1024
MESH = pl.DeviceIdType.MESH


def _pick(total, target, mult=128):
    for d in range(min(total, target), 0, -1):
        if total % d == 0 and d % mult == 0:
            return d
    return total


def _params(sem=None):
    kw = dict(vmem_limit_bytes=VMEM_LIMIT_BYTES)
    if sem is not None:
        kw["dimension_semantics"] = sem
    return pltpu.CompilerParams(**kw)


def _sigmoid(x):
    return 1.0 / (1.0 + jnp.exp(-x))


def _gelu(x):
    k = 0.7978845608028654
    t = jnp.tanh(k * (x + 0.044715 * (x * x * x)))
    return 0.5 * x * (1.0 + t)


def _gelu_grad(x):
    k = 0.7978845608028654
    t = jnp.tanh(k * (x + 0.044715 * (x * x * x)))
    return 0.5 * (1.0 + t) + 0.5 * x * (1.0 - t * t) * (k * (1.0 + 3.0 * 0.044715 * x * x))


def _log1p(e):
    u = 1.0 + e
    return jnp.where(u == 1.0, e, jnp.log(u) * (e / jnp.where(u == 1.0, 1.0, u - 1.0)))


def _softplus(y):
    return jnp.maximum(y, 0.0) + _log1p(jnp.exp(-jnp.abs(y)))


def _expm1(x):
    series = x * (1.0 + x * (0.5 + x * (1.0 / 6.0 + x * (1.0 / 24.0 + x * (1.0 / 120.0 + x * (1.0 / 720.0))))))
    return jnp.where(jnp.abs(x) < 0.1, series, jnp.exp(x) - 1.0)


def _rows(i, tm):
    return i * tm + lax.broadcasted_iota(jnp.int32, (tm, 1), 0)


def _sel(isctx, ref):
    return jnp.where(isctx, ref[0:1, :], ref[1:2, :])


def _colsum(v):
    return jnp.sum(v, axis=0, keepdims=True)


def _in_proj(xhat, g_in, b_in, shift, scale, w, *, tc, swiglu, name):
    T, D = xhat.shape
    S, _, Ns = w.shape
    tm, tn = _pick(T, TM_IN, 8), _pick(Ns, TN_IN)
    nps = Ns // tn
    nI = T // tm
    nN = (S // 2 if swiglu else S) * nps

    def body(x_ref, g_ref, b_ref, sh_ref, sc_ref, w_ref, h_out, *rest):
        i, n = pl.program_id(0), pl.program_id(1)
        h_scr = rest[-1]

        @pl.when(n == 0)
        def _():
            s = x_ref[...] * g_ref[...] + b_ref[...]
            isctx = _rows(i, tm) < tc
            h = (s * (1.0 + _sel(isctx, sc_ref)) + _sel(isctx, sh_ref)).astype(MXU_DTYPE)
            h_scr[...] = h
            h_out[...] = h

        h = h_scr[...]
        if swiglu:
            gu_out, act_out = rest[0], rest[1]
            gt = jnp.dot(h, w_ref[0], preferred_element_type=F32)
            ut = jnp.dot(h, w_ref[1], preferred_element_type=F32)
            gu_out[0] = gt.astype(MXU_DTYPE)
            gu_out[1] = ut.astype(MXU_DTYPE)
            act_out[...] = ((gt * _sigmoid(gt)) * ut).astype(MXU_DTYPE)
        else:
            rest[0][...] = jnp.dot(h, w_ref[...], preferred_element_type=F32)

    vec = pl.BlockSpec((1, D), lambda i, n: (0, 0))
    vec2 = pl.BlockSpec((2, D), lambda i, n: (0, 0))
    in_specs = [pl.BlockSpec((tm, D), lambda i, n: (i, 0)), vec, vec, vec2, vec2]
    h_shape = jax.ShapeDtypeStruct((T, D), MXU_DTYPE)
    h_spec = pl.BlockSpec((tm, D), lambda i, n: (i, 0))
    if swiglu:
        F = (S // 2) * Ns
        wv = w.reshape(2, S // 2, *w.shape[1:])
        in_specs.append(pl.BlockSpec((2, None, D, tn), lambda i, n: (0, n // nps, 0, n % nps)))
        out_shape = (h_shape, jax.ShapeDtypeStruct((2, T, F), MXU_DTYPE), jax.ShapeDtypeStruct((T, F), MXU_DTYPE))
        out_specs = (h_spec, pl.BlockSpec((2, tm, tn), lambda i, n: (0, i, n)), pl.BlockSpec((tm, tn), lambda i, n: (i, n)))
    else:
        wv = w
        in_specs.append(pl.BlockSpec((None, D, tn), lambda i, n: (n // nps, 0, n % nps)))
        out_shape = (h_shape, jax.ShapeDtypeStruct((S, T, Ns), F32))
        out_specs = (h_spec, pl.BlockSpec((None, tm, tn), lambda i, n: (n // nps, i, n % nps)))
    return pl.pallas_call(
        body, name=name, grid=(nI, nN), in_specs=in_specs, out_specs=out_specs, out_shape=out_shape,
        scratch_shapes=[pltpu.VMEM((tm, D), MXU_DTYPE)], compiler_params=_params(("arbitrary", "arbitrary")),
    )(xhat, g_in, b_in, shift, scale, wv)


def _out_proj_ln(a, w, bias, xin, g_in, b_in, gvec, gscale, alpha, *, tc, name):
    T, K = a.shape
    S, Ks, D = w.shape
    tm, tk = _pick(T, TM_OUT, 8), _pick(Ks, TK_OUT)
    kps = Ks // tk
    nK = S * kps

    def body(a_ref, w_ref, bias_ref, xin_ref, gi_ref, bi_ref, gv_ref, xh_out, rstd_out, f_out, acc):
        i, k = pl.program_id(0), pl.program_id(1)

        @pl.when(k == 0)
        def _():
            acc[...] = jnp.zeros_like(acc)

        acc[...] += jnp.dot(a_ref[...], w_ref[...], preferred_element_type=F32)

        @pl.when(k == nK - 1)
        def _():
            f = acc[...] + bias_ref[...]
            f_out[...] = f.astype(MXU_DTYPE)
            s = xin_ref[...] * gi_ref[...] + bi_ref[...]
            gv = gscale * _sel(_rows(i, tm) < tc, gv_ref)
            r = alpha * s + gv * f
            mu = jnp.mean(r, axis=-1, keepdims=True)
            d = r - mu
            var = jnp.mean(d * d, axis=-1, keepdims=True)
            rstd = lax.rsqrt(var + LN_EPS)
            xh_out[...] = d * rstd
            rstd_out[...] = rstd

    vec = pl.BlockSpec((1, D), lambda i, k: (0, 0))
    row = pl.BlockSpec((tm, D), lambda i, k: (i, 0))
    return pl.pallas_call(
        body, name=name, grid=(T // tm, nK),
        in_specs=[pl.BlockSpec((tm, tk), lambda i, k: (i, k)),
                  pl.BlockSpec((None, tk, D), lambda i, k: (k // kps, k % kps, 0)),
                  vec, row, vec, vec, pl.BlockSpec((2, D), lambda i, k: (0, 0))],
        out_specs=(row, pl.BlockSpec((tm, 1), lambda i, k: (i, 0)), row),
        out_shape=(jax.ShapeDtypeStruct((T, D), F32), jax.ShapeDtypeStruct((T, 1), F32),
                   jax.ShapeDtypeStruct((T, D), MXU_DTYPE)),
        scratch_shapes=[pltpu.VMEM((tm, D), F32)], compiler_params=_params(("arbitrary", "arbitrary")),
    )(a, w, bias, xin, g_in, b_in, gvec)


def _ln_bwd(ds, xhat, rstd, g_ln, f, gvec, gscale, alpha, after, *, tc, name):
    T, D = ds.shape
    tm = _pick(T, TM_OUT, 8)

    def body(ds_ref, xh_ref, rs_ref, gl_ref, f_ref, gv_ref, *rest):
        dy_out, dres_out, dgl_out, dbl_out, dgv_out, dbias_out = rest[-6:]
        i = pl.program_id(0)

        @pl.when(i == 0)
        def _():
            dgl_out[...] = jnp.zeros_like(dgl_out)
            dbl_out[...] = jnp.zeros_like(dbl_out)
            dgv_out[...] = jnp.zeros_like(dgv_out)
            dbias_out[...] = jnp.zeros_like(dbias_out)

        dsv, xh = ds_ref[...], xh_ref[...]
        dgl_out[...] += _colsum(dsv * xh)
        dbl_out[...] += _colsum(dsv)
        dxh = dsv * gl_ref[...]
        m1 = jnp.mean(dxh, axis=-1, keepdims=True)
        m2 = jnp.mean(dxh * xh, axis=-1, keepdims=True)
        dr = rs_ref[...] * (dxh - m1 - xh * m2)
        dres_out[...] = alpha * dr
        isctx = _rows(i, tm) < tc
        dyv = (gscale * _sel(isctx, gv_ref)) * dr
        dy_out[...] = dyv.astype(MXU_DTYPE)
        dbias_out[...] += _colsum(dyv)
        p = gscale * (dr * f_ref[...].astype(F32))
        dgv_out[0:1, :] += _colsum(jnp.where(isctx, p, 0.0))
        dgv_out[1:2, :] += _colsum(jnp.where(isctx, 0.0, p))

    vec = pl.BlockSpec((1, D), lambda i: (0, 0))
    vec2 = pl.BlockSpec((2, D), lambda i: (0, 0))
    row = pl.BlockSpec((tm, D), lambda i: (i, 0))
    in_specs = [row, row, pl.BlockSpec((tm, 1), lambda i: (i, 0)), vec, row, vec2]
    args = [ds, xhat, rstd, g_ln, f, gvec]
    if after is not None:
        in_specs.append(pl.BlockSpec(memory_space=pl.ANY))
        args.append(after)
    return pl.pallas_call(
        body, name=name, grid=(T // tm,), in_specs=in_specs,
        out_specs=(row, row, vec, vec, vec2, vec),
        out_shape=(jax.ShapeDtypeStruct((T, D), MXU_DTYPE), jax.ShapeDtypeStruct((T, D), F32),
                   jax.ShapeDtypeStruct((1, D), F32), jax.ShapeDtypeStruct((1, D), F32),
                   jax.ShapeDtypeStruct((2, D), F32), jax.ShapeDtypeStruct((1, D), F32)),
        compiler_params=_params(("arbitrary",)),
    )(*args)


def _nt(dy, w, gate_up, *, name):
    T, D = dy.shape
    S, Ks, _ = w.shape
    tm, tn = _pick(T, TM_NT, 8), _pick(Ks, TN_NT)
    nps = Ks // tn
    nN = S * nps
    F = S * Ks

    def body(dy_ref, w_ref, *rest):
        d = lax.dot_general(dy_ref[...], w_ref[...], (((1,), (1,)), ((), ())), preferred_element_type=F32)
        if gate_up is None:
            rest[0][...] = d
        else:
            gu_ref, dg_out = rest
            g, u = gu_ref[0].astype(F32), gu_ref[1].astype(F32)
            sg = _sigmoid(g)
            dg_out[0] = (d * u * (sg * (1.0 + g * (1.0 - sg)))).astype(MXU_DTYPE)
            dg_out[1] = (d * (g * sg)).astype(MXU_DTYPE)

    in_specs = [pl.BlockSpec((tm, D), lambda i, n: (i, 0)),
                pl.BlockSpec((None, tn, D), lambda i, n: (n // nps, n % nps, 0))]
    args = [dy, w]
    if gate_up is None:
        out_shape = jax.ShapeDtypeStruct((T, F), F32)
        out_specs = pl.BlockSpec((tm, tn), lambda i, n: (i, n))
    else:
        Ns = 2 * F // S
        q = Ns // tn
        in_specs.append(pl.BlockSpec((2, tm, tn), lambda i, n: (0, i, n)))
        args.append(gate_up)
        out_shape = jax.ShapeDtypeStruct((2, S // 2, T, Ns), MXU_DTYPE)
        out_specs = pl.BlockSpec((2, None, tm, tn), lambda i, n: (0, n // q, i, n % q))
    out = pl.pallas_call(
        body, name=name, grid=(T // tm, nN), in_specs=in_specs, out_specs=out_specs, out_shape=out_shape,
        compiler_params=_params(("arbitrary", "arbitrary")),
    )(*args)
    return out if gate_up is None else out.reshape(S, T, out.shape[-1])


def _dx_modbwd(dg, w, dres, xhat_in, g_in, b_in, scale, *, tc, name, after=None):
    S, T, Ns = dg.shape
    D = w.shape[1]
    tm, tk = _pick(T, TM_DX, 8), _pick(Ns, TK_DX)
    kps = Ns // tk
    nK = S * kps

    def body(dg_ref, w_ref, dres_ref, xin_ref, gi_ref, bi_ref, sc_ref, *rest):
        ds_out, dsc_out, dsh_out, acc = rest[-4:]
        i, k = pl.program_id(0), pl.program_id(1)

        @pl.when((i == 0) & (k == 0))
        def _():
            dsc_out[...] = jnp.zeros_like(dsc_out)
            dsh_out[...] = jnp.zeros_like(dsh_out)

        @pl.when(k == 0)
        def _():
            acc[...] = jnp.zeros_like(acc)

        acc[...] += lax.dot_general(dg_ref[...], w_ref[...], (((1,), (1,)), ((), ())), preferred_element_type=F32)

        @pl.when(k == nK - 1)
        def _():
            dh = acc[...]
            isctx = _rows(i, tm) < tc
            ds_out[...] = dres_ref[...] + dh * (1.0 + _sel(isctx, sc_ref))
            s = xin_ref[...] * gi_ref[...] + bi_ref[...]
            p = dh * s
            dsc_out[0:1, :] += _colsum(jnp.where(isctx, p, 0.0))
            dsc_out[1:2, :] += _colsum(jnp.where(isctx, 0.0, p))
            dsh_out[0:1, :] += _colsum(jnp.where(isctx, dh, 0.0))
            dsh_out[1:2, :] += _colsum(jnp.where(isctx, 0.0, dh))

    vec = pl.BlockSpec((1, D), lambda i, k: (0, 0))
    vec2 = pl.BlockSpec((2, D), lambda i, k: (0, 0))
    row = pl.BlockSpec((tm, D), lambda i, k: (i, 0))
    in_specs = [pl.BlockSpec((None, tm, tk), lambda i, k: (k // kps, i, k % kps)),
                pl.BlockSpec((None, D, tk), lambda i, k: (k // kps, 0, k % kps)),
                row, row, vec, vec, vec2]
    args = [dg, w, dres, xhat_in, g_in, b_in, scale]
    if after is not None:
        in_specs.append(pl.BlockSpec(memory_space=pl.ANY))
        args.append(after)
    return pl.pallas_call(
        body, name=name, grid=(T // tm, nK), in_specs=in_specs,
        out_specs=(row, vec2, vec2),
        out_shape=(jax.ShapeDtypeStruct((T, D), F32), jax.ShapeDtypeStruct((2, D), F32), jax.ShapeDtypeStruct((2, D), F32)),
        scratch_shapes=[pltpu.VMEM((tm, D), F32)], compiler_params=_params(("arbitrary", "arbitrary")),
    )(*args)


def _wgrad(a, b, *, cols_sharded, name, after=None):
    T = a.shape[0]
    tk = _pick(T, TK_WG, 8)
    if cols_sharded:
        S, _, Ns = b.shape
        D = a.shape[1]
        tb = _pick(Ns, TB_WG)
        grid = (S, Ns // tb, T // tk)
        in_specs = [pl.BlockSpec((tk, D), lambda s, j, k: (k, 0)), pl.BlockSpec((None, tk, tb), lambda s, j, k: (s, k, j))]
        out_shape = jax.ShapeDtypeStruct((S, D, Ns), F32)
        out_specs = pl.BlockSpec((None, D, tb), lambda s, j, k: (s, 0, j))
    else:
        D = b.shape[1]
        S = N_CHIPS
        Ks = a.shape[1] // S
        ta = _pick(Ks, TB_WG)
        q = Ks // ta
        grid = (S, q, T // tk)
        in_specs = [pl.BlockSpec((tk, ta), lambda s, j, k: (k, s * q + j)), pl.BlockSpec((tk, D), lambda s, j, k: (k, 0))]
        out_shape = jax.ShapeDtypeStruct((S, Ks, D), F32)
        out_specs = pl.BlockSpec((None, ta, D), lambda s, j, k: (s, j, 0))

    def body(a_ref, b_ref, *rest):
        o_ref = rest[-1]

        @pl.when(pl.program_id(2) == 0)
        def _():
            o_ref[...] = jnp.zeros_like(o_ref)

        o_ref[...] += lax.dot_general(a_ref[...], b_ref[...], (((0,), (0,)), ((), ())), preferred_element_type=F32)

    args = [a, b]
    if after is not None:
        in_specs.append(pl.BlockSpec(memory_space=pl.ANY))
        args.append(after)
    return pl.pallas_call(
        body, name=name, grid=grid, in_specs=in_specs, out_specs=out_specs, out_shape=out_shape,
        compiler_params=_params(("arbitrary", "arbitrary", "arbitrary")),
    )(*args)


def _halo_specs(tms, T, C, slot=None):
    r8 = tms // 8
    last8 = T // 8 - 1
    if slot is None:
        return (pl.BlockSpec((8, C), lambda i: (jnp.maximum(i * r8 - 1, 0), 0)),
                pl.BlockSpec((tms, C), lambda i: (i, 0)),
                pl.BlockSpec((8, C), lambda i: (jnp.minimum((i + 1) * r8, last8), 0)))
    return (pl.BlockSpec((None, 8, C), lambda i: (slot, jnp.maximum(i * r8 - 1, 0), 0)),
            pl.BlockSpec((None, tms, C), lambda i: (slot, i, 0)),
            pl.BlockSpec((None, 8, C), lambda i: (slot, jnp.minimum((i + 1) * r8, last8), 0)))


def _extended(prev_ref, cur, next_ref, i, n_tiles):
    first = (i == 0) | (i == 1)
    last = (i == 0) | (i == n_tiles - 1)
    pv = jnp.where(first, 0.0, prev_ref[...])
    nx = jnp.where(last, 0.0, next_ref[...])
    return jnp.concatenate([pv, cur, nx], axis=0)


def _shifted(ext, o, tms):
    n = tms + 16
    return pltpu.roll(ext, (-o) % n, 0)[8:8 + tms]


def _conv4_fwd(z, w4, b4, *, tms, name):
    S, T, C = z.shape
    nT = T // tms

    def body(p_ref, c_ref, n_ref, w_ref, b_ref, o_ref):
        i = pl.program_id(0)
        ext = _extended(p_ref, c_ref[...], n_ref, i, nT)
        acc = jnp.zeros((tms, C), F32) + b_ref[...]
        for k in range(4):
            acc = acc + w_ref[k:k + 1, :] * _shifted(ext, k - 2, tms)
        o_ref[...] = acc

    return pl.pallas_call(
        body, name=name, grid=(nT,),
        in_specs=[*_halo_specs(tms, T, C, 0), pl.BlockSpec((4, C), lambda i: (0, 0)), pl.BlockSpec((1, C), lambda i: (0, 0))],
        out_specs=pl.BlockSpec((tms, C), lambda i: (i, 0)), out_shape=jax.ShapeDtypeStruct((T, C), F32),
        compiler_params=_params(("arbitrary",)),
    )(z, z, z, w4, b4)


def _conv4_bwd(dxa, dxb, z, w4, dz, *, tms, name):
    S, T, C = z.shape
    nT = T // tms

    def body(pa, ca, na, pb, cb, nb, px, cx, nx, w_ref, dz_in, dz_out, dw_out, db_out):
        i = pl.program_id(0)

        @pl.when(i == 0)
        def _():
            dw_out[...] = jnp.zeros_like(dw_out)
            db_out[...] = jnp.zeros_like(db_out)

        dcur = ca[...] + cb[...]
        first = (i == 0) | (i == 1)
        last = (i == 0) | (i == nT - 1)
        dext = jnp.concatenate([jnp.where(first, 0.0, pa[...] + pb[...]), dcur, jnp.where(last, 0.0, na[...] + nb[...])], axis=0)
        xext = _extended(px, cx[...], nx, i, nT)
        acc = jnp.zeros((tms, C), F32)
        for k in range(4):
            acc = acc + w_ref[k:k + 1, :] * _shifted(dext, 2 - k, tms)
            dw_out[k:k + 1, :] += _colsum(dcur * _shifted(xext, k - 2, tms))
        db_out[...] += _colsum(dcur)
        dz_out[...] = acc.astype(MXU_DTYPE)

    h = _halo_specs(tms, T, C)
    return pl.pallas_call(
        body, name=name, grid=(nT,),
        in_specs=[*h, *h, *_halo_specs(tms, T, C, 0), pl.BlockSpec((4, C), lambda i: (0, 0)), pl.BlockSpec(memory_space=pl.ANY)],
        out_specs=(pl.BlockSpec((None, tms, C), lambda i: (0, i, 0)), pl.BlockSpec((4, C), lambda i: (0, 0)),
                   pl.BlockSpec((1, C), lambda i: (0, 0))),
        out_shape=(jax.ShapeDtypeStruct(dz.shape, dz.dtype), jax.ShapeDtypeStruct((4, C), F32), jax.ShapeDtypeStruct((1, C), F32)),
        input_output_aliases={10: 0}, compiler_params=_params(("arbitrary",)),
    )(dxa, dxa, dxa, dxb, dxb, dxb, z, z, z, w4, dz)


def _scan_tile(a, b, rev, n):
    rows = lax.broadcasted_iota(jnp.int32, (n, 1), 0)
    sft = 1
    while sft < n:
        if rev:
            a_sh, b_sh, valid = pltpu.roll(a, n - sft, 0), pltpu.roll(b, n - sft, 0), rows < n - sft
        else:
            a_sh, b_sh, valid = pltpu.roll(a, sft, 0), pltpu.roll(b, sft, 0), rows >= sft
        b = a * jnp.where(valid, b_sh, 0.0) + b
        a = a * jnp.where(valid, a_sh, 1.0)
        sft *= 2
    return a, b


def _scan_order(i, rev, nT):
    if not rev:
        return i
    return jnp.where(i == 0, 0, nT - i)


def _gate_pre(xc, w_ref, hd):
    parts = [jnp.dot(xc[:, h * hd:(h + 1) * hd].astype(MXU_DTYPE), w_ref[h], preferred_element_type=F32)
             for h in range(w_ref.shape[0])]
    return jnp.concatenate(parts, axis=1)


def _lru_fwd(xc, wr, wi, br, bi, lam, *, rev, tms, name):
    T, C = xc.shape
    nH, hd, _ = wr.shape
    nT = T // tms

    def body(xc_ref, wr_ref, wi_ref, br_ref, bi_ref, lam_ref, h_out, r_out, i_out, carry):
        @pl.when(pl.program_id(0) == 0)
        def _():
            carry[...] = jnp.zeros_like(carry)

        x = xc_ref[...]
        r = _sigmoid(_gate_pre(x, wr_ref, hd) + br_ref[...])
        ig = _sigmoid(_gate_pre(x, wi_ref, hd) + bi_ref[...])
        log_a = (-RG_C * r) * _softplus(-lam_ref[...])
        a = jnp.exp(log_a)
        b = jnp.sqrt(-_expm1(2.0 * log_a)) * (ig * x)
        A, B = _scan_tile(a, b, rev, tms)
        h = A * carry[...] + B
        carry[...] = h[0:1, :] if rev else h[tms - 1:tms, :]
        h_out[...] = h
        r_out[...] = r
        i_out[...] = ig

    tile = pl.BlockSpec((tms, C), lambda i: (_scan_order(i, rev, nT), 0))
    wspec = pl.BlockSpec((nH, hd, hd), lambda i: (0, 0, 0))
    vec = pl.BlockSpec((1, C), lambda i: (0, 0))
    shp = jax.ShapeDtypeStruct((T, C), F32)
    return pl.pallas_call(
        body, name=name, grid=(nT,), in_specs=[tile, wspec, wspec, vec, vec, vec], out_specs=(tile, tile, tile),
        out_shape=(shp, shp, shp), scratch_shapes=[pltpu.VMEM((1, C), F32)], compiler_params=_params(("arbitrary",)),
    )(xc, wr, wi, br, bi, lam)


def _lru_bwd(dh, h, r, ig, xc, wr, wi, lam, *, rev, tms, name):
    T, C = xc.shape
    nH, hd, _ = wr.shape
    nT = T // tms
    r8 = tms // 8

    def tile_of(ip):
        return _scan_order(nT - 1 - ip, rev, nT)

    def halo_of(ip):
        i = nT - 1 - ip
        if not rev:
            return jnp.maximum(i * r8 - 1, 0)
        return jnp.where(i <= 1, 0, (nT - i + 1) * r8)

    def body(dh_ref, h_ref, hh_ref, r_ref, i_ref, xc_ref, wr_ref, wi_ref, lam_ref,
             dxc_out, dwr_out, dwi_out, dbr_out, dbi_out, dlam_out, ucarry):
        ip = pl.program_id(0)
        pos = nT - 1 - ip

        @pl.when(ip == 0)
        def _():
            ucarry[...] = jnp.zeros_like(ucarry)
            for o in (dwr_out, dwi_out, dbr_out, dbi_out, dlam_out):
                o[...] = jnp.zeros_like(o)

        rows = lax.broadcasted_iota(jnp.int32, (tms, 1), 0)
        x, rr, ii, hh, dhv = xc_ref[...], r_ref[...], i_ref[...], h_ref[...], dh_ref[...]
        sp = _softplus(-lam_ref[...])
        log_a = (-RG_C * rr) * sp
        a = jnp.exp(log_a)
        a2 = jnp.exp(2.0 * log_a)
        s = jnp.sqrt(-_expm1(2.0 * log_a))
        A, B = _scan_tile(a, a * dhv, not rev, tms)
        u_in = ucarry[...]
        u = A * u_in + B
        if rev:
            u_next = jnp.where(rows == 0, u_in, pltpu.roll(u, 1, 0))
            ucarry[...] = u[tms - 1:tms, :]
            h_halo = jnp.where(pos == 0, 0.0, hh_ref[0:1, :])
            h_prev = jnp.where(rows == tms - 1, h_halo, pltpu.roll(hh, tms - 1, 0))
        else:
            u_next = jnp.where(rows == tms - 1, u_in, pltpu.roll(u, tms - 1, 0))
            ucarry[...] = u[0:1, :]
            h_halo = jnp.where(pos == 0, 0.0, hh_ref[7:8, :])
            h_prev = jnp.where(rows == 0, h_halo, pltpu.roll(hh, 1, 0))
        g = dhv + u_next
        da = g * h_prev
        gated = ii * x
        dgated = g * s
        dlog_a = da * a - (g * gated) * a2 / s
        di = dgated * x
        dr = dlog_a * (-RG_C * sp)
        dlam_out[...] += _colsum(dlog_a * (-RG_C * rr)) * (-_sigmoid(-lam_ref[...]))
        dpr = dr * rr * (1.0 - rr)
        dpi = di * ii * (1.0 - ii)
        dbr_out[...] += _colsum(dpr)
        dbi_out[...] += _colsum(dpi)
        parts = []
        for hx in range(nH):
            sl = slice(hx * hd, (hx + 1) * hd)
            xh, dprh, dpih = x[:, sl].astype(MXU_DTYPE), dpr[:, sl].astype(MXU_DTYPE), dpi[:, sl].astype(MXU_DTYPE)
            nt = (((1,), (1,)), ((), ()))
            tn = (((0,), (0,)), ((), ()))
            parts.append(lax.dot_general(dprh, wr_ref[hx], nt, preferred_element_type=F32)
                         + lax.dot_general(dpih, wi_ref[hx], nt, preferred_element_type=F32))
            dwr_out[hx] += lax.dot_general(xh, dprh, tn, preferred_element_type=F32)
            dwi_out[hx] += lax.dot_general(xh, dpih, tn, preferred_element_type=F32)
        dxc_out[...] = dgated * ii + jnp.concatenate(parts, axis=1)

    tile = pl.BlockSpec((tms, C), lambda ip: (tile_of(ip), 0))
    wspec = pl.BlockSpec((nH, hd, hd), lambda ip: (0, 0, 0))
    vec = pl.BlockSpec((1, C), lambda ip: (0, 0))
    wshape = jax.ShapeDtypeStruct((nH, hd, hd), F32)
    vshape = jax.ShapeDtypeStruct((1, C), F32)
    return pl.pallas_call(
        body, name=name, grid=(nT,),
        in_specs=[tile, tile, pl.BlockSpec((8, C), lambda ip: (halo_of(ip), 0)), tile, tile, tile, wspec, wspec, vec],
        out_specs=(tile, wspec, wspec, vec, vec, vec),
        out_shape=(jax.ShapeDtypeStruct((T, C), F32), wshape, wshape, vshape, vshape, vshape),
        scratch_shapes=[pltpu.VMEM((1, C), F32)], compiler_params=_params(("arbitrary",)),
    )(dh, h, h, r, ig, xc, wr, wi, lam)


def _seg_pos(i, tms):
    seg = jnp.where(i == 0, tms, GRID_W)
    rows = lax.broadcasted_iota(jnp.int32, (tms, 1), 0)
    return rows & (seg - 1), seg


def _convmod_fwd(hf, hb, z, w31, b31, clg, clb, *, tms, name):
    S, T, C = z.shape
    K = w31.shape[0]
    nT = T // tms

    def body(hf_ref, hb_ref, gr_ref, cv_ref, cg_ref, w_ref, b_ref, g_ref, bb_ref, y_out, uc_out):
        i = pl.program_id(0)
        y_out[:, 0:C] = ((hf_ref[...] + hb_ref[...]) * _gelu(gr_ref[...])).astype(MXU_DTYPE)
        u = cv_ref[...] * _sigmoid(cg_ref[...])
        pos, seg = _seg_pos(i, tms)
        acc = jnp.zeros((tms, C), F32) + b_ref[...]
        for k in range(K):
            o = k - K // 2
            valid = (pos + o >= 0) & (pos + o < seg)
            acc = acc + w_ref[k:k + 1, :] * jnp.where(valid, pltpu.roll(u, (-o) % tms, 0), 0.0)
        uc_out[...] = acc
        mu = jnp.mean(acc, axis=-1, keepdims=True)
        d = acc - mu
        var = jnp.mean(d * d, axis=-1, keepdims=True)
        yl = d * lax.rsqrt(var + LN_EPS) * g_ref[...] + bb_ref[...]
        y_out[:, C:2 * C] = (yl * _sigmoid(yl)).astype(MXU_DTYPE)

    tile = pl.BlockSpec((tms, C), lambda i: (i, 0))
    vec = pl.BlockSpec((1, C), lambda i: (0, 0))
    zs = [pl.BlockSpec((None, tms, C), functools.partial(lambda i, s: (s, i, 0), s=s)) for s in (1, 2, 3)]
    return pl.pallas_call(
        body, name=name, grid=(nT,),
        in_specs=[tile, tile, *zs, pl.BlockSpec((K, C), lambda i: (0, 0)), vec, vec, vec],
        out_specs=(pl.BlockSpec((tms, 2 * C), lambda i: (i, 0)), tile),
        out_shape=(jax.ShapeDtypeStruct((T, 2 * C), MXU_DTYPE), jax.ShapeDtypeStruct((T, C), F32)),
        compiler_params=_params(("arbitrary",)),
    )(hf, hb, z, z, z, w31, b31, clg, clb)


def _convmod_bwd(dymix, hf, hb, z, uc, w31, clg, clb, *, tms, name):
    S, T, C = z.shape
    K = w31.shape[0]
    nT = T // tms

    def body(dy_ref, hf_ref, hb_ref, gr_ref, cv_ref, cg_ref, uc_ref, w_ref, g_ref, bb_ref,
             dhs_out, dz_out, dw_out, db_out, dg_out, dbb_out):
        i = pl.program_id(0)

        @pl.when(i == 0)
        def _():
            for o in (dw_out, db_out, dg_out, dbb_out):
                o[...] = jnp.zeros_like(o)

        dyr, dyc = dy_ref[:, 0:C], dy_ref[:, C:2 * C]
        gr = gr_ref[...]
        dhs_out[...] = dyr * _gelu(gr)
        dz_out[0] = jnp.zeros((tms, C), MXU_DTYPE)
        dz_out[1] = (dyr * (hf_ref[...] + hb_ref[...]) * _gelu_grad(gr)).astype(MXU_DTYPE)
        ucv = uc_ref[...]
        mu = jnp.mean(ucv, axis=-1, keepdims=True)
        d = ucv - mu
        var = jnp.mean(d * d, axis=-1, keepdims=True)
        rstd = lax.rsqrt(var + LN_EPS)
        xh = d * rstd
        yl = xh * g_ref[...] + bb_ref[...]
        sg = _sigmoid(yl)
        dyl = dyc * (sg * (1.0 + yl * (1.0 - sg)))
        dg_out[...] += _colsum(dyl * xh)
        dbb_out[...] += _colsum(dyl)
        dxh = dyl * g_ref[...]
        m1 = jnp.mean(dxh, axis=-1, keepdims=True)
        m2 = jnp.mean(dxh * xh, axis=-1, keepdims=True)
        duc = rstd * (dxh - m1 - xh * m2)
        db_out[...] += _colsum(duc)
        cv, sc = cv_ref[...], _sigmoid(cg_ref[...])
        u = cv * sc
        pos, seg = _seg_pos(i, tms)
        du = jnp.zeros((tms, C), F32)
        for k in range(K):
            o = k - K // 2
            fwd_ok = (pos + o >= 0) & (pos + o < seg)
            bwd_ok = (pos - o >= 0) & (pos - o < seg)
            du = du + w_ref[k:k + 1, :] * jnp.where(bwd_ok, pltpu.roll(duc, o % tms, 0), 0.0)
            dw_out[k:k + 1, :] += _colsum(duc * jnp.where(fwd_ok, pltpu.roll(u, (-o) % tms, 0), 0.0))
        dz_out[2] = (du * sc).astype(MXU_DTYPE)
        dz_out[3] = (du * cv * sc * (1.0 - sc)).astype(MXU_DTYPE)

    tile = pl.BlockSpec((tms, C), lambda i: (i, 0))
    vec = pl.BlockSpec((1, C), lambda i: (0, 0))
    kc = pl.BlockSpec((K, C), lambda i: (0, 0))
    zs = [pl.BlockSpec((None, tms, C), functools.partial(lambda i, s: (s, i, 0), s=s)) for s in (1, 2, 3)]
    vshape = jax.ShapeDtypeStruct((1, C), F32)
    return pl.pallas_call(
        body, name=name, grid=(nT,),
        in_specs=[pl.BlockSpec((tms, 2 * C), lambda i: (i, 0)), tile, tile, *zs, tile, kc, vec, vec],
        out_specs=(tile, pl.BlockSpec((S, tms, C), lambda i: (0, i, 0)), kc, vec, vec, vec),
        out_shape=(jax.ShapeDtypeStruct((T, C), F32), jax.ShapeDtypeStruct((S, T, C), MXU_DTYPE),
                   jax.ShapeDtypeStruct((K, C), F32), vshape, vshape, vshape),
        compiler_params=_params(("arbitrary",)),
    )(dymix, hf, hb, z, z, z, uc, w31, clg, clb)


def _loss_head(xhat, g, b, target, *, tc, tms, name):
    T, D = xhat.shape
    nT = T // tms
    nc = tc // tms

    def body(x_ref, g_ref, b_ref, t_ref, dy_out, loss_out):
        i = pl.program_id(0)

        @pl.when(i == 0)
        def _():
            loss_out[...] = jnp.zeros_like(loss_out)

        err = jnp.where(i < nc, 0.0, x_ref[...] * g_ref[...] + b_ref[...] - t_ref[...])
        dy_out[...] = err / D
        loss_out[...] += 0.5 * jnp.sum(jnp.sum(err * err, axis=-1, keepdims=True) / D)

    vec = pl.BlockSpec((1, D), lambda i: (0, 0))
    tile = pl.BlockSpec((tms, D), lambda i: (i, 0))
    return pl.pallas_call(
        body, name=name, grid=(nT,),
        in_specs=[tile, vec, vec, pl.BlockSpec((tms, D), lambda i: (jnp.maximum(i - nc, 0), 0))],
        out_specs=(tile, pl.BlockSpec((8, 128), lambda i: (0, 0))),
        out_shape=(jax.ShapeDtypeStruct((T, D), F32), jax.ShapeDtypeStruct((8, 128), F32)),
        compiler_params=_params(("arbitrary",)),
    )(xhat, g, b, target)


def _ada_fwd(s16, w_ada, b_cols, *, name):
    L, D, Na = w_ada.shape
    tn = _pick(Na, 512)

    def body(s_ref, w_ref, b_ref, o_ref):
        o_ref[...] = jnp.dot(s_ref[...], w_ref[...].astype(MXU_DTYPE), preferred_element_type=F32) + b_ref[...]

    return pl.pallas_call(
        body, name=name, grid=(L, Na // tn),
        in_specs=[pl.BlockSpec((16, D), lambda l, n: (0, 0)), pl.BlockSpec((None, D, tn), lambda l, n: (l, 0, n)),
                  pl.BlockSpec((None, 1, tn), lambda l, n: (l, 0, n))],
        out_specs=pl.BlockSpec((None, 16, tn), lambda l, n: (l, 0, n)),
        out_shape=jax.ShapeDtypeStruct((L, 16, Na), F32), compiler_params=_params(("arbitrary", "arbitrary")),
    )(s16, w_ada, b_cols.reshape(L, 1, Na))


def _ada_bwd(s16, dm16, w_ada, *, name):
    L, D, Na = w_ada.shape
    tn = _pick(Na, 512)

    def body(s_ref, dm_ref, w_ref, dw_out, ds_out):
        @pl.when((pl.program_id(0) == 0) & (pl.program_id(1) == 0))
        def _():
            ds_out[...] = jnp.zeros_like(ds_out)

        dm = dm_ref[...].astype(MXU_DTYPE)
        dw_out[...] = lax.dot_general(s_ref[...], dm, (((0,), (0,)), ((), ())), preferred_element_type=F32)
        ds_out[...] += lax.dot_general(dm, w_ref[...].astype(MXU_DTYPE), (((1,), (1,)), ((), ())), preferred_element_type=F32)

    return pl.pallas_call(
        body, name=name, grid=(L, Na // tn),
        in_specs=[pl.BlockSpec((16, D), lambda l, n: (0, 0)), pl.BlockSpec((None, 16, tn), lambda l, n: (l, 0, n)),
                  pl.BlockSpec((None, D, tn), lambda l, n: (l, 0, n))],
        out_specs=(pl.BlockSpec((None, D, tn), lambda l, n: (l, 0, n)), pl.BlockSpec((16, D), lambda l, n: (0, 0))),
        out_shape=(jax.ShapeDtypeStruct((L, D, Na), F32), jax.ShapeDtypeStruct((16, D), F32)),
        compiler_params=_params(("arbitrary", "arbitrary")),
    )(s16, dm16, w_ada)


def _silu_rows(cvec, *, name):
    R, D = cvec.shape

    def body(c_ref, s_out, ds_out):
        c = c_ref[...]
        sg = _sigmoid(c)
        s_out[...] = (c * sg).astype(MXU_DTYPE)
        ds_out[...] = sg * (1.0 + c * (1.0 - sg))

    return pl.pallas_call(
        body, name=name, out_shape=(jax.ShapeDtypeStruct((R, D), MXU_DTYPE), jax.ShapeDtypeStruct((R, D), F32)),
    )(cvec)


def _sum_leading(v, order, *, name, scale_by=None):
    N, R, C = v.shape
    tr = _pick(R, max(8, ADAM_BLOCK_ELEMS // C), 8)

    def body(v_ref, *rest):
        acc = v_ref[order[0]]
        for j in order[1:]:
            acc = acc + v_ref[j]
        if scale_by is not None:
            acc = acc * rest[0][...]
        rest[-1][...] = acc

    in_specs = [pl.BlockSpec((N, tr, C), lambda i: (0, i, 0))]
    args = [v]
    if scale_by is not None:
        in_specs.append(pl.BlockSpec((tr, C), lambda i: (i, 0)))
        args.append(scale_by)
    return pl.pallas_call(
        body, name=name, grid=(R // tr,), in_specs=in_specs, out_specs=pl.BlockSpec((tr, C), lambda i: (i, 0)),
        out_shape=jax.ShapeDtypeStruct((R, C), F32), compiler_params=_params(("arbitrary",)),
    )(*args)


def _sum_scattered(g, land, chip_arr, l, n_layers, prev, *, name):
    S, R, C = g.shape
    tr = _pick(R, max(8, ADAM_BLOCK_ELEMS // C), 8)

    def body(ch_ref, g_ref, a_ref, b_ref, c_ref, *rest):
        rest[-1][...] = ((g_ref[...] + a_ref[...]) + b_ref[...]) + c_ref[...]

    def slot(d):
        return pl.BlockSpec((None, tr, C), lambda i, ch: ((ch[0] + d) % S, i, 0))

    in_specs = [slot(0), slot(1), slot(2), slot(3)]
    args, aliases = [chip_arr, g, land, land, land], {}
    if prev is not None:
        in_specs.append(pl.BlockSpec(memory_space=pl.ANY))
        args.append(prev)
        aliases = {5: 0}
    grid_spec = pltpu.PrefetchScalarGridSpec(
        num_scalar_prefetch=1, grid=(R // tr,), in_specs=in_specs,
        out_specs=pl.BlockSpec((None, tr, C), lambda i, ch: (l, i, 0)))
    return pl.pallas_call(
        body, name=name, grid_spec=grid_spec, out_shape=jax.ShapeDtypeStruct((n_layers, R, C), F32),
        input_output_aliases=aliases, compiler_params=_params(("arbitrary",)),
    )(*args)


def _adamw(w, g, m, v, g_other=None, *, name):
    shape = w.shape
    C = shape[-1]
    R = w.size // C
    two = g_other is not None
    ins = [t.reshape(R, C) for t in ((w, g, m, v, g_other) if two else (w, g, m, v))]
    tr = _pick(R, max(8, ADAM_BLOCK_ELEMS // C), 8)

    def body(w_ref, g_ref, m_ref, v_ref, *rest):
        d_out, m_out, v_out = rest[-3:]
        gg = g_ref[...]
        if two:
            gg = gg + rest[0][...]
            rest[1][...] = gg
        mn = ADAM_B1 * m_ref[...] + (1.0 - ADAM_B1) * gg
        vn = ADAM_B2 * v_ref[...] + (1.0 - ADAM_B2) * (gg * gg)
        m_hat = mn / (1.0 - ADAM_B1 ** ADAM_STEP)
        v_hat = vn / (1.0 - ADAM_B2 ** ADAM_STEP)
        d_out[...] = -ADAM_LR * (m_hat / (jnp.sqrt(v_hat) + ADAM_EPS) + ADAM_WD * w_ref[...])
        m_out[...] = mn
        v_out[...] = vn

    blk = pl.BlockSpec((tr, C), lambda i: (i, 0))
    shp = jax.ShapeDtypeStruct((R, C), F32)
    n_out = 4 if two else 3
    outs = pl.pallas_call(
        body, name=name, grid=(R // tr,), in_specs=[blk] * len(ins), out_specs=(blk,) * n_out, out_shape=(shp,) * n_out,
        compiler_params=_params(("arbitrary",)),
    )(*ins)
    return tuple(o.reshape(shape) for o in outs)


def _place():
    x, y, c = lax.axis_index("x"), lax.axis_index("y"), lax.axis_index("c")
    return x, y, c, [(1 - x, y), (x, 1 - y), (1 - x, 1 - y)]


def _allgather_small(v, *, name, after=None):
    m_per, n = v.shape

    def body(x_ref, *rest):
        out_ref, send_sems, recv_sems, local_sem = rest[-4:]
        x, y, c, chips = _place()
        me, sibling = (x, y, c), (x, y, 1 - c)

        def rows(px, py, pc):
            return out_ref.at[pl.ds((4 * px + 2 * py + pc) * m_per, m_per), :]

        def copy(k, block, to, src=None):
            return pltpu.make_async_remote_copy(
                src_ref=rows(*block) if src is None else src, dst_ref=rows(*block),
                send_sem=send_sems.at[k], recv_sem=recv_sems.at[k], device_id=to, device_id_type=MESH)

        mine = pltpu.make_async_copy(x_ref, rows(*me), local_sem)
        mine.start()
        first = [copy(0, me, sibling, src=x_ref)]
        first += [copy(1 + j, me, (*chip, c), src=x_ref) for j, chip in enumerate(chips)]
        for cp in first:
            cp.start()
        passed = [copy(4 + j, (*chip, c), sibling) for j, chip in enumerate(chips)]
        for j, chip in enumerate(chips):
            copy(1 + j, (*chip, c), me).wait_recv()
            passed[j].start()
        copy(0, sibling, me).wait_recv()
        for j, chip in enumerate(chips):
            copy(4 + j, (*chip, 1 - c), me).wait_recv()
        for cp in first + passed:
            cp.wait_send()
        mine.wait()

    in_specs, args = [pl.BlockSpec(memory_space=pltpu.VMEM)], [v]
    if after is not None:
        in_specs.append(pl.BlockSpec(memory_space=pl.ANY))
        args.append(after)
    return pl.pallas_call(
        body, name=name, out_shape=jax.ShapeDtypeStruct((N_DEV * m_per, n), v.dtype),
        in_specs=in_specs, out_specs=pl.BlockSpec(memory_space=pltpu.VMEM),
        scratch_shapes=[pltpu.SemaphoreType.DMA((7,)), pltpu.SemaphoreType.DMA((7,)), pltpu.SemaphoreType.DMA],
        compiler_params=pltpu.CompilerParams(vmem_limit_bytes=VMEM_LIMIT_BYTES),
    )(*args)


_HBM = pl.BlockSpec(memory_space=pltpu.HBM)
_SEM = pl.BlockSpec(memory_space=pltpu.SEMAPHORE)
_ANY = pl.BlockSpec(memory_space=pl.ANY)
_TOKEN = jax.ShapeDtypeStruct((8, 128), F32)


def _in_hbm(a):
    return pltpu.with_memory_space_constraint(a, pltpu.HBM)


def _place_cast(w, l, chip_arr, *, name):
    L, C = w.shape[0], w.shape[-1]
    R = w.size // (L * C)
    tr = _pick(R, max(8, ADAM_BLOCK_ELEMS // C), 16)

    def body(ch_ref, w_ref, o_ref):
        o_ref[...] = w_ref[...].astype(MXU_DTYPE)

    grid_spec = pltpu.PrefetchScalarGridSpec(
        num_scalar_prefetch=1, grid=(R // tr,),
        in_specs=[pl.BlockSpec((None, tr, C), lambda i, ch: (l, i, 0))],
        out_specs=pl.BlockSpec((None, tr, C), lambda i, ch: (ch[0], i, 0)))
    return pl.pallas_call(
        body, name=name, grid_spec=grid_spec, out_shape=jax.ShapeDtypeStruct((N_CHIPS, R, C), MXU_DTYPE),
        compiler_params=_params(("arbitrary",)),
    )(chip_arr, w.reshape(L, R, C))


def _chip_copies(srcs, lands, sends, recvs, k, j, px, py):
    x, y, c, _ = _place()
    me, peer = 2 * x + y, 2 * px + py
    src = srcs[k].at[peer] if srcs else lands[k].at[me]
    return pltpu.make_async_remote_copy(
        src_ref=src, dst_ref=lands[k].at[me], send_sem=sends[k].at[j], recv_sem=recvs[k].at[j],
        device_id=(px, py, c), device_id_type=MESH), peer


def _exchange_start(srcs, lands, after, *, name):
    ns, n = len(srcs), len(lands)
    arrays = (*srcs, *lands)

    def body(*refs):
        outs = refs[-(2 * n + ns + n + 1):]
        chips = _place()[3]
        for k in range(n):
            for j, (px, py) in enumerate(chips):
                _chip_copies(refs[:ns], refs[ns:ns + n], outs[:n], outs[n:2 * n], k, j, px, py)[0].start()
        outs[-1][...] = jnp.zeros_like(outs[-1])

    sem = pltpu.SemaphoreType.DMA((N_CHIPS - 1,))
    args = [_in_hbm(a) for a in arrays]
    in_specs = [_HBM] * (ns + n)
    if after is not None:
        args.append(after)
        in_specs.append(_ANY)
    outs = pl.pallas_call(
        body, name=name,
        out_shape=(*[sem] * (2 * n), *[pltpu.HBM(a.shape, a.dtype) for a in arrays], _TOKEN),
        in_specs=in_specs, out_specs=(*[_SEM] * (2 * n), *[_HBM] * (ns + n), pl.BlockSpec(memory_space=pltpu.VMEM)),
        input_output_aliases={k: 2 * n + k for k in range(ns + n)},
        compiler_params=pltpu.CompilerParams(has_side_effects=pltpu.SideEffectType.DATAFLOW_SIDE_EFFECTING),
    )(*args)
    return outs[:n], outs[n:2 * n], outs[2 * n:2 * n + ns], outs[2 * n + ns:2 * n + ns + n], outs[-1]


def _exchange_wait(sends, recvs, srcs, lands, after, *, name):
    ns, n = len(srcs), len(lands)

    def body(*refs):
        srcs_r, lands_r = refs[:ns], refs[ns:ns + n]
        sends_r, recvs_r = refs[ns + n:ns + 2 * n], refs[ns + 2 * n:ns + 3 * n]
        chips = _place()[3]
        for k in range(n):
            for j, (px, py) in enumerate(chips):
                cp, peer = _chip_copies(srcs_r, lands_r, sends_r, recvs_r, k, j, px, py)
                cp.wait_send()
                pltpu.make_async_remote_copy(
                    src_ref=lands_r[k].at[peer], dst_ref=lands_r[k].at[peer], send_sem=sends_r[k].at[j],
                    recv_sem=recvs_r[k].at[j], device_id=(px, py, _place()[2]), device_id_type=MESH).wait_recv()

    outs = pl.pallas_call(
        body, name=name, out_shape=[pltpu.HBM(a.shape, a.dtype) for a in (*srcs, *lands)],
        in_specs=[*[_HBM] * (ns + n), *[_SEM] * (2 * n), _ANY], out_specs=[_HBM] * (ns + n),
        input_output_aliases={k: k for k in range(ns + n)},
        compiler_params=pltpu.CompilerParams(has_side_effects=pltpu.SideEffectType.DATAFLOW_SIDE_EFFECTING),
    )(*srcs, *lands, *sends, *recvs, after)
    return outs[:ns], outs[ns:]


def _swap_sibling(ps, *, name):
    n = len(ps)

    def body(*refs):
        x, y, c, _ = _place()
        cps = [pltpu.make_async_remote_copy(src_ref=refs[k], dst_ref=refs[n + k], send_sem=refs[2 * n].at[k],
                                            recv_sem=refs[2 * n + 1].at[k], device_id=(x, y, 1 - c), device_id_type=MESH)
               for k in range(n)]
        for cp in cps:
            cp.start()
        for cp in cps:
            cp.wait()

    return pl.pallas_call(
        body, name=name, out_shape=[jax.ShapeDtypeStruct(p.shape, p.dtype) for p in ps],
        in_specs=[_ANY] * n, out_specs=[_ANY] * n,
        scratch_shapes=[pltpu.SemaphoreType.DMA((n,)), pltpu.SemaphoreType.DMA((n,))],
    )(*ps)


def kernel(x, c, ctx, c_ctx, w_ada, b_ada, ln_g, ln_b, ff1_in, ff1_out, ff2_in, ff2_out, w_in, conv4_w, conv4_b, w_rg, b_rg, w_ig, b_ig, lam, conv31_w, conv31_b, cln_g, cln_b, w_out, b_out, loss_target, m_c_ctx, m_w_ada, m_b_ada, m_ln_g, m_ln_b, m_ff1_in, m_ff1_out, m_ff2_in, m_ff2_out, m_w_in, m_conv4_w, m_conv4_b, m_w_rg, m_b_rg, m_w_ig, m_b_ig, m_lam, m_conv31_w, m_conv31_b, m_cln_g, m_cln_b, m_w_out, m_b_out, v_c_ctx, v_w_ada, v_b_ada, v_ln_g, v_ln_b, v_ff1_in, v_ff1_out, v_ff2_in, v_ff2_out, v_w_in, v_conv4_w, v_conv4_b, v_w_rg, v_b_rg, v_w_ig, v_b_ig, v_lam, v_conv31_w, v_conv31_b, v_cln_g, v_cln_b, v_w_out, v_b_out):
    weights = dict(c_ctx=c_ctx, w_ada=w_ada, b_ada=b_ada, ln_g=ln_g, ln_b=ln_b, ff1_in=ff1_in, ff1_out=ff1_out,
                   ff2_in=ff2_in, ff2_out=ff2_out, w_in=w_in, conv4_w=conv4_w, conv4_b=conv4_b, w_rg=w_rg, b_rg=b_rg,
                   w_ig=w_ig, b_ig=b_ig, lam=lam, conv31_w=conv31_w, conv31_b=conv31_b, cln_g=cln_g, cln_b=cln_b,
                   w_out=w_out, b_out=b_out)
    m_in = dict(c_ctx=m_c_ctx, w_ada=m_w_ada, b_ada=m_b_ada, ln_g=m_ln_g, ln_b=m_ln_b, ff1_in=m_ff1_in, ff1_out=m_ff1_out,
                ff2_in=m_ff2_in, ff2_out=m_ff2_out, w_in=m_w_in, conv4_w=m_conv4_w, conv4_b=m_conv4_b, w_rg=m_w_rg,
                b_rg=m_b_rg, w_ig=m_w_ig, b_ig=m_b_ig, lam=m_lam, conv31_w=m_conv31_w, conv31_b=m_conv31_b,
                cln_g=m_cln_g, cln_b=m_cln_b, w_out=m_w_out, b_out=m_b_out)
    v_in = dict(c_ctx=v_c_ctx, w_ada=v_w_ada, b_ada=v_b_ada, ln_g=v_ln_g, ln_b=v_ln_b, ff1_in=v_ff1_in, ff1_out=v_ff1_out,
                ff2_in=v_ff2_in, ff2_out=v_ff2_out, w_in=v_w_in, conv4_w=v_conv4_w, conv4_b=v_conv4_b, w_rg=v_w_rg,
                b_rg=v_b_rg, w_ig=v_w_ig, b_ig=v_b_ig, lam=v_lam, conv31_w=v_conv31_w, conv31_b=v_conv31_b,
                cln_g=v_cln_g, cln_b=v_cln_b, w_out=v_w_out, b_out=v_b_out)
    order = list(weights)

    ax, ay, ac = lax.axis_index("x"), lax.axis_index("y"), lax.axis_index("c")
    chip = 2 * ax + ay
    dev = 4 * ax + 2 * ay + ac
    chip_arr = jnp.reshape(chip, (1,)).astype(jnp.int32)

    L, D, Na = w_ada.shape
    Tl, Tc = x.shape[1], ctx.shape[1]
    T = Tc + Tl
    tms = Tc
    C = conv4_w.shape[2] * N_CHIPS
    nH, hds, hd = w_rg.shape[2], w_rg.shape[3], w_rg.shape[4]
    K31 = conv31_w.shape[1]
    assert L == 2 and Tl % tms == 0 and tms % GRID_W == 0 and tms % 8 == 0 and tms & (tms - 1) == 0
    assert hds * N_CHIPS == hd and nH * hd == C and D == 2 * C
    alpha = (2 * L) ** 0.25
    n_mod = N_CHIPS * Na // D

    def shard_cols(full, width):
        return lax.dynamic_slice_in_dim(full, chip * width, width, axis=full.ndim - 1)

    c8 = jnp.zeros((8, D), F32).at[0].set(c[0]).at[1].set(c_ctx)
    c_all = _allgather_small(c8, name="gather_cond").reshape(N_DEV, 8, D)
    c16 = jnp.concatenate([c_all[:, 0], c_ctx[None], jnp.zeros((7, D), F32)], axis=0)
    s16, ds16 = _silu_rows(c16, name="silu_cond")
    mod_part = _ada_fwd(s16, w_ada, shard_cols(b_ada, Na), name="ada_fwd")
    mod_all = _allgather_small(mod_part.reshape(L * 16, Na), name="gather_mod").reshape(N_DEV, L, 16, Na)
    mod_full = jnp.transpose(mod_all[0::2], (1, 2, 0, 3)).reshape(L, 16, N_CHIPS * Na)
    mod_rows = jnp.stack([mod_full[:, 8], lax.dynamic_index_in_dim(mod_full, dev, axis=1, keepdims=False)], axis=1)
    mod = mod_rows.reshape(L, 2, n_mod, D)

    def mvec(l, k):
        return mod[l, :, k, :]

    def full_gate(g):
        return jnp.transpose(g, (1, 2, 0, 3, 4)).reshape(2, nH, hd, hd)

    small_sharded = ("ln_g", "ln_b", "conv4_w", "b_rg", "b_ig", "lam", "conv31_w")
    pieces = {n: weights[n].reshape(-1, weights[n].shape[-1]) for n in small_sharded}
    widths = {n: p.shape[1] for n, p in pieces.items()}
    rows_of = {n: p.shape[0] for n, p in pieces.items()}
    wcat = max(widths.values())
    cat = jnp.concatenate([jnp.pad(p, ((0, 0), (0, wcat - p.shape[1]))) for p in pieces.values()], axis=0)
    rpad = -cat.shape[0] % 8
    cat_all = _allgather_small(jnp.pad(cat, ((0, rpad), (0, 0))), name="gather_small").reshape(N_DEV, -1, wcat)
    full_small, r0 = {}, 0
    for n in small_sharded:
        blk = cat_all[0::2, r0:r0 + rows_of[n], :widths[n]]
        full_small[n] = jnp.transpose(blk, (1, 0, 2)).reshape(rows_of[n], N_CHIPS * widths[n])
        r0 += rows_of[n]
    ln_g_f = full_small["ln_g"].reshape(L, 3, 1, D)
    ln_b_f = full_small["ln_b"].reshape(L, 3, 1, D)
    conv4_w_f = full_small["conv4_w"].reshape(L, 4, C)
    b_rg_f = full_small["b_rg"].reshape(L, 2, 1, C)
    b_ig_f = full_small["b_ig"].reshape(L, 2, 1, C)
    lam_f = full_small["lam"].reshape(L, 2, 1, C)
    conv31_w_f = full_small["conv31_w"].reshape(L, K31, C)

    ones, zeros = jnp.ones((1, D), F32), jnp.zeros((1, D), F32)

    big_names = ("ff1_in", "ff1_out", "w_in", "w_rg", "w_ig", "w_out", "ff2_in", "ff2_out")
    inflight, tok = [], mod[0, 0, 0, :1] + cat_all[0, 0, :1]
    for l in range(L):
        lands = [_place_cast(weights[n], l, chip_arr, name=f"place_{n}_{l}") for n in big_names]
        sends, recvs, _, lands_t, tok = _exchange_start([], lands, tok, name=f"gather_start_{l}")
        inflight.append({n: (sends[k], recvs[k], lands_t[k]) for k, n in enumerate(big_names)})
    gw = [{} for _ in range(L)]

    def gathered(l, n, after):
        s, r, land = inflight[l][n]
        land = _exchange_wait([s], [r], [], [land], after, name=f"gather_wait_{n}_{l}")[1][0]
        gw[l][n] = land.reshape(N_CHIPS, *weights[n].shape[1:])
        return gw[l][n]

    s0 = jnp.concatenate([ctx[0], x[0]], axis=0)
    cur = (s0, ones, zeros)
    saved = []
    for l in range(L):
        sv = {"in": cur}
        w = gathered(l, "ff1_in", tok if l == 0 else cur[0])
        h1, gu1, act1 = _in_proj(*cur, mvec(l, 0), mvec(l, 1), w, tc=Tc, swiglu=True, name=f"ffn1_in_{l}")
        xh1, rs1, f1 = _out_proj_ln(act1, gathered(l, "ff1_out", act1), zeros, *cur, mvec(l, 2), 0.5, alpha, tc=Tc, name=f"ffn1_out_{l}")
        sv.update(h1=h1, gu1=gu1, act1=act1, xh1=xh1, rs1=rs1, f1=f1)
        cur1 = (xh1, ln_g_f[l, 0], ln_b_f[l, 0])
        h2, z = _in_proj(*cur1, mvec(l, 3), mvec(l, 4), gathered(l, "w_in", xh1), tc=Tc, swiglu=False, name=f"mix_in_{l}")
        xc = _conv4_fwd(z, conv4_w_f[l], conv4_b[l][None], tms=tms, name=f"conv4_{l}")
        wr_l, wi_l = full_gate(gathered(l, "w_rg", xc)), full_gate(gathered(l, "w_ig", xc))
        sv.update(wr=wr_l, wi=wi_l)
        rec = []
        for d in range(2):
            rec.append(_lru_fwd(xc, wr_l[d], wi_l[d], b_rg_f[l, d], b_ig_f[l, d], lam_f[l, d],
                                rev=bool(d), tms=tms, name=f"lru_{l}_{d}"))
        ymix, uc = _convmod_fwd(rec[0][0], rec[1][0], z, conv31_w_f[l], conv31_b[l][None], cln_g[l][None], cln_b[l][None],
                                tms=tms, name=f"convmod_{l}")
        xh2, rs2, f2 = _out_proj_ln(ymix, gathered(l, "w_out", ymix), b_out[l][None], *cur1, mvec(l, 5), 1.0, alpha, tc=Tc, name=f"mix_out_{l}")
        sv.update(h2=h2, z=z, xc=xc, rec=rec, ymix=ymix, uc=uc, xh2=xh2, rs2=rs2, f2=f2)
        cur2 = (xh2, ln_g_f[l, 1], ln_b_f[l, 1])
        h3, gu3, act3 = _in_proj(*cur2, mvec(l, 6), mvec(l, 7), gathered(l, "ff2_in", xh2), tc=Tc, swiglu=True, name=f"ffn2_in_{l}")
        xh3, rs3, f3 = _out_proj_ln(act3, gathered(l, "ff2_out", act3), zeros, *cur2, mvec(l, 8), 0.5, alpha, tc=Tc, name=f"ffn2_out_{l}")
        sv.update(h3=h3, gu3=gu3, act3=act3, xh3=xh3, rs3=rs3, f3=f3)
        cur = (xh3, ln_g_f[l, 2], ln_b_f[l, 2])
        saved.append(sv)

    ds, loss_blk = _loss_head(*cur, loss_target[0], tc=Tc, tms=tms, name="loss_head")

    reduced = {n: None for n in big_names}
    pending = []

    def start_group(l, names, gs):
        gs = [g.reshape(N_CHIPS, -1, g.shape[-1]) for g in gs]
        lands = [lax.empty(g.shape, F32) for g in gs]
        sends, recvs, srcs_t, lands_t, token = _exchange_start(gs, lands, None, name=f"grad_start_{names[0]}_{l}")
        pending.append((sends, recvs, srcs_t, lands_t, names, l))
        return token

    def finish_group(after):
        sends, recvs, srcs_t, lands_t, names, l = pending.pop(0)
        gs, lands = _exchange_wait(sends, recvs, srcs_t, lands_t, after, name=f"grad_wait_{names[0]}_{l}")
        for k, n in enumerate(names):
            reduced[n] = _sum_scattered(gs[k], lands[k], chip_arr, l, L, reduced[n], name=f"grad_sum_{n}_{l}")

    def finish_older(keep, after):
        while len(pending) > keep:
            finish_group(after)

    dmod = [[None] * n_mod for _ in range(L)]
    d_ln_g = [[None] * 3 for _ in range(L)]
    d_ln_b = [[None] * 3 for _ in range(L)]
    small = {n: [None] * L for n in ("conv4_w", "conv4_b", "conv31_w", "conv31_b", "cln_g", "cln_b", "b_out")}
    gate_w = {n: [[None, None] for _ in range(L)] for n in ("w_rg", "w_ig", "b_rg", "b_ig", "lam")}

    def ffn_bwd(ds, l, k, names, sv_in, sfx, after):
        sv = saved[l]
        dy, dres, d_ln_g[l][k], d_ln_b[l][k], dmod[l][3 * k + 2], _ = _ln_bwd(
            ds, sv["xh" + sfx], sv["rs" + sfx], ln_g_f[l, k], sv["f" + sfx], mvec(l, 3 * k + 2), 0.5, alpha, after,
            tc=Tc, name=f"ffn{sfx}_ln_bwd_{l}")
        dg = _nt(dy, gw[l][names[1]], sv["gu" + sfx], name=f"ffn{sfx}_dact_{l}")
        g_in = _wgrad(sv["h" + sfx], dg, cols_sharded=True, name=f"ffn{sfx}_wgrad_in_{l}")
        t_in = start_group(l, (names[0],), (g_in,))
        g_out = _wgrad(sv["act" + sfx], dy, cols_sharded=False, name=f"ffn{sfx}_wgrad_out_{l}", after=t_in)
        t_out = start_group(l, (names[1],), (g_out,))
        ds_new, dmod[l][3 * k + 1], dmod[l][3 * k] = _dx_modbwd(
            dg, gw[l][names[0]], dres, *sv_in, mvec(l, 3 * k + 1), tc=Tc, name=f"ffn{sfx}_dx_{l}", after=t_out)
        return ds_new, t_out

    def gate_slots(g):
        g = g.reshape(2, nH, N_CHIPS, hds, hd)
        return jnp.transpose(g, (2, 0, 1, 3, 4)).reshape(N_CHIPS, 2 * nH * hds, hd)

    token = None
    for l in reversed(range(L)):
        sv = saved[l]
        cur1 = (sv["xh1"], ln_g_f[l, 0], ln_b_f[l, 0])
        cur2 = (sv["xh2"], ln_g_f[l, 1], ln_b_f[l, 1])
        ds, token = ffn_bwd(ds, l, 2, ("ff2_in", "ff2_out"), cur2, "3", token)
        finish_older(2, ds)
        dy, dres, d_ln_g[l][1], d_ln_b[l][1], dmod[l][5], small["b_out"][l] = _ln_bwd(
            ds, sv["xh2"], sv["rs2"], ln_g_f[l, 1], sv["f2"], mvec(l, 5), 1.0, alpha, token, tc=Tc, name=f"mix_ln_bwd_{l}")
        dymix = _nt(dy, gw[l]["w_out"], None, name=f"mix_dy_{l}")
        g_w_out = _wgrad(sv["ymix"], dy, cols_sharded=False, name=f"mix_wgrad_out_{l}")
        dhs, dz, small["conv31_w"][l], small["conv31_b"][l], small["cln_g"][l], small["cln_b"][l] = _convmod_bwd(
            dymix, sv["rec"][0][0], sv["rec"][1][0], sv["z"], sv["uc"], conv31_w_f[l], cln_g[l][None], cln_b[l][None],
            tms=tms, name=f"convmod_bwd_{l}")
        dxc = []
        for d in range(2):
            hd_, rd_, id_ = sv["rec"][d]
            o = _lru_bwd(dhs, hd_, rd_, id_, sv["xc"], sv["wr"][d], sv["wi"][d], lam_f[l, d],
                         rev=bool(d), tms=tms, name=f"lru_bwd_{l}_{d}")
            dxc.append(o[0])
            for n, val in zip(("w_rg", "w_ig", "b_rg", "b_ig", "lam"), o[1:]):
                gate_w[n][l][d] = val
        dz, small["conv4_w"][l], small["conv4_b"][l] = _conv4_bwd(dxc[0], dxc[1], sv["z"], conv4_w_f[l], dz, tms=tms, name=f"conv4_bwd_{l}")
        ds, dmod[l][4], dmod[l][3] = _dx_modbwd(dz, gw[l]["w_in"], dres, *cur1, mvec(l, 4), tc=Tc, name=f"mix_dx_{l}")
        g_w_in = _wgrad(sv["h2"], dz, cols_sharded=True, name=f"mix_wgrad_in_{l}")
        token = start_group(l, ("w_out", "w_in", "w_rg", "w_ig"),
                            (g_w_out, g_w_in, gate_slots(jnp.stack(gate_w["w_rg"][l])), gate_slots(jnp.stack(gate_w["w_ig"][l]))))
        finish_older(1, ds)
        ds, token = ffn_bwd(ds, l, 0, ("ff1_in", "ff1_out"), sv["in"], "1", token)
        finish_older(2, ds)

    grad_x = ds[Tc:][None]

    delta, new_m, new_v, grads = {}, {}, {}, {}

    def finish_weights(names, tag):
        others = _swap_sibling([reduced[n] for n in names], name=f"grad_swap_{tag}")
        for n, other in zip(names, others):
            shp = weights[n].shape
            grads[n], delta[n], new_m[n], new_v[n] = _adamw(weights[n], reduced[n].reshape(shp), m_in[n], v_in[n],
                                                            other.reshape(shp), name=f"adamw_{n}")

    finish_weights(("ff2_in", "ff2_out", "w_in", "w_out", "w_rg", "w_ig"), "early")
    finish_older(0, new_v["ff2_in"])
    finish_weights(("ff1_in", "ff1_out"), "late")


    dmod_arr = jnp.stack([jnp.stack(dmod[l], axis=1) for l in range(L)])
    dm_ctx = dmod_arr[:, 0].reshape(L, n_mod * D)
    dm_lat = dmod_arr[:, 1].reshape(L, n_mod * D)
    summed = {
        "loss": loss_blk[0:1, 0:1],
        "dm_ctx": dm_ctx,
        "ln_g": jnp.stack([jnp.concatenate(d_ln_g[l], axis=0) for l in range(L)]),
        "ln_b": jnp.stack([jnp.concatenate(d_ln_b[l], axis=0) for l in range(L)]),
        "conv4_w": jnp.stack(small["conv4_w"]),
        "conv4_b": jnp.concatenate(small["conv4_b"], axis=0),
        "b_rg": jnp.stack([jnp.concatenate(gate_w["b_rg"][l], axis=0) for l in range(L)]),
        "b_ig": jnp.stack([jnp.concatenate(gate_w["b_ig"][l], axis=0) for l in range(L)]),
        "lam": jnp.stack([jnp.concatenate(gate_w["lam"][l], axis=0) for l in range(L)]),
        "conv31_w": jnp.stack(small["conv31_w"]),
        "conv31_b": jnp.concatenate(small["conv31_b"], axis=0),
        "cln_g": jnp.concatenate(small["cln_g"], axis=0),
        "cln_b": jnp.concatenate(small["cln_b"], axis=0),
        "b_out": jnp.concatenate(small["b_out"], axis=0),
        "dm_lat": dm_lat,
    }
    flat = jnp.concatenate([v.reshape(-1) for v in summed.values()])
    n_flat = flat.shape[0]
    n_rows = -(-n_flat // 128)
    n_rows += -n_rows % 8
    vec = jnp.pad(flat, (0, n_rows * 128 - n_flat)).reshape(n_rows, 128)
    vec_all = _allgather_small(vec, name="gather_small_grads", after=new_v["ff1_out"]).reshape(N_DEV, n_rows, 128)
    vec_sum = _sum_leading(vec_all, tuple(range(N_DEV)), name="sum_small_grads").reshape(-1)
    tot, off = {}, 0
    for n, v in summed.items():
        tot[n] = vec_sum[off:off + v.size].reshape(v.shape)
        if n == "dm_lat":
            dm_lat_all = vec_all.reshape(N_DEV, -1)[:, off:off + v.size].reshape(N_DEV, L, n_mod * D)
        off += v.size
    loss = tot["loss"].reshape(())

    dm16 = jnp.concatenate([jnp.transpose(dm_lat_all, (1, 0, 2)), tot["dm_ctx"][:, None], jnp.zeros((L, 7, n_mod * D), F32)], axis=1)
    g_w_ada, ds16_part = _ada_bwd(s16, shard_cols(dm16, Na), w_ada, name="ada_bwd")
    ds_all = _allgather_small(ds16_part[8:16], name="gather_dcond").reshape(N_DEV, 8, D)
    g_c_ctx = _sum_leading(ds_all[:, 0:1], (0, 2, 4, 6), name="sum_dcond", scale_by=ds16[8:9]).reshape(D)
    g_b_ada = _sum_leading(jnp.stack([tot["dm_lat"], tot["dm_ctx"]]), (0, 1), name="sum_b_ada")

    grads.update(c_ctx=g_c_ctx, w_ada=g_w_ada, b_ada=g_b_ada)
    for n in ("ln_g", "ln_b", "conv4_w", "b_rg", "b_ig", "lam", "conv31_w"):
        grads[n] = shard_cols(tot[n], weights[n].shape[-1])
    for n in ("conv4_b", "conv31_b", "cln_g", "cln_b", "b_out"):
        grads[n] = tot[n]
    for n in order:
        if n not in reduced:
            delta[n], new_m[n], new_v[n] = _adamw(weights[n], grads[n], m_in[n], v_in[n], name=f"adamw_{n}")
    return (loss, grad_x, *[grads[n] for n in order], *[delta[n] for n in order],
            *[new_m[n] for n in order], *[new_v[n] for n in order])
```

```python
import functools

import jax
import jax.numpy as jnp
from jax import lax
from jax.experimental import pallas as pl
from jax.experimental.pallas import tpu as pltpu

F32 = jnp.float32
MXU_DTYPE = jnp.bfloat16
GRID_W = 64
RG_C = 8.0
LN_EPS = 1e-6
ADAM_LR, ADAM_B1, ADAM_B2, ADAM_EPS, ADAM_WD, ADAM_STEP = 0.001, 0.9, 0.999, 1e-08, 0.01, 10
LANES = 128
N_CHIPS = 4
N_DEV = 8
VMEM_LIMIT_BYTES = 56 * 1024 * 1024
TM_IN, TN_IN = 768, 256
TM_OUT, TK_OUT = 384, 1408
TM_NT, TN_NT = 384, 1408
TM_DX, TK_DX = 384, 1408
TK_WG, TB_WG = 768, 1408
ADAM_BLOCK_ELEMS = 256 * 1024
MESH = pl.DeviceIdType.MESH


def _pick(total, target, mult=128):
    for d in range(min(total, target), 0, -1):
        if total % d == 0 and d % mult == 0:
            return d
    return total


def _params(sem=None):
    kw = dict(vmem_limit_bytes=VMEM_LIMIT_BYTES)
    if sem is not None:
        kw["dimension_semantics"] = sem
    return pltpu.CompilerParams(**kw)


def _sigmoid(x):
    return 1.0 / (1.0 + jnp.exp(-x))


def _gelu(x):
    k = 0.7978845608028654
    t = jnp.tanh(k * (x + 0.044715 * (x * x * x)))
    return 0.5 * x * (1.0 + t)


def _gelu_grad(x):
    k = 0.7978845608028654
    t = jnp.tanh(k * (x + 0.044715 * (x * x * x)))
    return 0.5 * (1.0 + t) + 0.5 * x * (1.0 - t * t) * (k * (1.0 + 3.0 * 0.044715 * x * x))


def _log1p(e):
    u = 1.0 + e
    return jnp.where(u == 1.0, e, jnp.log(u) * (e / jnp.where(u == 1.0, 1.0, u - 1.0)))


def _softplus(y):
    return jnp.maximum(y, 0.0) + _log1p(jnp.exp(-jnp.abs(y)))


def _one_minus_sq(log_a, a):
    x = 2.0 * log_a
    series = -x * (1.0 + x * (0.5 + x * (1.0 / 6.0 + x * (1.0 / 24.0))))
    return jnp.where(x > -0.01, series, 1.0 - a * a)


def _strips(width, sw):
    return [(j * sw, (j + 1) * sw) for j in range(width // sw)]


def _rows(i, tm):
    return i * tm + lax.broadcasted_iota(jnp.int32, (tm, 1), 0)


def _sel(isctx, ref):
    return jnp.where(isctx, ref[0:1, :], ref[1:2, :])


def _colsum(v):
    return jnp.sum(v, axis=0, keepdims=True)


def _in_proj(xhat, g_in, b_in, shift, scale, w, *, tc, swiglu, name):
    T, D = xhat.shape
    S, _, Ns = w.shape
    tm, tn = _pick(T, TM_IN, 8), _pick(Ns, TN_IN)
    nps = Ns // tn
    nI = T // tm
    nN = (S // 2 if swiglu else S) * nps

    def body(x_ref, g_ref, b_ref, sh_ref, sc_ref, w_ref, h_out, *rest):
        i, n = pl.program_id(0), pl.program_id(1)
        h_scr = rest[-1]

        @pl.when(n == 0)
        def _():
            s = x_ref[...] * g_ref[...] + b_ref[...]
            isctx = _rows(i, tm) < tc
            h = (s * (1.0 + _sel(isctx, sc_ref)) + _sel(isctx, sh_ref)).astype(MXU_DTYPE)
            h_scr[...] = h
            h_out[...] = h

        h = h_scr[...]
        if swiglu:
            gu_out, act_out = rest[0], rest[1]
            gt = jnp.dot(h, w_ref[0], preferred_element_type=F32)
            ut = jnp.dot(h, w_ref[1], preferred_element_type=F32)
            gu_out[0] = gt.astype(MXU_DTYPE)
            gu_out[1] = ut.astype(MXU_DTYPE)
            act_out[...] = ((gt * _sigmoid(gt)) * ut).astype(MXU_DTYPE)
        else:
            rest[0][...] = jnp.dot(h, w_ref[...], preferred_element_type=F32)

    vec = pl.BlockSpec((1, D), lambda i, n: (0, 0))
    vec2 = pl.BlockSpec((2, D), lambda i, n: (0, 0))
    in_specs = [pl.BlockSpec((tm, D), lambda i, n: (i, 0)), vec, vec, vec2, vec2]
    h_shape = jax.ShapeDtypeStruct((T, D), MXU_DTYPE)
    h_spec = pl.BlockSpec((tm, D), lambda i, n: (i, 0))
    if swiglu:
        F = (S // 2) * Ns
        wv = w.reshape(2, S // 2, *w.shape[1:])
        in_specs.append(pl.BlockSpec((2, None, D, tn), lambda i, n: (0, n // nps, 0, n % nps)))
        out_shape = (h_shape, jax.ShapeDtypeStruct((2, T, F), MXU_DTYPE), jax.ShapeDtypeStruct((T, F), MXU_DTYPE))
        out_specs = (h_spec, pl.BlockSpec((2, tm, tn), lambda i, n: (0, i, n)), pl.BlockSpec((tm, tn), lambda i, n: (i, n)))
    else:
        wv = w
        in_specs.append(pl.BlockSpec((None, D, tn), lambda i, n: (n // nps, 0, n % nps)))
        out_shape = (h_shape, jax.ShapeDtypeStruct((S, T, Ns), F32))
        out_specs = (h_spec, pl.BlockSpec((None, tm, tn), lambda i, n: (n // nps, i, n % nps)))
    return pl.pallas_call(
        body, name=name, grid=(nI, nN), in_specs=in_specs, out_specs=out_specs, out_shape=out_shape,
        scratch_shapes=[pltpu.VMEM((tm, D), MXU_DTYPE)], compiler_params=_params(("arbitrary", "arbitrary")),
    )(xhat, g_in, b_in, shift, scale, wv)


def _out_proj_ln(a, w, bias, xin, g_in, b_in, gvec, gscale, alpha, *, tc, name):
    T, K = a.shape
    S, Ks, D = w.shape
    tm, tk = _pick(T, TM_OUT, 8), _pick(Ks, TK_OUT)
    kps = Ks // tk
    nK = S * kps

    def body(a_ref, w_ref, bias_ref, xin_ref, gi_ref, bi_ref, gv_ref, xh_out, rstd_out, f_out, acc):
        i, k = pl.program_id(0), pl.program_id(1)

        @pl.when(k == 0)
        def _():
            acc[...] = jnp.zeros_like(acc)

        acc[...] += jnp.dot(a_ref[...], w_ref[...], preferred_element_type=F32)

        @pl.when(k == nK - 1)
        def _():
            f = acc[...] + bias_ref[...]
            f_out[...] = f.astype(MXU_DTYPE)
            s = xin_ref[...] * gi_ref[...] + bi_ref[...]
            gv = gscale * _sel(_rows(i, tm) < tc, gv_ref)
            r = alpha * s + gv * f
            mu = jnp.mean(r, axis=-1, keepdims=True)
            d = r - mu
            var = jnp.mean(d * d, axis=-1, keepdims=True)
            rstd = lax.rsqrt(var + LN_EPS)
            xh_out[...] = d * rstd
            rstd_out[...] = rstd

    vec = pl.BlockSpec((1, D), lambda i, k: (0, 0))
    row = pl.BlockSpec((tm, D), lambda i, k: (i, 0))
    return pl.pallas_call(
        body, name=name, grid=(T // tm, nK),
        in_specs=[pl.BlockSpec((tm, tk), lambda i, k: (i, k)),
                  pl.BlockSpec((None, tk, D), lambda i, k: (k // kps, k % kps, 0)),
                  vec, row, vec, vec, pl.BlockSpec((2, D), lambda i, k: (0, 0))],
        out_specs=(row, pl.BlockSpec((tm, 1), lambda i, k: (i, 0)), row),
        out_shape=(jax.ShapeDtypeStruct((T, D), F32), jax.ShapeDtypeStruct((T, 1), F32),
                   jax.ShapeDtypeStruct((T, D), MXU_DTYPE)),
        scratch_shapes=[pltpu.VMEM((tm, D), F32)], compiler_params=_params(("arbitrary", "arbitrary")),
    )(a, w, bias, xin, g_in, b_in, gvec)


def _ln_bwd(ds, xhat, rstd, g_ln, f, gvec, gscale, alpha, after, *, tc, name):
    T, D = ds.shape
    tm = _pick(T, TM_OUT, 8)

    def body(ds_ref, xh_ref, rs_ref, gl_ref, f_ref, gv_ref, *rest):
        dy_out, dres_out, dgl_out, dbl_out, dgv_out, dbias_out = rest[-6:]
        i = pl.program_id(0)

        @pl.when(i == 0)
        def _():
            dgl_out[...] = jnp.zeros_like(dgl_out)
            dbl_out[...] = jnp.zeros_like(dbl_out)
            dgv_out[...] = jnp.zeros_like(dgv_out)
            dbias_out[...] = jnp.zeros_like(dbias_out)

        dsv, xh = ds_ref[...], xh_ref[...]
        dgl_out[...] += _colsum(dsv * xh)
        dbl_out[...] += _colsum(dsv)
        dxh = dsv * gl_ref[...]
        m1 = jnp.mean(dxh, axis=-1, keepdims=True)
        m2 = jnp.mean(dxh * xh, axis=-1, keepdims=True)
        dr = rs_ref[...] * (dxh - m1 - xh * m2)
        dres_out[...] = alpha * dr
        isctx = _rows(i, tm) < tc
        dyv = (gscale * _sel(isctx, gv_ref)) * dr
        dy_out[...] = dyv.astype(MXU_DTYPE)
        dbias_out[...] += _colsum(dyv)
        p = gscale * (dr * f_ref[...].astype(F32))
        dgv_out[0:1, :] += _colsum(jnp.where(isctx, p, 0.0))
        dgv_out[1:2, :] += _colsum(jnp.where(isctx, 0.0, p))

    vec = pl.BlockSpec((1, D), lambda i: (0, 0))
    vec2 = pl.BlockSpec((2, D), lambda i: (0, 0))
    row = pl.BlockSpec((tm, D), lambda i: (i, 0))
    in_specs = [row, row, pl.BlockSpec((tm, 1), lambda i: (i, 0)), vec, row, vec2]
    args = [ds, xhat, rstd, g_ln, f, gvec]
    if after is not None:
        in_specs.append(pl.BlockSpec(memory_space=pl.ANY))
        args.append(after)
    return pl.pallas_call(
        body, name=name, grid=(T // tm,), in_specs=in_specs,
        out_specs=(row, row, vec, vec, vec2, vec),
        out_shape=(jax.ShapeDtypeStruct((T, D), MXU_DTYPE), jax.ShapeDtypeStruct((T, D), F32),
                   jax.ShapeDtypeStruct((1, D), F32), jax.ShapeDtypeStruct((1, D), F32),
                   jax.ShapeDtypeStruct((2, D), F32), jax.ShapeDtypeStruct((1, D), F32)),
        compiler_params=_params(("arbitrary",)),
    )(*args)


def _nt(dy, w, gate_up, *, name):
    T, D = dy.shape
    S, Ks, _ = w.shape
    tm, tn = _pick(T, TM_NT, 8), _pick(Ks, TN_NT)
    nps = Ks // tn
    nN = S * nps
    F = S * Ks

    def body(dy_ref, w_ref, *rest):
        d = lax.dot_general(dy_ref[...], w_ref[...], (((1,), (1,)), ((), ())), preferred_element_type=F32)
        if gate_up is None:
            rest[0][...] = d
        else:
            gu_ref, dg_out = rest
            g, u = gu_ref[0].astype(F32), gu_ref[1].astype(F32)
            sg = _sigmoid(g)
            dg_out[0] = (d * u * (sg * (1.0 + g * (1.0 - sg)))).astype(MXU_DTYPE)
            dg_out[1] = (d * (g * sg)).astype(MXU_DTYPE)

    in_specs = [pl.BlockSpec((tm, D), lambda i, n: (i, 0)),
                pl.BlockSpec((None, tn, D), lambda i, n: (n // nps, n % nps, 0))]
    args = [dy, w]
    if gate_up is None:
        out_shape = jax.ShapeDtypeStruct((T, F), F32)
        out_specs = pl.BlockSpec((tm, tn), lambda i, n: (i, n))
    else:
        Ns = 2 * F // S
        q = Ns // tn
        in_specs.append(pl.BlockSpec((2, tm, tn), lambda i, n: (0, i, n)))
        args.append(gate_up)
        out_shape = jax.ShapeDtypeStruct((2, S // 2, T, Ns), MXU_DTYPE)
        out_specs = pl.BlockSpec((2, None, tm, tn), lambda i, n: (0, n // q, i, n % q))
    out = pl.pallas_call(
        body, name=name, grid=(T // tm, nN), in_specs=in_specs, out_specs=out_specs, out_shape=out_shape,
        compiler_params=_params(("arbitrary", "arbitrary")),
    )(*args)
    return out if gate_up is None else out.reshape(S, T, out.shape[-1])


def _dx_modbwd(dg, w, dres, xhat_in, g_in, b_in, scale, *, tc, name, after=None):
    S, T, Ns = dg.shape
    D = w.shape[1]
    tm, tk = _pick(T, TM_DX, 8), _pick(Ns, TK_DX)
    kps = Ns // tk
    nK = S * kps

    def body(dg_ref, w_ref, dres_ref, xin_ref, gi_ref, bi_ref, sc_ref, *rest):
        ds_out, dsc_out, dsh_out, acc = rest[-4:]
        i, k = pl.program_id(0), pl.program_id(1)

        @pl.when((i == 0) & (k == 0))
        def _():
            dsc_out[...] = jnp.zeros_like(dsc_out)
            dsh_out[...] = jnp.zeros_like(dsh_out)

        @pl.when(k == 0)
        def _():
            acc[...] = jnp.zeros_like(acc)

        acc[...] += lax.dot_general(dg_ref[...], w_ref[...], (((1,), (1,)), ((), ())), preferred_element_type=F32)

        @pl.when(k == nK - 1)
        def _():
            dh = acc[...]
            isctx = _rows(i, tm) < tc
            ds_out[...] = dres_ref[...] + dh * (1.0 + _sel(isctx, sc_ref))
            s = xin_ref[...] * gi_ref[...] + bi_ref[...]
            p = dh * s
            dsc_out[0:1, :] += _colsum(jnp.where(isctx, p, 0.0))
            dsc_out[1:2, :] += _colsum(jnp.where(isctx, 0.0, p))
            dsh_out[0:1, :] += _colsum(jnp.where(isctx, dh, 0.0))
            dsh_out[1:2, :] += _colsum(jnp.where(isctx, 0.0, dh))

    vec = pl.BlockSpec((1, D), lambda i, k: (0, 0))
    vec2 = pl.BlockSpec((2, D), lambda i, k: (0, 0))
    row = pl.BlockSpec((tm, D), lambda i, k: (i, 0))
    in_specs = [pl.BlockSpec((None, tm, tk), lambda i, k: (k // kps, i, k % kps)),
                pl.BlockSpec((None, D, tk), lambda i, k: (k // kps, 0, k % kps)),
                row, row, vec, vec, vec2]
    args = [dg, w, dres, xhat_in, g_in, b_in, scale]
    if after is not None:
        in_specs.append(pl.BlockSpec(memory_space=pl.ANY))
        args.append(after)
    return pl.pallas_call(
        body, name=name, grid=(T // tm, nK), in_specs=in_specs,
        out_specs=(row, vec2, vec2),
        out_shape=(jax.ShapeDtypeStruct((T, D), F32), jax.ShapeDtypeStruct((2, D), F32), jax.ShapeDtypeStruct((2, D), F32)),
        scratch_shapes=[pltpu.VMEM((tm, D), F32)], compiler_params=_params(("arbitrary", "arbitrary")),
    )(*args)


def _wgrad(a, b, *, cols_sharded, name, after=None):
    T = a.shape[0]
    tk = _pick(T, TK_WG, 8)
    if cols_sharded:
        S, _, Ns = b.shape
        D = a.shape[1]
        tb = _pick(Ns, TB_WG)
        grid = (S, Ns // tb, T // tk)
        in_specs = [pl.BlockSpec((tk, D), lambda s, j, k: (k, 0)), pl.BlockSpec((None, tk, tb), lambda s, j, k: (s, k, j))]
        out_shape = jax.ShapeDtypeStruct((S, D, Ns), F32)
        out_specs = pl.BlockSpec((None, D, tb), lambda s, j, k: (s, 0, j))
    else:
        D = b.shape[1]
        S = N_CHIPS
        Ks = a.shape[1] // S
        ta = _pick(Ks, TB_WG)
        q = Ks // ta
        grid = (S, q, T // tk)
        in_specs = [pl.BlockSpec((tk, ta), lambda s, j, k: (k, s * q + j)), pl.BlockSpec((tk, D), lambda s, j, k: (k, 0))]
        out_shape = jax.ShapeDtypeStruct((S, Ks, D), F32)
        out_specs = pl.BlockSpec((None, ta, D), lambda s, j, k: (s, j, 0))

    def body(a_ref, b_ref, *rest):
        o_ref = rest[-1]

        @pl.when(pl.program_id(2) == 0)
        def _():
            o_ref[...] = jnp.zeros_like(o_ref)

        o_ref[...] += lax.dot_general(a_ref[...], b_ref[...], (((0,), (0,)), ((), ())), preferred_element_type=F32)

    args = [a, b]
    if after is not None:
        in_specs.append(pl.BlockSpec(memory_space=pl.ANY))
        args.append(after)
    return pl.pallas_call(
        body, name=name, grid=grid, in_specs=in_specs, out_specs=out_specs, out_shape=out_shape,
        compiler_params=_params(("arbitrary", "arbitrary", "arbitrary")),
    )(*args)


def _halo_specs(tms, T, C, slot=None):
    r8 = tms // 8
    last8 = T // 8 - 1
    if slot is None:
        return (pl.BlockSpec((8, C), lambda i: (jnp.maximum(i * r8 - 1, 0), 0)),
                pl.BlockSpec((tms, C), lambda i: (i, 0)),
                pl.BlockSpec((8, C), lambda i: (jnp.minimum((i + 1) * r8, last8), 0)))
    return (pl.BlockSpec((None, 8, C), lambda i: (slot, jnp.maximum(i * r8 - 1, 0), 0)),
            pl.BlockSpec((None, tms, C), lambda i: (slot, i, 0)),
            pl.BlockSpec((None, 8, C), lambda i: (slot, jnp.minimum((i + 1) * r8, last8), 0)))


def _extended(prev_ref, cur, next_ref, i, n_tiles):
    first = (i == 0) | (i == 1)
    last = (i == 0) | (i == n_tiles - 1)
    pv = jnp.where(first, 0.0, prev_ref[...])
    nx = jnp.where(last, 0.0, next_ref[...])
    return jnp.concatenate([pv, cur, nx], axis=0)


def _shifted(ext, o, tms):
    n = tms + 16
    return pltpu.roll(ext, (-o) % n, 0)[8:8 + tms]


def _conv4_fwd(z, w4, b4, *, tms, name):
    S, T, C = z.shape
    nT = T // tms

    def body(p_ref, c_ref, n_ref, w_ref, b_ref, o_ref):
        i = pl.program_id(0)
        ext = _extended(p_ref, c_ref[...], n_ref, i, nT)
        acc = jnp.zeros((tms, C), F32) + b_ref[...]
        for k in range(4):
            acc = acc + w_ref[k:k + 1, :] * _shifted(ext, k - 2, tms)
        o_ref[...] = acc

    return pl.pallas_call(
        body, name=name, grid=(nT,),
        in_specs=[*_halo_specs(tms, T, C, 0), pl.BlockSpec((4, C), lambda i: (0, 0)), pl.BlockSpec((1, C), lambda i: (0, 0))],
        out_specs=pl.BlockSpec((tms, C), lambda i: (i, 0)), out_shape=jax.ShapeDtypeStruct((T, C), F32),
        compiler_params=_params(("arbitrary",)),
    )(z, z, z, w4, b4)


def _conv4_bwd(dxa, dxb, z, w4, dz, *, tms, name):
    S, T, C = z.shape
    nT = T // tms

    def body(pa, ca, na, pb, cb, nb, px, cx, nx, w_ref, dz_in, dz_out, dw_out, db_out):
        i = pl.program_id(0)

        @pl.when(i == 0)
        def _():
            dw_out[...] = jnp.zeros_like(dw_out)
            db_out[...] = jnp.zeros_like(db_out)

        dcur = ca[...] + cb[...]
        first = (i == 0) | (i == 1)
        last = (i == 0) | (i == nT - 1)
        dext = jnp.concatenate([jnp.where(first, 0.0, pa[...] + pb[...]), dcur, jnp.where(last, 0.0, na[...] + nb[...])], axis=0)
        xext = _extended(px, cx[...], nx, i, nT)
        acc = jnp.zeros((tms, C), F32)
        for k in range(4):
            acc = acc + w_ref[k:k + 1, :] * _shifted(dext, 2 - k, tms)
            dw_out[k:k + 1, :] += _colsum(dcur * _shifted(xext, k - 2, tms))
        db_out[...] += _colsum(dcur)
        dz_out[...] = acc.astype(MXU_DTYPE)

    h = _halo_specs(tms, T, C)
    return pl.pallas_call(
        body, name=name, grid=(nT,),
        in_specs=[*h, *h, *_halo_specs(tms, T, C, 0), pl.BlockSpec((4, C), lambda i: (0, 0)), pl.BlockSpec(memory_space=pl.ANY)],
        out_specs=(pl.BlockSpec((None, tms, C), lambda i: (0, i, 0)), pl.BlockSpec((4, C), lambda i: (0, 0)),
                   pl.BlockSpec((1, C), lambda i: (0, 0))),
        out_shape=(jax.ShapeDtypeStruct(dz.shape, dz.dtype), jax.ShapeDtypeStruct((4, C), F32), jax.ShapeDtypeStruct((1, C), F32)),
        input_output_aliases={10: 0}, compiler_params=_params(("arbitrary",)),
    )(dxa, dxa, dxa, dxb, dxb, dxb, z, z, z, w4, dz)


def _scan_rows(a, b, h_in, rev, n):
    sub = lax.broadcasted_iota(jnp.int32, (n, 1), 0) & 7
    for sft in (1, 2, 4):
        if rev:
            a_sh, b_sh, valid = pltpu.roll(a, n - sft, 0), pltpu.roll(b, n - sft, 0), sub < 8 - sft
        else:
            a_sh, b_sh, valid = pltpu.roll(a, sft, 0), pltpu.roll(b, sft, 0), sub >= sft
        b = a * jnp.where(valid, b_sh, 0.0) + b
        a = a * jnp.where(valid, a_sh, 1.0)
    out, c = [None] * (n // 8), h_in
    for g in (reversed(range(n // 8)) if rev else range(n // 8)):
        out[g] = a[8 * g:8 * g + 8] * c + b[8 * g:8 * g + 8]
        c = out[g][0:1] if rev else out[g][7:8]
    return jnp.concatenate(out, axis=0)


def _scan_order(i, rev, nT):
    if not rev:
        return i
    return jnp.where(i == 0, 0, nT - i)


def _lru_fwd(xc, wr, wi, br, bi, lam, *, rev, tms, name):
    T, C = xc.shape
    nH, hd, _ = wr.shape
    nT = T // tms
    sw = min(hd, LANES)

    def body(xc_ref, wr_ref, wi_ref, br_ref, bi_ref, lam_ref, h_out, r_out, i_out, carry):
        @pl.when(pl.program_id(0) == 0)
        def _():
            carry[...] = jnp.zeros_like(carry)

        for hx in range(nH):
            xh = xc_ref[:, hx * hd:(hx + 1) * hd]
            xb = xh.astype(MXU_DTYPE)
            pre_r = jnp.dot(xb, wr_ref[hx], preferred_element_type=F32)
            pre_i = jnp.dot(xb, wi_ref[hx], preferred_element_type=F32)
            for a0, a1 in _strips(hd, sw):
                sl = slice(hx * hd + a0, hx * hd + a1)
                x = xh[:, a0:a1]
                r = _sigmoid(pre_r[:, a0:a1] + br_ref[:, sl])
                ig = _sigmoid(pre_i[:, a0:a1] + bi_ref[:, sl])
                log_a = (-RG_C * r) * _softplus(-lam_ref[:, sl])
                a = jnp.exp(log_a)
                b = jnp.sqrt(_one_minus_sq(log_a, a)) * (ig * x)
                h = _scan_rows(a, b, carry[:, sl], rev, tms)
                carry[:, sl] = h[0:1, :] if rev else h[tms - 1:tms, :]
                h_out[:, sl] = h
                r_out[:, sl] = r
                i_out[:, sl] = ig

    tile = pl.BlockSpec((tms, C), lambda i: (_scan_order(i, rev, nT), 0))
    wspec = pl.BlockSpec((nH, hd, hd), lambda i: (0, 0, 0))
    vec = pl.BlockSpec((1, C), lambda i: (0, 0))
    shp = jax.ShapeDtypeStruct((T, C), F32)
    return pl.pallas_call(
        body, name=name, grid=(nT,), in_specs=[tile, wspec, wspec, vec, vec, vec], out_specs=(tile, tile, tile),
        out_shape=(shp, shp, shp), scratch_shapes=[pltpu.VMEM((1, C), F32)], compiler_params=_params(("arbitrary",)),
    )(xc, wr, wi, br, bi, lam)


def _lru_bwd(dh, h, r, ig, xc, wr, wi, lam, *, rev, tms, name):
    T, C = xc.shape
    nH, hd, _ = wr.shape
    nT = T // tms
    r8 = tms // 8
    sw = min(hd, LANES)

    def tile_of(ip):
        return _scan_order(nT - 1 - ip, rev, nT)

    def halo_of(ip):
        i = nT - 1 - ip
        if not rev:
            return jnp.maximum(i * r8 - 1, 0)
        return jnp.where(i <= 1, 0, (nT - i + 1) * r8)

    def body(dh_ref, h_ref, hh_ref, r_ref, i_ref, xc_ref, wr_ref, wi_ref, lam_ref,
             dxc_out, dwr_out, dwi_out, dbr_out, dbi_out, dlam_out, ucarry):
        ip = pl.program_id(0)
        pos = nT - 1 - ip

        @pl.when(ip == 0)
        def _():
            ucarry[...] = jnp.zeros_like(ucarry)
            for o in (dwr_out, dwi_out, dbr_out, dbi_out, dlam_out):
                o[...] = jnp.zeros_like(o)

        rows = lax.broadcasted_iota(jnp.int32, (tms, 1), 0)
        nt = (((1,), (1,)), ((), ()))
        tn = (((0,), (0,)), ((), ()))
        for hx in range(nH):
            dpr_parts, dpi_parts, direct_parts = [], [], []
            for a0, a1 in _strips(hd, sw):
                sl = slice(hx * hd + a0, hx * hd + a1)
                x, rr, ii, hh, dhv = xc_ref[:, sl], r_ref[:, sl], i_ref[:, sl], h_ref[:, sl], dh_ref[:, sl]
                sp = _softplus(-lam_ref[:, sl])
                log_a = (-RG_C * rr) * sp
                a = jnp.exp(log_a)
                s = jnp.sqrt(_one_minus_sq(log_a, a))
                u_in = ucarry[:, sl]
                u = _scan_rows(a, a * dhv, u_in, not rev, tms)
                if rev:
                    u_next = jnp.where(rows == 0, u_in, pltpu.roll(u, 1, 0))
                    ucarry[:, sl] = u[tms - 1:tms, :]
                    h_halo = jnp.where(pos == 0, 0.0, hh_ref[0:1, sl])
                    h_prev = jnp.where(rows == tms - 1, h_halo, pltpu.roll(hh, tms - 1, 0))
                else:
                    u_next = jnp.where(rows == tms - 1, u_in, pltpu.roll(u, tms - 1, 0))
                    ucarry[:, sl] = u[0:1, :]
                    h_halo = jnp.where(pos == 0, 0.0, hh_ref[7:8, sl])
                    h_prev = jnp.where(rows == 0, h_halo, pltpu.roll(hh, 1, 0))
                g = dhv + u_next
                dgated = g * s
                dlog_a = (g * h_prev) * a - (g * (ii * x)) * (a * a) / s
                dlam_out[:, sl] += _colsum(dlog_a * (-RG_C * rr)) * (-_sigmoid(-lam_ref[:, sl]))
                dpr = (dlog_a * (-RG_C * sp)) * rr * (1.0 - rr)
                dpi = (dgated * x) * ii * (1.0 - ii)
                dbr_out[:, sl] += _colsum(dpr)
                dbi_out[:, sl] += _colsum(dpi)
                dpr_parts.append(dpr.astype(MXU_DTYPE))
                dpi_parts.append(dpi.astype(MXU_DTYPE))
                direct_parts.append(dgated * ii)
            cat = (lambda p: p[0] if len(p) == 1 else jnp.concatenate(p, axis=1))
            dprh, dpih = cat(dpr_parts), cat(dpi_parts)
            xh = xc_ref[:, hx * hd:(hx + 1) * hd].astype(MXU_DTYPE)
            dxc_out[:, hx * hd:(hx + 1) * hd] = (cat(direct_parts)
                                                + lax.dot_general(dprh, wr_ref[hx], nt, preferred_element_type=F32)
                                                + lax.dot_general(dpih, wi_ref[hx], nt, preferred_element_type=F32))
            dwr_out[hx] += lax.dot_general(xh, dprh, tn, preferred_element_type=F32)
            dwi_out[hx] += lax.dot_general(xh, dpih, tn, preferred_element_type=F32)

    tile = pl.BlockSpec((tms, C), lambda ip: (tile_of(ip), 0))
    wspec = pl.BlockSpec((nH, hd, hd), lambda ip: (0, 0, 0))
    vec = pl.BlockSpec((1, C), lambda ip: (0, 0))
    wshape = jax.ShapeDtypeStruct((nH, hd, hd), F32)
    vshape = jax.ShapeDtypeStruct((1, C), F32)
    return pl.pallas_call(
        body, name=name, grid=(nT,),
        in_specs=[tile, tile, pl.BlockSpec((8, C), lambda ip: (halo_of(ip), 0)), tile, tile, tile, wspec, wspec, vec],
        out_specs=(tile, wspec, wspec, vec, vec, vec),
        out_shape=(jax.ShapeDtypeStruct((T, C), F32), wshape, wshape, vshape, vshape, vshape),
        scratch_shapes=[pltpu.VMEM((1, C), F32)], compiler_params=_params(("arbitrary",)),
    )(dh, h, h, r, ig, xc, wr, wi, lam)


CONV_PAD = 16


def _shift_bank(u):
    pad = jnp.zeros((CONV_PAD, u.shape[1]), u.dtype)
    ext = jnp.concatenate([pad, u, pad], axis=0)
    return [ext] + [pltpu.roll(ext, r, 0) for r in range(1, 8)]


def _shifted_rows(bank, o, rows):
    a = -((-o) // 8)
    start = CONV_PAD + 8 * a
    return bank[8 * a - o][start:start + rows]


def _convmod_fwd(hf, hb, z, w31, b31, clg, clb, *, tms, name):
    S, T, C = z.shape
    K = w31.shape[0]
    nT = T // tms

    sw = min(C, LANES)
    G = GRID_W

    def body(hf_ref, hb_ref, gr_ref, cv_ref, cg_ref, w_ref, b_ref, g_ref, bb_ref, y_out, uc_out):
        i = pl.program_id(0)
        for a0, a1 in _strips(C, sw):
            y_out[:, a0:a1] = ((hf_ref[:, a0:a1] + hb_ref[:, a0:a1]) * _gelu(gr_ref[:, a0:a1])).astype(MXU_DTYPE)

        def conv(seg):
            for r0 in range(0, tms, seg):
                rs = slice(r0, r0 + seg)
                s1 = jnp.zeros((seg, 1), F32)
                for a0, a1 in _strips(C, sw):
                    bank = _shift_bank(cv_ref[rs, a0:a1] * _sigmoid(cg_ref[rs, a0:a1]))
                    acc = jnp.zeros((seg, sw), F32) + b_ref[:, a0:a1]
                    for k in range(K):
                        acc = acc + w_ref[k:k + 1, a0:a1] * _shifted_rows(bank, k - K // 2, seg)
                    uc_out[rs, a0:a1] = acc
                    s1 = s1 + jnp.sum(acc, axis=-1, keepdims=True)
                mu = s1 / C
                s2 = jnp.zeros((seg, 1), F32)
                for a0, a1 in _strips(C, sw):
                    d = uc_out[rs, a0:a1] - mu
                    s2 = s2 + jnp.sum(d * d, axis=-1, keepdims=True)
                rstd = lax.rsqrt(s2 / C + LN_EPS)
                for a0, a1 in _strips(C, sw):
                    yl = (uc_out[rs, a0:a1] - mu) * rstd * g_ref[:, a0:a1] + bb_ref[:, a0:a1]
                    y_out[rs, C + a0:C + a1] = (yl * _sigmoid(yl)).astype(MXU_DTYPE)

        @pl.when(i == 0)
        def _():
            conv(tms)

        @pl.when(i != 0)
        def _():
            conv(G)

    tile = pl.BlockSpec((tms, C), lambda i: (i, 0))
    vec = pl.BlockSpec((1, C), lambda i: (0, 0))
    zs = [pl.BlockSpec((None, tms, C), functools.partial(lambda i, s: (s, i, 0), s=s)) for s in (1, 2, 3)]
    return pl.pallas_call(
        body, name=name, grid=(nT,),
        in_specs=[tile, tile, *zs, pl.BlockSpec((K, C), lambda i: (0, 0)), vec, vec, vec],
        out_specs=(pl.BlockSpec((tms, 2 * C), lambda i: (i, 0)), tile),
        out_shape=(jax.ShapeDtypeStruct((T, 2 * C), MXU_DTYPE), jax.ShapeDtypeStruct((T, C), F32)),
        compiler_params=_params(("arbitrary",)),
    )(hf, hb, z, z, z, w31, b31, clg, clb)


def _convmod_bwd(dymix, hf, hb, z, uc, w31, clg, clb, *, tms, name):
    S, T, C = z.shape
    K = w31.shape[0]
    nT = T // tms

    sw = min(C, LANES)

    def body(dy_ref, hf_ref, hb_ref, gr_ref, cv_ref, cg_ref, uc_ref, w_ref, g_ref, bb_ref,
             dhs_out, dz_out, dw_out, db_out, dg_out, dbb_out, dxh_buf):
        i = pl.program_id(0)

        @pl.when(i == 0)
        def _():
            for o in (dw_out, db_out, dg_out, dbb_out):
                o[...] = jnp.zeros_like(o)

        for a0, a1 in _strips(C, sw):
            dyr, gr = dy_ref[:, a0:a1], gr_ref[:, a0:a1]
            dhs_out[:, a0:a1] = dyr * _gelu(gr)
            dz_out[0, :, a0:a1] = jnp.zeros((tms, sw), MXU_DTYPE)
            dz_out[1, :, a0:a1] = (dyr * (hf_ref[:, a0:a1] + hb_ref[:, a0:a1]) * _gelu_grad(gr)).astype(MXU_DTYPE)

        def conv(seg):
            for r0 in range(0, tms, seg):
                rs = slice(r0, r0 + seg)
                s1 = jnp.zeros((seg, 1), F32)
                for a0, a1 in _strips(C, sw):
                    s1 = s1 + jnp.sum(uc_ref[rs, a0:a1], axis=-1, keepdims=True)
                mu = s1 / C
                s2 = jnp.zeros((seg, 1), F32)
                for a0, a1 in _strips(C, sw):
                    d = uc_ref[rs, a0:a1] - mu
                    s2 = s2 + jnp.sum(d * d, axis=-1, keepdims=True)
                rstd = lax.rsqrt(s2 / C + LN_EPS)
                m1 = jnp.zeros((seg, 1), F32)
                m2 = jnp.zeros((seg, 1), F32)
                for a0, a1 in _strips(C, sw):
                    xh = (uc_ref[rs, a0:a1] - mu) * rstd
                    yl = xh * g_ref[:, a0:a1] + bb_ref[:, a0:a1]
                    sg = _sigmoid(yl)
                    dyl = dy_ref[rs, C + a0:C + a1] * (sg * (1.0 + yl * (1.0 - sg)))
                    dg_out[:, a0:a1] += _colsum(dyl * xh)
                    dbb_out[:, a0:a1] += _colsum(dyl)
                    dxh = dyl * g_ref[:, a0:a1]
                    dxh_buf[rs, a0:a1] = dxh
                    m1 = m1 + jnp.sum(dxh, axis=-1, keepdims=True)
                    m2 = m2 + jnp.sum(dxh * xh, axis=-1, keepdims=True)
                m1, m2 = m1 / C, m2 / C
                for a0, a1 in _strips(C, sw):
                    xh = (uc_ref[rs, a0:a1] - mu) * rstd
                    duc = rstd * (dxh_buf[rs, a0:a1] - m1 - xh * m2)
                    db_out[:, a0:a1] += _colsum(duc)
                    cv, sc = cv_ref[rs, a0:a1], _sigmoid(cg_ref[rs, a0:a1])
                    bank_u, bank_d = _shift_bank(cv * sc), _shift_bank(duc)
                    du = jnp.zeros((seg, sw), F32)
                    for k in range(K):
                        o = k - K // 2
                        du = du + w_ref[k:k + 1, a0:a1] * _shifted_rows(bank_d, -o, seg)
                        dw_out[k:k + 1, a0:a1] += _colsum(duc * _shifted_rows(bank_u, o, seg))
                    dz_out[2, rs, a0:a1] = (du * sc).astype(MXU_DTYPE)
                    dz_out[3, rs, a0:a1] = (du * cv * sc * (1.0 - sc)).astype(MXU_DTYPE)

        @pl.when(i == 0)
        def _():
            conv(tms)

        @pl.when(i != 0)
        def _():
            conv(GRID_W)

    tile = pl.BlockSpec((tms, C), lambda i: (i, 0))
    vec = pl.BlockSpec((1, C), lambda i: (0, 0))
    kc = pl.BlockSpec((K, C), lambda i: (0, 0))
    zs = [pl.BlockSpec((None, tms, C), functools.partial(lambda i, s: (s, i, 0), s=s)) for s in (1, 2, 3)]
    vshape = jax.ShapeDtypeStruct((1, C), F32)
    return pl.pallas_call(
        body, name=name, grid=(nT,),
        in_specs=[pl.BlockSpec((tms, 2 * C), lambda i: (i, 0)), tile, tile, *zs, tile, kc, vec, vec],
        out_specs=(tile, pl.BlockSpec((S, tms, C), lambda i: (0, i, 0)), kc, vec, vec, vec),
        out_shape=(jax.ShapeDtypeStruct((T, C), F32), jax.ShapeDtypeStruct((S, T, C), MXU_DTYPE),
                   jax.ShapeDtypeStruct((K, C), F32), vshape, vshape, vshape),
        scratch_shapes=[pltpu.VMEM((tms, C), F32)], compiler_params=_params(("arbitrary",)),
    )(dymix, hf, hb, z, z, z, uc, w31, clg, clb)


def _loss_head(xhat, g, b, target, *, tc, tms, name):
    T, D = xhat.shape
    nT = T // tms
    nc = tc // tms

    def body(x_ref, g_ref, b_ref, t_ref, dy_out, loss_out):
        i = pl.program_id(0)

        @pl.when(i == 0)
        def _():
            loss_out[...] = jnp.zeros_like(loss_out)

        err = jnp.where(i < nc, 0.0, x_ref[...] * g_ref[...] + b_ref[...] - t_ref[...])
        dy_out[...] = err / D
        loss_out[...] += 0.5 * jnp.sum(jnp.sum(err * err, axis=-1, keepdims=True) / D)

    vec = pl.BlockSpec((1, D), lambda i: (0, 0))
    tile = pl.BlockSpec((tms, D), lambda i: (i, 0))
    return pl.pallas_call(
        body, name=name, grid=(nT,),
        in_specs=[tile, vec, vec, pl.BlockSpec((tms, D), lambda i: (jnp.maximum(i - nc, 0), 0))],
        out_specs=(tile, pl.BlockSpec((8, 128), lambda i: (0, 0))),
        out_shape=(jax.ShapeDtypeStruct((T, D), F32), jax.ShapeDtypeStruct((8, 128), F32)),
        compiler_params=_params(("arbitrary",)),
    )(xhat, g, b, target)


def _ada_fwd(s16, w_ada, b_cols, *, name):
    L, D, Na = w_ada.shape
    tn = _pick(Na, 512)

    def body(s_ref, w_ref, b_ref, o_ref):
        o_ref[...] = jnp.dot(s_ref[...], w_ref[...].astype(MXU_DTYPE), preferred_element_type=F32) + b_ref[...]

    return pl.pallas_call(
        body, name=name, grid=(L, Na // tn),
        in_specs=[pl.BlockSpec((16, D), lambda l, n: (0, 0)), pl.BlockSpec((None, D, tn), lambda l, n: (l, 0, n)),
                  pl.BlockSpec((None, 1, tn), lambda l, n: (l, 0, n))],
        out_specs=pl.BlockSpec((None, 16, tn), lambda l, n: (l, 0, n)),
        out_shape=jax.ShapeDtypeStruct((L, 16, Na), F32), compiler_params=_params(("arbitrary", "arbitrary")),
    )(s16, w_ada, b_cols.reshape(L, 1, Na))


def _ada_bwd(s16, dm16, w_ada, *, name):
    L, D, Na = w_ada.shape
    tn = _pick(Na, 512)

    def body(s_ref, dm_ref, w_ref, dw_out, ds_out):
        @pl.when((pl.program_id(0) == 0) & (pl.program_id(1) == 0))
        def _():
            ds_out[...] = jnp.zeros_like(ds_out)

        dm = dm_ref[...].astype(MXU_DTYPE)
        dw_out[...] = lax.dot_general(s_ref[...], dm, (((0,), (0,)), ((), ())), preferred_element_type=F32)
        ds_out[...] += lax.dot_general(dm, w_ref[...].astype(MXU_DTYPE), (((1,), (1,)), ((), ())), preferred_element_type=F32)

    return pl.pallas_call(
        body, name=name, grid=(L, Na // tn),
        in_specs=[pl.BlockSpec((16, D), lambda l, n: (0, 0)), pl.BlockSpec((None, 16, tn), lambda l, n: (l, 0, n)),
                  pl.BlockSpec((None, D, tn), lambda l, n: (l, 0, n))],
        out_specs=(pl.BlockSpec((None, D, tn), lambda l, n: (l, 0, n)), pl.BlockSpec((16, D), lambda l, n: (0, 0))),
        out_shape=(jax.ShapeDtypeStruct((L, D, Na), F32), jax.ShapeDtypeStruct((16, D), F32)),
        compiler_params=_params(("arbitrary", "arbitrary")),
    )(s16, dm16, w_ada)


def _silu_rows(cvec, *, name):
    R, D = cvec.shape

    def body(c_ref, s_out, ds_out):
        c = c_ref[...]
        sg = _sigmoid(c)
        s_out[...] = (c * sg).astype(MXU_DTYPE)
        ds_out[...] = sg * (1.0 + c * (1.0 - sg))

    return pl.pallas_call(
        body, name=name, out_shape=(jax.ShapeDtypeStruct((R, D), MXU_DTYPE), jax.ShapeDtypeStruct((R, D), F32)),
    )(cvec)


def _sum_leading(v, order, *, name, scale_by=None):
    N, R, C = v.shape
    tr = _pick(R, max(8, ADAM_BLOCK_ELEMS // C), 8)

    def body(v_ref, *rest):
        acc = v_ref[order[0]]
        for j in order[1:]:
            acc = acc + v_ref[j]
        if scale_by is not None:
            acc = acc * rest[0][...]
        rest[-1][...] = acc

    in_specs = [pl.BlockSpec((N, tr, C), lambda i: (0, i, 0))]
    args = [v]
    if scale_by is not None:
        in_specs.append(pl.BlockSpec((tr, C), lambda i: (i, 0)))
        args.append(scale_by)
    return pl.pallas_call(
        body, name=name, grid=(R // tr,), in_specs=in_specs, out_specs=pl.BlockSpec((tr, C), lambda i: (i, 0)),
        out_shape=jax.ShapeDtypeStruct((R, C), F32), compiler_params=_params(("arbitrary",)),
    )(*args)


def _sum_scattered(g, land, chip_arr, l, n_layers, prev, *, name):
    S, R, C = g.shape
    tr = _pick(R, max(8, ADAM_BLOCK_ELEMS // C), 8)

    def body(ch_ref, g_ref, a_ref, b_ref, c_ref, *rest):
        rest[-1][...] = ((g_ref[...] + a_ref[...]) + b_ref[...]) + c_ref[...]

    def slot(d):
        return pl.BlockSpec((None, tr, C), lambda i, ch: ((ch[0] + d) % S, i, 0))

    in_specs = [slot(0), slot(1), slot(2), slot(3)]
    args, aliases = [chip_arr, g, land, land, land], {}
    if prev is not None:
        in_specs.append(pl.BlockSpec(memory_space=pl.ANY))
        args.append(prev)
        aliases = {5: 0}
    grid_spec = pltpu.PrefetchScalarGridSpec(
        num_scalar_prefetch=1, grid=(R // tr,), in_specs=in_specs,
        out_specs=pl.BlockSpec((None, tr, C), lambda i, ch: (l, i, 0)))
    return pl.pallas_call(
        body, name=name, grid_spec=grid_spec, out_shape=jax.ShapeDtypeStruct((n_layers, R, C), F32),
        input_output_aliases=aliases, compiler_params=_params(("arbitrary",)),
    )(*args)


def _adamw(w, g, m, v, g_other=None, *, name):
    shape = w.shape
    C = shape[-1]
    R = w.size // C
    two = g_other is not None
    ins = [t.reshape(R, C) for t in ((w, g, m, v, g_other) if two else (w, g, m, v))]
    tr = _pick(R, max(8, ADAM_BLOCK_ELEMS // C), 8)

    def body(w_ref, g_ref, m_ref, v_ref, *rest):
        d_out, m_out, v_out = rest[-3:]
        gg = g_ref[...]
        if two:
            gg = gg + rest[0][...]
            rest[1][...] = gg
        mn = ADAM_B1 * m_ref[...] + (1.0 - ADAM_B1) * gg
        vn = ADAM_B2 * v_ref[...] + (1.0 - ADAM_B2) * (gg * gg)
        m_hat = mn / (1.0 - ADAM_B1 ** ADAM_STEP)
        v_hat = vn / (1.0 - ADAM_B2 ** ADAM_STEP)
        d_out[...] = -ADAM_LR * (m_hat / (jnp.sqrt(v_hat) + ADAM_EPS) + ADAM_WD * w_ref[...])
        m_out[...] = mn
        v_out[...] = vn

    blk = pl.BlockSpec((tr, C), lambda i: (i, 0))
    shp = jax.ShapeDtypeStruct((R, C), F32)
    n_out = 4 if two else 3
    outs = pl.pallas_call(
        body, name=name, grid=(R // tr,), in_specs=[blk] * len(ins), out_specs=(blk,) * n_out, out_shape=(shp,) * n_out,
        compiler_params=_params(("arbitrary",)),
    )(*ins)
    return tuple(o.reshape(shape) for o in outs)


def _place():
    x, y, c = lax.axis_index("x"), lax.axis_index("y"), lax.axis_index("c")
    return x, y, c, [(1 - x, y), (x, 1 - y), (1 - x, 1 - y)]


def _allgather_small(v, *, name, after=None):
    m_per, n = v.shape

    def body(x_ref, *rest):
        out_ref, send_sems, recv_sems, local_sem = rest[-4:]
        x, y, c, chips = _place()
        me, sibling = (x, y, c), (x, y, 1 - c)

        def rows(px, py, pc):
            return out_ref.at[pl.ds((4 * px + 2 * py + pc) * m_per, m_per), :]

        def copy(k, block, to, src=None):
            return pltpu.make_async_remote_copy(
                src_ref=rows(*block) if src is None else src, dst_ref=rows(*block),
                send_sem=send_sems.at[k], recv_sem=recv_sems.at[k], device_id=to, device_id_type=MESH)

        mine = pltpu.make_async_copy(x_ref, rows(*me), local_sem)
        mine.start()
        first = [copy(0, me, sibling, src=x_ref)]
        first += [copy(1 + j, me, (*chip, c), src=x_ref) for j, chip in enumerate(chips)]
        for cp in first:
            cp.start()
        passed = [copy(4 + j, (*chip, c), sibling) for j, chip in enumerate(chips)]
        for j, chip in enumerate(chips):
            copy(1 + j, (*chip, c), me).wait_recv()
            passed[j].start()
        copy(0, sibling, me).wait_recv()
        for j, chip in enumerate(chips):
            copy(4 + j, (*chip, 1 - c), me).wait_recv()
        for cp in first + passed:
            cp.wait_send()
        mine.wait()

    in_specs, args = [pl.BlockSpec(memory_space=pltpu.VMEM)], [v]
    if after is not None:
        in_specs.append(pl.BlockSpec(memory_space=pl.ANY))
        args.append(after)
    return pl.pallas_call(
        body, name=name, out_shape=jax.ShapeDtypeStruct((N_DEV * m_per, n), v.dtype),
        in_specs=in_specs, out_specs=pl.BlockSpec(memory_space=pltpu.VMEM),
        scratch_shapes=[pltpu.SemaphoreType.DMA((7,)), pltpu.SemaphoreType.DMA((7,)), pltpu.SemaphoreType.DMA],
        compiler_params=pltpu.CompilerParams(vmem_limit_bytes=VMEM_LIMIT_BYTES),
    )(*args)


_HBM = pl.BlockSpec(memory_space=pltpu.HBM)
_SEM = pl.BlockSpec(memory_space=pltpu.SEMAPHORE)
_ANY = pl.BlockSpec(memory_space=pl.ANY)
_TOKEN = jax.ShapeDtypeStruct((8, 128), F32)


def _in_hbm(a):
    return pltpu.with_memory_space_constraint(a, pltpu.HBM)


def _place_cast(w, l, chip_arr, *, name):
    L, C = w.shape[0], w.shape[-1]
    R = w.size // (L * C)
    tr = _pick(R, max(8, ADAM_BLOCK_ELEMS // C), 16)

    def body(ch_ref, w_ref, o_ref):
        o_ref[...] = w_ref[...].astype(MXU_DTYPE)

    grid_spec = pltpu.PrefetchScalarGridSpec(
        num_scalar_prefetch=1, grid=(R // tr,),
        in_specs=[pl.BlockSpec((None, tr, C), lambda i, ch: (l, i, 0))],
        out_specs=pl.BlockSpec((None, tr, C), lambda i, ch: (ch[0], i, 0)))
    return pl.pallas_call(
        body, name=name, grid_spec=grid_spec, out_shape=jax.ShapeDtypeStruct((N_CHIPS, R, C), MXU_DTYPE),
        compiler_params=_params(("arbitrary",)),
    )(chip_arr, w.reshape(L, R, C))


def _chip_copies(srcs, lands, sends, recvs, k, j, px, py):
    x, y, c, _ = _place()
    me, peer = 2 * x + y, 2 * px + py
    src = srcs[k].at[peer] if srcs else lands[k].at[me]
    return pltpu.make_async_remote_copy(
        src_ref=src, dst_ref=lands[k].at[me], send_sem=sends[k].at[j], recv_sem=recvs[k].at[j],
        device_id=(px, py, c), device_id_type=MESH), peer


def _exchange_start(srcs, lands, after, *, name):
    ns, n = len(srcs), len(lands)
    arrays = (*srcs, *lands)

    def body(*refs):
        outs = refs[-(2 * n + ns + n + 1):]
        chips = _place()[3]
        for k in range(n):
            for j, (px, py) in enumerate(chips):
                _chip_copies(refs[:ns], refs[ns:ns + n], outs[:n], outs[n:2 * n], k, j, px, py)[0].start()
        outs[-1][...] = jnp.zeros_like(outs[-1])

    sem = pltpu.SemaphoreType.DMA((N_CHIPS - 1,))
    args = [_in_hbm(a) for a in arrays]
    in_specs = [_HBM] * (ns + n)
    if after is not None:
        args.append(after)
        in_specs.append(_ANY)
    outs = pl.pallas_call(
        body, name=name,
        out_shape=(*[sem] * (2 * n), *[pltpu.HBM(a.shape, a.dtype) for a in arrays], _TOKEN),
        in_specs=in_specs, out_specs=(*[_SEM] * (2 * n), *[_HBM] * (ns + n), pl.BlockSpec(memory_space=pltpu.VMEM)),
        input_output_aliases={k: 2 * n + k for k in range(ns + n)},
        compiler_params=pltpu.CompilerParams(has_side_effects=pltpu.SideEffectType.DATAFLOW_SIDE_EFFECTING),
    )(*args)
    return outs[:n], outs[n:2 * n], outs[2 * n:2 * n + ns], outs[2 * n + ns:2 * n + ns + n], outs[-1]


def _exchange_wait(sends, recvs, srcs, lands, after, *, name):
    ns, n = len(srcs), len(lands)

    def body(*refs):
        srcs_r, lands_r = refs[:ns], refs[ns:ns + n]
        sends_r, recvs_r = refs[ns + n:ns + 2 * n], refs[ns + 2 * n:ns + 3 * n]
        chips = _place()[3]
        for k in range(n):
            for j, (px, py) in enumerate(chips):
                cp, peer = _chip_copies(srcs_r, lands_r, sends_r, recvs_r, k, j, px, py)
                cp.wait_send()
                pltpu.make_async_remote_copy(
                    src_ref=lands_r[k].at[peer], dst_ref=lands_r[k].at[peer], send_sem=sends_r[k].at[j],
                    recv_sem=recvs_r[k].at[j], device_id=(px, py, _place()[2]), device_id_type=MESH).wait_recv()

    outs = pl.pallas_call(
        body, name=name, out_shape=[pltpu.HBM(a.shape, a.dtype) for a in (*srcs, *lands)],
        in_specs=[*[_HBM] * (ns + n), *[_SEM] * (2 * n), _ANY], out_specs=[_HBM] * (ns + n),
        input_output_aliases={k: k for k in range(ns + n)},
        compiler_params=pltpu.CompilerParams(has_side_effects=pltpu.SideEffectType.DATAFLOW_SIDE_EFFECTING),
    )(*srcs, *lands, *sends, *recvs, after)
    return outs[:ns], outs[ns:]


def _swap_sibling(ps, *, name):
    n = len(ps)

    def body(*refs):
        x, y, c, _ = _place()
        cps = [pltpu.make_async_remote_copy(src_ref=refs[k], dst_ref=refs[n + k], send_sem=refs[2 * n].at[k],
                                            recv_sem=refs[2 * n + 1].at[k], device_id=(x, y, 1 - c), device_id_type=MESH)
               for k in range(n)]
        for cp in cps:
            cp.start()
        for cp in cps:
            cp.wait()

    return pl.pallas_call(
        body, name=name, out_shape=[jax.ShapeDtypeStruct(p.shape, p.dtype) for p in ps],
        in_specs=[_ANY] * n, out_specs=[_ANY] * n,
        scratch_shapes=[pltpu.SemaphoreType.DMA((n,)), pltpu.SemaphoreType.DMA((n,))],
    )(*ps)


def kernel(x, c, ctx, c_ctx, w_ada, b_ada, ln_g, ln_b, ff1_in, ff1_out, ff2_in, ff2_out, w_in, conv4_w, conv4_b, w_rg, b_rg, w_ig, b_ig, lam, conv31_w, conv31_b, cln_g, cln_b, w_out, b_out, loss_target, m_c_ctx, m_w_ada, m_b_ada, m_ln_g, m_ln_b, m_ff1_in, m_ff1_out, m_ff2_in, m_ff2_out, m_w_in, m_conv4_w, m_conv4_b, m_w_rg, m_b_rg, m_w_ig, m_b_ig, m_lam, m_conv31_w, m_conv31_b, m_cln_g, m_cln_b, m_w_out, m_b_out, v_c_ctx, v_w_ada, v_b_ada, v_ln_g, v_ln_b, v_ff1_in, v_ff1_out, v_ff2_in, v_ff2_out, v_w_in, v_conv4_w, v_conv4_b, v_w_rg, v_b_rg, v_w_ig, v_b_ig, v_lam, v_conv31_w, v_conv31_b, v_cln_g, v_cln_b, v_w_out, v_b_out):
    weights = dict(c_ctx=c_ctx, w_ada=w_ada, b_ada=b_ada, ln_g=ln_g, ln_b=ln_b, ff1_in=ff1_in, ff1_out=ff1_out,
                   ff2_in=ff2_in, ff2_out=ff2_out, w_in=w_in, conv4_w=conv4_w, conv4_b=conv4_b, w_rg=w_rg, b_rg=b_rg,
                   w_ig=w_ig, b_ig=b_ig, lam=lam, conv31_w=conv31_w, conv31_b=conv31_b, cln_g=cln_g, cln_b=cln_b,
                   w_out=w_out, b_out=b_out)
    m_in = dict(c_ctx=m_c_ctx, w_ada=m_w_ada, b_ada=m_b_ada, ln_g=m_ln_g, ln_b=m_ln_b, ff1_in=m_ff1_in, ff1_out=m_ff1_out,
                ff2_in=m_ff2_in, ff2_out=m_ff2_out, w_in=m_w_in, conv4_w=m_conv4_w, conv4_b=m_conv4_b, w_rg=m_w_rg,
                b_rg=m_b_rg, w_ig=m_w_ig, b_ig=m_b_ig, lam=m_lam, conv31_w=m_conv31_w, conv31_b=m_conv31_b,
                cln_g=m_cln_g, cln_b=m_cln_b, w_out=m_w_out, b_out=m_b_out)
    v_in = dict(c_ctx=v_c_ctx, w_ada=v_w_ada, b_ada=v_b_ada, ln_g=v_ln_g, ln_b=v_ln_b, ff1_in=v_ff1_in, ff1_out=v_ff1_out,
                ff2_in=v_ff2_in, ff2_out=v_ff2_out, w_in=v_w_in, conv4_w=v_conv4_w, conv4_b=v_conv4_b, w_rg=v_w_rg,
                b_rg=v_b_rg, w_ig=v_w_ig, b_ig=v_b_ig, lam=v_lam, conv31_w=v_conv31_w, conv31_b=v_conv31_b,
                cln_g=v_cln_g, cln_b=v_cln_b, w_out=v_w_out, b_out=v_b_out)
    order = list(weights)

    ax, ay, ac = lax.axis_index("x"), lax.axis_index("y"), lax.axis_index("c")
    chip = 2 * ax + ay
    dev = 4 * ax + 2 * ay + ac
    chip_arr = jnp.reshape(chip, (1,)).astype(jnp.int32)

    L, D, Na = w_ada.shape
    Tl, Tc = x.shape[1], ctx.shape[1]
    T = Tc + Tl
    tms = Tc
    C = conv4_w.shape[2] * N_CHIPS
    nH, hds, hd = w_rg.shape[2], w_rg.shape[3], w_rg.shape[4]
    K31 = conv31_w.shape[1]
    assert L == 2 and Tl % tms == 0 and tms % GRID_W == 0 and tms % 8 == 0 and tms & (tms - 1) == 0
    assert hds * N_CHIPS == hd and nH * hd == C and D == 2 * C and K31 // 2 < CONV_PAD
    alpha = (2 * L) ** 0.25
    n_mod = N_CHIPS * Na // D

    def shard_cols(full, width):
        return lax.dynamic_slice_in_dim(full, chip * width, width, axis=full.ndim - 1)

    c8 = jnp.zeros((8, D), F32).at[0].set(c[0]).at[1].set(c_ctx)
    c_all = _allgather_small(c8, name="gather_cond").reshape(N_DEV, 8, D)
    c16 = jnp.concatenate([c_all[:, 0], c_ctx[None], jnp.zeros((7, D), F32)], axis=0)
    s16, ds16 = _silu_rows(c16, name="silu_cond")
    mod_part = _ada_fwd(s16, w_ada, shard_cols(b_ada, Na), name="ada_fwd")
    mod_all = _allgather_small(mod_part.reshape(L * 16, Na), name="gather_mod").reshape(N_DEV, L, 16, Na)
    mod_full = jnp.transpose(mod_all[0::2], (1, 2, 0, 3)).reshape(L, 16, N_CHIPS * Na)
    mod_rows = jnp.stack([mod_full[:, 8], lax.dynamic_index_in_dim(mod_full, dev, axis=1, keepdims=False)], axis=1)
    mod = mod_rows.reshape(L, 2, n_mod, D)

    def mvec(l, k):
        return mod[l, :, k, :]

    def full_gate(g):
        return jnp.transpose(g, (1, 2, 0, 3, 4)).reshape(2, nH, hd, hd)

    small_sharded = ("ln_g", "ln_b", "conv4_w", "b_rg", "b_ig", "lam", "conv31_w")
    pieces = {n: weights[n].reshape(-1, weights[n].shape[-1]) for n in small_sharded}
    widths = {n: p.shape[1] for n, p in pieces.items()}
    rows_of = {n: p.shape[0] for n, p in pieces.items()}
    wcat = max(widths.values())
    cat = jnp.concatenate([jnp.pad(p, ((0, 0), (0, wcat - p.shape[1]))) for p in pieces.values()], axis=0)
    rpad = -cat.shape[0] % 8
    cat_all = _allgather_small(jnp.pad(cat, ((0, rpad), (0, 0))), name="gather_small").reshape(N_DEV, -1, wcat)
    full_small, r0 = {}, 0
    for n in small_sharded:
        blk = cat_all[0::2, r0:r0 + rows_of[n], :widths[n]]
        full_small[n] = jnp.transpose(blk, (1, 0, 2)).reshape(rows_of[n], N_CHIPS * widths[n])
        r0 += rows_of[n]
    ln_g_f = full_small["ln_g"].reshape(L, 3, 1, D)
    ln_b_f = full_small["ln_b"].reshape(L, 3, 1, D)
    conv4_w_f = full_small["conv4_w"].reshape(L, 4, C)
    b_rg_f = full_small["b_rg"].reshape(L, 2, 1, C)
    b_ig_f = full_small["b_ig"].reshape(L, 2, 1, C)
    lam_f = full_small["lam"].reshape(L, 2, 1, C)
    conv31_w_f = full_small["conv31_w"].reshape(L, K31, C)

    ones, zeros = jnp.ones((1, D), F32), jnp.zeros((1, D), F32)

    big_names = ("ff1_in", "ff1_out", "w_in", "w_rg", "w_ig", "w_out", "ff2_in", "ff2_out")
    inflight, tok = [], mod[0, 0, 0, :1] + cat_all[0, 0, :1]
    for l in range(L):
        lands = [_place_cast(weights[n], l, chip_arr, name=f"place_{n}_{l}") for n in big_names]
        sends, recvs, _, lands_t, tok = _exchange_start([], lands, tok, name=f"gather_start_{l}")
        inflight.append({n: (sends[k], recvs[k], lands_t[k]) for k, n in enumerate(big_names)})
    gw = [{} for _ in range(L)]

    def gathered(l, n, after):
        s, r, land = inflight[l][n]
        land = _exchange_wait([s], [r], [], [land], after, name=f"gather_wait_{n}_{l}")[1][0]
        gw[l][n] = land.reshape(N_CHIPS, *weights[n].shape[1:])
        return gw[l][n]

    s0 = jnp.concatenate([ctx[0], x[0]], axis=0)
    cur = (s0, ones, zeros)
    saved = []
    for l in range(L):
        sv = {"in": cur}
        w = gathered(l, "ff1_in", tok if l == 0 else cur[0])
        h1, gu1, act1 = _in_proj(*cur, mvec(l, 0), mvec(l, 1), w, tc=Tc, swiglu=True, name=f"ffn1_in_{l}")
        xh1, rs1, f1 = _out_proj_ln(act1, gathered(l, "ff1_out", act1), zeros, *cur, mvec(l, 2), 0.5, alpha, tc=Tc, name=f"ffn1_out_{l}")
        sv.update(h1=h1, gu1=gu1, act1=act1, xh1=xh1, rs1=rs1, f1=f1)
        cur1 = (xh1, ln_g_f[l, 0], ln_b_f[l, 0])
        h2, z = _in_proj(*cur1, mvec(l, 3), mvec(l, 4), gathered(l, "w_in", xh1), tc=Tc, swiglu=False, name=f"mix_in_{l}")
        xc = _conv4_fwd(z, conv4_w_f[l], conv4_b[l][None], tms=tms, name=f"conv4_{l}")
        wr_l, wi_l = full_gate(gathered(l, "w_rg", xc)), full_gate(gathered(l, "w_ig", xc))
        sv.update(wr=wr_l, wi=wi_l)
        rec = []
        for d in range(2):
            rec.append(_lru_fwd(xc, wr_l[d], wi_l[d], b_rg_f[l, d], b_ig_f[l, d], lam_f[l, d],
                                rev=bool(d), tms=tms, name=f"lru_{l}_{d}"))
        ymix, uc = _convmod_fwd(rec[0][0], rec[1][0], z, conv31_w_f[l], conv31_b[l][None], cln_g[l][None], cln_b[l][None],
                                tms=tms, name=f"convmod_{l}")
        xh2, rs2, f2 = _out_proj_ln(ymix, gathered(l, "w_out", ymix), b_out[l][None], *cur1, mvec(l, 5), 1.0, alpha, tc=Tc, name=f"mix_out_{l}")
        sv.update(h2=h2, z=z, xc=xc, rec=rec, ymix=ymix, uc=uc, xh2=xh2, rs2=rs2, f2=f2)
        cur2 = (xh2, ln_g_f[l, 1], ln_b_f[l, 1])
        h3, gu3, act3 = _in_proj(*cur2, mvec(l, 6), mvec(l, 7), gathered(l, "ff2_in", xh2), tc=Tc, swiglu=True, name=f"ffn2_in_{l}")
        xh3, rs3, f3 = _out_proj_ln(act3, gathered(l, "ff2_out", act3), zeros, *cur2, mvec(l, 8), 0.5, alpha, tc=Tc, name=f"ffn2_out_{l}")
        sv.update(h3=h3, gu3=gu3, act3=act3, xh3=xh3, rs3=rs3, f3=f3)
        cur = (xh3, ln_g_f[l, 2], ln_b_f[l, 2])
        saved.append(sv)

    ds, loss_blk = _loss_head(*cur, loss_target[0], tc=Tc, tms=tms, name="loss_head")

    reduced = {n: None for n in big_names}
    pending = []

    def start_group(l, names, gs):
        gs = [g.reshape(N_CHIPS, -1, g.shape[-1]) for g in gs]
        lands = [lax.empty(g.shape, F32) for g in gs]
        sends, recvs, srcs_t, lands_t, token = _exchange_start(gs, lands, None, name=f"grad_start_{names[0]}_{l}")
        pending.append((sends, recvs, srcs_t, lands_t, names, l))
        return token

    def finish_group(after):
        sends, recvs, srcs_t, lands_t, names, l = pending.pop(0)
        gs, lands = _exchange_wait(sends, recvs, srcs_t, lands_t, after, name=f"grad_wait_{names[0]}_{l}")
        for k, n in enumerate(names):
            reduced[n] = _sum_scattered(gs[k], lands[k], chip_arr, l, L, reduced[n], name=f"grad_sum_{n}_{l}")

    def finish_older(keep, after):
        while len(pending) > keep:
            finish_group(after)

    dmod = [[None] * n_mod for _ in range(L)]
    d_ln_g = [[None] * 3 for _ in range(L)]
    d_ln_b = [[None] * 3 for _ in range(L)]
    small = {n: [None] * L for n in ("conv4_w", "conv4_b", "conv31_w", "conv31_b", "cln_g", "cln_b", "b_out")}
    gate_w = {n: [[None, None] for _ in range(L)] for n in ("w_rg", "w_ig", "b_rg", "b_ig", "lam")}

    def ffn_bwd(ds, l, k, names, sv_in, sfx, after):
        sv = saved[l]
        dy, dres, d_ln_g[l][k], d_ln_b[l][k], dmod[l][3 * k + 2], _ = _ln_bwd(
            ds, sv["xh" + sfx], sv["rs" + sfx], ln_g_f[l, k], sv["f" + sfx], mvec(l, 3 * k + 2), 0.5, alpha, after,
            tc=Tc, name=f"ffn{sfx}_ln_bwd_{l}")
        dg = _nt(dy, gw[l][names[1]], sv["gu" + sfx], name=f"ffn{sfx}_dact_{l}")
        g_in = _wgrad(sv["h" + sfx], dg, cols_sharded=True, name=f"ffn{sfx}_wgrad_in_{l}")
        t_in = start_group(l, (names[0],), (g_in,))
        g_out = _wgrad(sv["act" + sfx], dy, cols_sharded=False, name=f"ffn{sfx}_wgrad_out_{l}", after=t_in)
        t_out = start_group(l, (names[1],), (g_out,))
        ds_new, dmod[l][3 * k + 1], dmod[l][3 * k] = _dx_modbwd(
            dg, gw[l][names[0]], dres, *sv_in, mvec(l, 3 * k + 1), tc=Tc, name=f"ffn{sfx}_dx_{l}", after=t_out)
        return ds_new, t_out

    def gate_slots(g):
        g = g.reshape(2, nH, N_CHIPS, hds, hd)
        return jnp.transpose(g, (2, 0, 1, 3, 4)).reshape(N_CHIPS, 2 * nH * hds, hd)

    token = None
    for l in reversed(range(L)):
        sv = saved[l]
        cur1 = (sv["xh1"], ln_g_f[l, 0], ln_b_f[l, 0])
        cur2 = (sv["xh2"], ln_g_f[l, 1], ln_b_f[l, 1])
        ds, token = ffn_bwd(ds, l, 2, ("ff2_in", "ff2_out"), cur2, "3", token)
        finish_older(2, ds)
        dy, dres, d_ln_g[l][1], d_ln_b[l][1], dmod[l][5], small["b_out"][l] = _ln_bwd(
            ds, sv["xh2"], sv["rs2"], ln_g_f[l, 1], sv["f2"], mvec(l, 5), 1.0, alpha, token, tc=Tc, name=f"mix_ln_bwd_{l}")
        dymix = _nt(dy, gw[l]["w_out"], None, name=f"mix_dy_{l}")
        g_w_out = _wgrad(sv["ymix"], dy, cols_sharded=False, name=f"mix_wgrad_out_{l}")
        dhs, dz, small["conv31_w"][l], small["conv31_b"][l], small["cln_g"][l], small["cln_b"][l] = _convmod_bwd(
            dymix, sv["rec"][0][0], sv["rec"][1][0], sv["z"], sv["uc"], conv31_w_f[l], cln_g[l][None], cln_b[l][None],
            tms=tms, name=f"convmod_bwd_{l}")
        dxc = []
        for d in range(2):
            hd_, rd_, id_ = sv["rec"][d]
            o = _lru_bwd(dhs, hd_, rd_, id_, sv["xc"], sv["wr"][d], sv["wi"][d], lam_f[l, d],
                         rev=bool(d), tms=tms, name=f"lru_bwd_{l}_{d}")
            dxc.append(o[0])
            for n, val in zip(("w_rg", "w_ig", "b_rg", "b_ig", "lam"), o[1:]):
                gate_w[n][l][d] = val
        dz, small["conv4_w"][l], small["conv4_b"][l] = _conv4_bwd(dxc[0], dxc[1], sv["z"], conv4_w_f[l], dz, tms=tms, name=f"conv4_bwd_{l}")
        ds, dmod[l][4], dmod[l][3] = _dx_modbwd(dz, gw[l]["w_in"], dres, *cur1, mvec(l, 4), tc=Tc, name=f"mix_dx_{l}")
        g_w_in = _wgrad(sv["h2"], dz, cols_sharded=True, name=f"mix_wgrad_in_{l}")
        token = start_group(l, ("w_out", "w_in", "w_rg", "w_ig"),
                            (g_w_out, g_w_in, gate_slots(jnp.stack(gate_w["w_rg"][l])), gate_slots(jnp.stack(gate_w["w_ig"][l]))))
        finish_older(1, ds)
        ds, token = ffn_bwd(ds, l, 0, ("ff1_in", "ff1_out"), sv["in"], "1", token)
        finish_older(2, ds)

    grad_x = ds[Tc:][None]

    delta, new_m, new_v, grads = {}, {}, {}, {}

    def finish_weights(names, tag):
        others = _swap_sibling([reduced[n] for n in names], name=f"grad_swap_{tag}")
        for n, other in zip(names, others):
            shp = weights[n].shape
            grads[n], delta[n], new_m[n], new_v[n] = _adamw(weights[n], reduced[n].reshape(shp), m_in[n], v_in[n],
                                                            other.reshape(shp), name=f"adamw_{n}")

    finish_weights(("ff2_in", "ff2_out", "w_in", "w_out", "w_rg", "w_ig"), "early")
    finish_older(0, new_v["ff2_in"])
    finish_weights(("ff1_in", "ff1_out"), "late")


    dmod_arr = jnp.stack([jnp.stack(dmod[l], axis=1) for l in range(L)])
    dm_ctx = dmod_arr[:, 0].reshape(L, n_mod * D)
    dm_lat = dmod_arr[:, 1].reshape(L, n_mod * D)
    summed = {
        "loss": loss_blk[0:1, 0:1],
        "dm_ctx": dm_ctx,
        "ln_g": jnp.stack([jnp.concatenate(d_ln_g[l], axis=0) for l in range(L)]),
        "ln_b": jnp.stack([jnp.concatenate(d_ln_b[l], axis=0) for l in range(L)]),
        "conv4_w": jnp.stack(small["conv4_w"]),
        "conv4_b": jnp.concatenate(small["conv4_b"], axis=0),
        "b_rg": jnp.stack([jnp.concatenate(gate_w["b_rg"][l], axis=0) for l in range(L)]),
        "b_ig": jnp.stack([jnp.concatenate(gate_w["b_ig"][l], axis=0) for l in range(L)]),
        "lam": jnp.stack([jnp.concatenate(gate_w["lam"][l], axis=0) for l in range(L)]),
        "conv31_w": jnp.stack(small["conv31_w"]),
        "conv31_b": jnp.concatenate(small["conv31_b"], axis=0),
        "cln_g": jnp.concatenate(small["cln_g"], axis=0),
        "cln_b": jnp.concatenate(small["cln_b"], axis=0),
        "b_out": jnp.concatenate(small["b_out"], axis=0),
        "dm_lat": dm_lat,
    }
    flat = jnp.concatenate([v.reshape(-1) for v in summed.values()])
    n_flat = flat.shape[0]
    n_rows = -(-n_flat // 128)
    n_rows += -n_rows % 8
    vec = jnp.pad(flat, (0, n_rows * 128 - n_flat)).reshape(n_rows, 128)
    vec_all = _allgather_small(vec, name="gather_small_grads", after=new_v["ff1_out"]).reshape(N_DEV, n_rows, 128)
    vec_sum = _sum_leading(vec_all, tuple(range(N_DEV)), name="sum_small_grads").reshape(-1)
    tot, off = {}, 0
    for n, v in summed.items():
        tot[n] = vec_sum[off:off + v.size].reshape(v.shape)
        if n == "dm_lat":
            dm_lat_all = vec_all.reshape(N_DEV, -1)[:, off:off + v.size].reshape(N_DEV, L, n_mod * D)
        off += v.size
    loss = tot["loss"].reshape(())

    dm16 = jnp.concatenate([jnp.transpose(dm_lat_all, (1, 0, 2)), tot["dm_ctx"][:, None], jnp.zeros((L, 7, n_mod * D), F32)], axis=1)
    g_w_ada, ds16_part = _ada_bwd(s16, shard_cols(dm16, Na), w_ada, name="ada_bwd")
    ds_all = _allgather_small(ds16_part[8:16], name="gather_dcond").reshape(N_DEV, 8, D)
    g_c_ctx = _sum_leading(ds_all[:, 0:1], (0, 2, 4, 6), name="sum_dcond", scale_by=ds16[8:9]).reshape(D)
    g_b_ada = _sum_leading(jnp.stack([tot["dm_lat"], tot["dm_ctx"]]), (0, 1), name="sum_b_ada")

    grads.update(c_ctx=g_c_ctx, w_ada=g_w_ada, b_ada=g_b_ada)
    for n in ("ln_g", "ln_b", "conv4_w", "b_rg", "b_ig", "lam", "conv31_w"):
        grads[n] = shard_cols(tot[n], weights[n].shape[-1])
    for n in ("conv4_b", "conv31_b", "cln_g", "cln_b", "b_out"):
        grads[n] = tot[n]
    for n in order:
        if n not in reduced:
            delta[n], new_m[n], new_v[n] = _adamw(weights[n], grads[n], m_in[n], v_in[n], name=f"adamw_{n}")
    return (loss, grad_x, *[grads[n] for n in order], *[delta[n] for n in order],
            *[new_m[n] for n in order], *[new_v[n] for n in order])
```

```python
import functools

import jax
import jax.numpy as jnp
from jax import lax
from jax.experimental import pallas as pl
from jax.experimental.pallas import tpu as pltpu

F32 = jnp.float32
MXU_DTYPE = jnp.bfloat16
GRID_W = 64
RG_C = 8.0
LN_EPS = 1e-6
ADAM_LR, ADAM_B1, ADAM_B2, ADAM_EPS, ADAM_WD, ADAM_STEP = 0.001, 0.9, 0.999, 1e-08, 0.01, 10
LANES = 128
N_CHIPS = 4
N_DEV = 8
VMEM_LIMIT_BYTES = 56 * 1024 * 1024
TM_IN, TN_IN = 768, 256
TM_OUT, TK_OUT = 384, 1408
TM_NT, TN_NT, NT_CHUNK = 768, 1408, 256
TM_DX, TK_DX = 384, 1408
TK_WG, TB_WG = 768, 1408
ADAM_BLOCK_ELEMS = 256 * 1024
MESH = pl.DeviceIdType.MESH


def _pick(total, target, mult=128):
    for d in range(min(total, target), 0, -1):
        if total % d == 0 and d % mult == 0:
            return d
    return total


def _params(sem=None):
    kw = dict(vmem_limit_bytes=VMEM_LIMIT_BYTES)
    if sem is not None:
        kw["dimension_semantics"] = sem
    return pltpu.CompilerParams(**kw)


def _sigmoid(x):
    return 1.0 / (1.0 + jnp.exp(-x))


def _gelu(x):
    k = 0.7978845608028654
    t = jnp.tanh(k * (x + 0.044715 * (x * x * x)))
    return 0.5 * x * (1.0 + t)


def _gelu_grad(x):
    k = 0.7978845608028654
    t = jnp.tanh(k * (x + 0.044715 * (x * x * x)))
    return 0.5 * (1.0 + t) + 0.5 * x * (1.0 - t * t) * (k * (1.0 + 3.0 * 0.044715 * x * x))


def _log1p(e):
    u = 1.0 + e
    return jnp.where(u == 1.0, e, jnp.log(u) * (e / jnp.where(u == 1.0, 1.0, u - 1.0)))


def _softplus(y):
    return jnp.maximum(y, 0.0) + _log1p(jnp.exp(-jnp.abs(y)))


def _one_minus_sq(log_a, a):
    x = 2.0 * log_a
    series = -x * (1.0 + x * (0.5 + x * (1.0 / 6.0 + x * (1.0 / 24.0))))
    return jnp.where(x > -0.01, series, 1.0 - a * a)


def _strips(width, sw):
    return [(j * sw, (j + 1) * sw) for j in range(width // sw)]


def _rows(i, tm):
    return i * tm + lax.broadcasted_iota(jnp.int32, (tm, 1), 0)


def _sel(isctx, ref):
    return jnp.where(isctx, ref[0:1, :], ref[1:2, :])


def _colsum(v):
    return jnp.sum(v, axis=0, keepdims=True)


def _in_proj(xhat, g_in, b_in, shift, scale, w, *, tc, swiglu, name):
    T, D = xhat.shape
    S, _, Ns = w.shape
    tm, tn = _pick(T, TM_IN, 8), _pick(Ns, TN_IN)
    nps = Ns // tn
    nI = T // tm
    nN = (S // 2 if swiglu else S) * nps

    def body(x_ref, g_ref, b_ref, sh_ref, sc_ref, w_ref, h_out, *rest):
        i, n = pl.program_id(0), pl.program_id(1)
        h_scr = rest[-1]

        @pl.when(n == 0)
        def _():
            s = x_ref[...] * g_ref[...] + b_ref[...]
            isctx = _rows(i, tm) < tc
            h = (s * (1.0 + _sel(isctx, sc_ref)) + _sel(isctx, sh_ref)).astype(MXU_DTYPE)
            h_scr[...] = h
            h_out[...] = h

        h = h_scr[...]
        if swiglu:
            gu_out, act_out = rest[0], rest[1]
            gt = jnp.dot(h, w_ref[0], preferred_element_type=F32)
            ut = jnp.dot(h, w_ref[1], preferred_element_type=F32)
            gu_out[0] = gt.astype(MXU_DTYPE)
            gu_out[1] = ut.astype(MXU_DTYPE)
            act_out[...] = ((gt * _sigmoid(gt)) * ut).astype(MXU_DTYPE)
        else:
            rest[0][...] = jnp.dot(h, w_ref[...], preferred_element_type=F32)

    vec = pl.BlockSpec((1, D), lambda i, n: (0, 0))
    vec2 = pl.BlockSpec((2, D), lambda i, n: (0, 0))
    in_specs = [pl.BlockSpec((tm, D), lambda i, n: (i, 0)), vec, vec, vec2, vec2]
    h_shape = jax.ShapeDtypeStruct((T, D), MXU_DTYPE)
    h_spec = pl.BlockSpec((tm, D), lambda i, n: (i, 0))
    if swiglu:
        F = (S // 2) * Ns
        wv = w.reshape(2, S // 2, *w.shape[1:])
        in_specs.append(pl.BlockSpec((2, None, D, tn), lambda i, n: (0, n // nps, 0, n % nps)))
        out_shape = (h_shape, jax.ShapeDtypeStruct((2, T, F), MXU_DTYPE), jax.ShapeDtypeStruct((T, F), MXU_DTYPE))
        out_specs = (h_spec, pl.BlockSpec((2, tm, tn), lambda i, n: (0, i, n)), pl.BlockSpec((tm, tn), lambda i, n: (i, n)))
    else:
        wv = w
        in_specs.append(pl.BlockSpec((None, D, tn), lambda i, n: (n // nps, 0, n % nps)))
        out_shape = (h_shape, jax.ShapeDtypeStruct((S, T, Ns), F32))
        out_specs = (h_spec, pl.BlockSpec((None, tm, tn), lambda i, n: (n // nps, i, n % nps)))
    return pl.pallas_call(
        body, name=name, grid=(nI, nN), in_specs=in_specs, out_specs=out_specs, out_shape=out_shape,
        scratch_shapes=[pltpu.VMEM((tm, D), MXU_DTYPE)], compiler_params=_params(("arbitrary", "arbitrary")),
    )(xhat, g_in, b_in, shift, scale, wv)


def _out_proj_ln(a, w, bias, xin, g_in, b_in, gvec, gscale, alpha, *, tc, name):
    T, K = a.shape
    S, Ks, D = w.shape
    tm, tk = _pick(T, TM_OUT, 8), _pick(Ks, TK_OUT)
    kps = Ks // tk
    nK = S * kps

    def body(a_ref, w_ref, bias_ref, xin_ref, gi_ref, bi_ref, gv_ref, xh_out, rstd_out, f_out, acc):
        i, k = pl.program_id(0), pl.program_id(1)

        @pl.when(k == 0)
        def _():
            acc[...] = jnp.zeros_like(acc)

        acc[...] += jnp.dot(a_ref[...], w_ref[...], preferred_element_type=F32)

        @pl.when(k == nK - 1)
        def _():
            f = acc[...] + bias_ref[...]
            f_out[...] = f.astype(MXU_DTYPE)
            s = xin_ref[...] * gi_ref[...] + bi_ref[...]
            gv = gscale * _sel(_rows(i, tm) < tc, gv_ref)
            r = alpha * s + gv * f
            mu = jnp.mean(r, axis=-1, keepdims=True)
            d = r - mu
            var = jnp.mean(d * d, axis=-1, keepdims=True)
            rstd = lax.rsqrt(var + LN_EPS)
            xh_out[...] = d * rstd
            rstd_out[...] = rstd

    vec = pl.BlockSpec((1, D), lambda i, k: (0, 0))
    row = pl.BlockSpec((tm, D), lambda i, k: (i, 0))
    return pl.pallas_call(
        body, name=name, grid=(T // tm, nK),
        in_specs=[pl.BlockSpec((tm, tk), lambda i, k: (i, k)),
                  pl.BlockSpec((None, tk, D), lambda i, k: (k // kps, k % kps, 0)),
                  vec, row, vec, vec, pl.BlockSpec((2, D), lambda i, k: (0, 0))],
        out_specs=(row, pl.BlockSpec((tm, 1), lambda i, k: (i, 0)), row),
        out_shape=(jax.ShapeDtypeStruct((T, D), F32), jax.ShapeDtypeStruct((T, 1), F32),
                   jax.ShapeDtypeStruct((T, D), MXU_DTYPE)),
        scratch_shapes=[pltpu.VMEM((tm, D), F32)], compiler_params=_params(("arbitrary", "arbitrary")),
    )(a, w, bias, xin, g_in, b_in, gvec)


def _ln_bwd(ds, xhat, rstd, g_ln, f, gvec, gscale, alpha, after, *, tc, name):
    T, D = ds.shape
    tm = _pick(T, TM_OUT, 8)

    def body(ds_ref, xh_ref, rs_ref, gl_ref, f_ref, gv_ref, *rest):
        dy_out, dres_out, dgl_out, dbl_out, dgv_out, dbias_out = rest[-6:]
        i = pl.program_id(0)

        @pl.when(i == 0)
        def _():
            dgl_out[...] = jnp.zeros_like(dgl_out)
            dbl_out[...] = jnp.zeros_like(dbl_out)
            dgv_out[...] = jnp.zeros_like(dgv_out)
            dbias_out[...] = jnp.zeros_like(dbias_out)

        dsv, xh = ds_ref[...], xh_ref[...]
        dgl_out[...] += _colsum(dsv * xh)
        dbl_out[...] += _colsum(dsv)
        dxh = dsv * gl_ref[...]
        m1 = jnp.mean(dxh, axis=-1, keepdims=True)
        m2 = jnp.mean(dxh * xh, axis=-1, keepdims=True)
        dr = rs_ref[...] * (dxh - m1 - xh * m2)
        dres_out[...] = alpha * dr
        isctx = _rows(i, tm) < tc
        dyv = (gscale * _sel(isctx, gv_ref)) * dr
        dy_out[...] = dyv.astype(MXU_DTYPE)
        dbias_out[...] += _colsum(dyv)
        p = gscale * (dr * f_ref[...].astype(F32))
        dgv_out[0:1, :] += _colsum(jnp.where(isctx, p, 0.0))
        dgv_out[1:2, :] += _colsum(jnp.where(isctx, 0.0, p))

    vec = pl.BlockSpec((1, D), lambda i: (0, 0))
    vec2 = pl.BlockSpec((2, D), lambda i: (0, 0))
    row = pl.BlockSpec((tm, D), lambda i: (i, 0))
    in_specs = [row, row, pl.BlockSpec((tm, 1), lambda i: (i, 0)), vec, row, vec2]
    args = [ds, xhat, rstd, g_ln, f, gvec]
    if after is not None:
        in_specs.append(pl.BlockSpec(memory_space=pl.ANY))
        args.append(after)
    return pl.pallas_call(
        body, name=name, grid=(T // tm,), in_specs=in_specs,
        out_specs=(row, row, vec, vec, vec2, vec),
        out_shape=(jax.ShapeDtypeStruct((T, D), MXU_DTYPE), jax.ShapeDtypeStruct((T, D), F32),
                   jax.ShapeDtypeStruct((1, D), F32), jax.ShapeDtypeStruct((1, D), F32),
                   jax.ShapeDtypeStruct((2, D), F32), jax.ShapeDtypeStruct((1, D), F32)),
        compiler_params=_params(("arbitrary",)),
    )(*args)


def _nt(dy, w, gate_up, *, name):
    T, D = dy.shape
    S, Ks, _ = w.shape
    tm, tn = _pick(T, TM_NT, 8), _pick(Ks, TN_NT)
    nps = Ks // tn
    nN = S * nps
    F = S * Ks

    def body(dy_ref, w_ref, *rest):
        dyv = dy_ref[...]
        for c0 in range(0, tn, NT_CHUNK):
            c1 = min(c0 + NT_CHUNK, tn)
            d = lax.dot_general(dyv, w_ref[c0:c1, :], (((1,), (1,)), ((), ())), preferred_element_type=F32)
            if gate_up is None:
                rest[0][:, c0:c1] = d
            else:
                gu_ref, dg_out = rest
                g, u = gu_ref[0, :, c0:c1].astype(F32), gu_ref[1, :, c0:c1].astype(F32)
                sg = _sigmoid(g)
                dg_out[0, :, c0:c1] = (d * u * (sg * (1.0 + g * (1.0 - sg)))).astype(MXU_DTYPE)
                dg_out[1, :, c0:c1] = (d * (g * sg)).astype(MXU_DTYPE)

    in_specs = [pl.BlockSpec((tm, D), lambda i, n: (i, 0)),
                pl.BlockSpec((None, tn, D), lambda i, n: (n // nps, n % nps, 0))]
    args = [dy, w]
    if gate_up is None:
        out_shape = jax.ShapeDtypeStruct((T, F), F32)
        out_specs = pl.BlockSpec((tm, tn), lambda i, n: (i, n))
    else:
        Ns = 2 * F // S
        q = Ns // tn
        in_specs.append(pl.BlockSpec((2, tm, tn), lambda i, n: (0, i, n)))
        args.append(gate_up)
        out_shape = jax.ShapeDtypeStruct((2, S // 2, T, Ns), MXU_DTYPE)
        out_specs = pl.BlockSpec((2, None, tm, tn), lambda i, n: (0, n // q, i, n % q))
    out = pl.pallas_call(
        body, name=name, grid=(T // tm, nN), in_specs=in_specs, out_specs=out_specs, out_shape=out_shape,
        compiler_params=_params(("arbitrary", "arbitrary")),
    )(*args)
    return out if gate_up is None else out.reshape(S, T, out.shape[-1])


def _dx_modbwd(dg, w, dres, xhat_in, g_in, b_in, scale, *, tc, name, after=None):
    S, T, Ns = dg.shape
    D = w.shape[1]
    tm, tk = _pick(T, TM_DX, 8), _pick(Ns, TK_DX)
    kps = Ns // tk
    nK = S * kps

    def body(dg_ref, w_ref, dres_ref, xin_ref, gi_ref, bi_ref, sc_ref, *rest):
        ds_out, dsc_out, dsh_out, acc = rest[-4:]
        i, k = pl.program_id(0), pl.program_id(1)

        @pl.when((i == 0) & (k == 0))
        def _():
            dsc_out[...] = jnp.zeros_like(dsc_out)
            dsh_out[...] = jnp.zeros_like(dsh_out)

        @pl.when(k == 0)
        def _():
            acc[...] = jnp.zeros_like(acc)

        acc[...] += lax.dot_general(dg_ref[...], w_ref[...], (((1,), (1,)), ((), ())), preferred_element_type=F32)

        @pl.when(k == nK - 1)
        def _():
            dh = acc[...]
            isctx = _rows(i, tm) < tc
            ds_out[...] = dres_ref[...] + dh * (1.0 + _sel(isctx, sc_ref))
            s = xin_ref[...] * gi_ref[...] + bi_ref[...]
            p = dh * s
            dsc_out[0:1, :] += _colsum(jnp.where(isctx, p, 0.0))
            dsc_out[1:2, :] += _colsum(jnp.where(isctx, 0.0, p))
            dsh_out[0:1, :] += _colsum(jnp.where(isctx, dh, 0.0))
            dsh_out[1:2, :] += _colsum(jnp.where(isctx, 0.0, dh))

    vec = pl.BlockSpec((1, D), lambda i, k: (0, 0))
    vec2 = pl.BlockSpec((2, D), lambda i, k: (0, 0))
    row = pl.BlockSpec((tm, D), lambda i, k: (i, 0))
    in_specs = [pl.BlockSpec((None, tm, tk), lambda i, k: (k // kps, i, k % kps)),
                pl.BlockSpec((None, D, tk), lambda i, k: (k // kps, 0, k % kps)),
                row, row, vec, vec, vec2]
    args = [dg, w, dres, xhat_in, g_in, b_in, scale]
    if after is not None:
        in_specs.append(pl.BlockSpec(memory_space=pl.ANY))
        args.append(after)
    return pl.pallas_call(
        body, name=name, grid=(T // tm, nK), in_specs=in_specs,
        out_specs=(row, vec2, vec2),
        out_shape=(jax.ShapeDtypeStruct((T, D), F32), jax.ShapeDtypeStruct((2, D), F32), jax.ShapeDtypeStruct((2, D), F32)),
        scratch_shapes=[pltpu.VMEM((tm, D), F32)], compiler_params=_params(("arbitrary", "arbitrary")),
    )(*args)


def _wgrad(a, b, *, cols_sharded, name, after=None):
    T = a.shape[0]
    tk = _pick(T, TK_WG, 8)
    if cols_sharded:
        S, _, Ns = b.shape
        D = a.shape[1]
        tb = _pick(Ns, TB_WG)
        grid = (S, Ns // tb, T // tk)
        in_specs = [pl.BlockSpec((tk, D), lambda s, j, k: (k, 0)), pl.BlockSpec((None, tk, tb), lambda s, j, k: (s, k, j))]
        out_shape = jax.ShapeDtypeStruct((S, D, Ns), F32)
        out_specs = pl.BlockSpec((None, D, tb), lambda s, j, k: (s, 0, j))
    else:
        D = b.shape[1]
        S = N_CHIPS
        Ks = a.shape[1] // S
        ta = _pick(Ks, TB_WG)
        q = Ks // ta
        grid = (S, q, T // tk)
        in_specs = [pl.BlockSpec((tk, ta), lambda s, j, k: (k, s * q + j)), pl.BlockSpec((tk, D), lambda s, j, k: (k, 0))]
        out_shape = jax.ShapeDtypeStruct((S, Ks, D), F32)
        out_specs = pl.BlockSpec((None, ta, D), lambda s, j, k: (s, j, 0))

    def body(a_ref, b_ref, *rest):
        o_ref = rest[-1]

        @pl.when(pl.program_id(2) == 0)
        def _():
            o_ref[...] = jnp.zeros_like(o_ref)

        o_ref[...] += lax.dot_general(a_ref[...], b_ref[...], (((0,), (0,)), ((), ())), preferred_element_type=F32)

    args = [a, b]
    if after is not None:
        in_specs.append(pl.BlockSpec(memory_space=pl.ANY))
        args.append(after)
    return pl.pallas_call(
        body, name=name, grid=grid, in_specs=in_specs, out_specs=out_specs, out_shape=out_shape,
        compiler_params=_params(("arbitrary", "arbitrary", "arbitrary")),
    )(*args)


def _halo_specs(tms, T, C, slot=None):
    r8 = tms // 8
    last8 = T // 8 - 1
    if slot is None:
        return (pl.BlockSpec((8, C), lambda i: (jnp.maximum(i * r8 - 1, 0), 0)),
                pl.BlockSpec((tms, C), lambda i: (i, 0)),
                pl.BlockSpec((8, C), lambda i: (jnp.minimum((i + 1) * r8, last8), 0)))
    return (pl.BlockSpec((None, 8, C), lambda i: (slot, jnp.maximum(i * r8 - 1, 0), 0)),
            pl.BlockSpec((None, tms, C), lambda i: (slot, i, 0)),
            pl.BlockSpec((None, 8, C), lambda i: (slot, jnp.minimum((i + 1) * r8, last8), 0)))


def _extended(prev_ref, cur, next_ref, i, n_tiles):
    first = (i == 0) | (i == 1)
    last = (i == 0) | (i == n_tiles - 1)
    pv = jnp.where(first, 0.0, prev_ref[...])
    nx = jnp.where(last, 0.0, next_ref[...])
    return jnp.concatenate([pv, cur, nx], axis=0)


def _shifted(ext, o, tms):
    n = tms + 16
    return pltpu.roll(ext, (-o) % n, 0)[8:8 + tms]


def _conv4_fwd(z, w4, b4, *, tms, name):
    S, T, C = z.shape
    nT = T // tms

    def body(p_ref, c_ref, n_ref, w_ref, b_ref, o_ref):
        i = pl.program_id(0)
        ext = _extended(p_ref, c_ref[...], n_ref, i, nT)
        acc = jnp.zeros((tms, C), F32) + b_ref[...]
        for k in range(4):
            acc = acc + w_ref[k:k + 1, :] * _shifted(ext, k - 2, tms)
        o_ref[...] = acc

    return pl.pallas_call(
        body, name=name, grid=(nT,),
        in_specs=[*_halo_specs(tms, T, C, 0), pl.BlockSpec((4, C), lambda i: (0, 0)), pl.BlockSpec((1, C), lambda i: (0, 0))],
        out_specs=pl.BlockSpec((tms, C), lambda i: (i, 0)), out_shape=jax.ShapeDtypeStruct((T, C), F32),
        compiler_params=_params(("arbitrary",)),
    )(z, z, z, w4, b4)


def _conv4_bwd(dxa, dxb, z, w4, dz, *, tms, name):
    S, T, C = z.shape
    nT = T // tms

    def body(pa, ca, na, pb, cb, nb, px, cx, nx, w_ref, dz_in, dz_out, dw_out, db_out):
        i = pl.program_id(0)

        @pl.when(i == 0)
        def _():
            dw_out[...] = jnp.zeros_like(dw_out)
            db_out[...] = jnp.zeros_like(db_out)

        dcur = ca[...] + cb[...]
        first = (i == 0) | (i == 1)
        last = (i == 0) | (i == nT - 1)
        dext = jnp.concatenate([jnp.where(first, 0.0, pa[...] + pb[...]), dcur, jnp.where(last, 0.0, na[...] + nb[...])], axis=0)
        xext = _extended(px, cx[...], nx, i, nT)
        acc = jnp.zeros((tms, C), F32)
        for k in range(4):
            acc = acc + w_ref[k:k + 1, :] * _shifted(dext, 2 - k, tms)
            dw_out[k:k + 1, :] += _colsum(dcur * _shifted(xext, k - 2, tms))
        db_out[...] += _colsum(dcur)
        dz_out[...] = acc.astype(MXU_DTYPE)

    h = _halo_specs(tms, T, C)
    return pl.pallas_call(
        body, name=name, grid=(nT,),
        in_specs=[*h, *h, *_halo_specs(tms, T, C, 0), pl.BlockSpec((4, C), lambda i: (0, 0)), pl.BlockSpec(memory_space=pl.ANY)],
        out_specs=(pl.BlockSpec((None, tms, C), lambda i: (0, i, 0)), pl.BlockSpec((4, C), lambda i: (0, 0)),
                   pl.BlockSpec((1, C), lambda i: (0, 0))),
        out_shape=(jax.ShapeDtypeStruct(dz.shape, dz.dtype), jax.ShapeDtypeStruct((4, C), F32), jax.ShapeDtypeStruct((1, C), F32)),
        input_output_aliases={10: 0}, compiler_params=_params(("arbitrary",)),
    )(dxa, dxa, dxa, dxb, dxb, dxb, z, z, z, w4, dz)


def _scan_rows(a, b, h_in, rev, n):
    sub = lax.broadcasted_iota(jnp.int32, (n, 1), 0) & 7
    for sft in (1, 2, 4):
        if rev:
            a_sh, b_sh, valid = pltpu.roll(a, n - sft, 0), pltpu.roll(b, n - sft, 0), sub < 8 - sft
        else:
            a_sh, b_sh, valid = pltpu.roll(a, sft, 0), pltpu.roll(b, sft, 0), sub >= sft
        b = a * jnp.where(valid, b_sh, 0.0) + b
        a = a * jnp.where(valid, a_sh, 1.0)
    out, c = [None] * (n // 8), h_in
    for g in (reversed(range(n // 8)) if rev else range(n // 8)):
        out[g] = a[8 * g:8 * g + 8] * c + b[8 * g:8 * g + 8]
        c = out[g][0:1] if rev else out[g][7:8]
    return jnp.concatenate(out, axis=0)


def _scan_order(i, rev, nT):
    if not rev:
        return i
    return jnp.where(i == 0, 0, nT - i)


def _lru_fwd(xc, wr, wi, br, bi, lam, *, rev, tms, name):
    T, C = xc.shape
    nH, hd, _ = wr.shape
    nT = T // tms
    sw = min(hd, LANES)

    def body(xc_ref, wr_ref, wi_ref, br_ref, bi_ref, lam_ref, h_out, r_out, i_out, carry):
        @pl.when(pl.program_id(0) == 0)
        def _():
            carry[...] = jnp.zeros_like(carry)

        for hx in range(nH):
            xh = xc_ref[:, hx * hd:(hx + 1) * hd]
            xb = xh.astype(MXU_DTYPE)
            pre_r = jnp.dot(xb, wr_ref[hx], preferred_element_type=F32)
            pre_i = jnp.dot(xb, wi_ref[hx], preferred_element_type=F32)
            for a0, a1 in _strips(hd, sw):
                sl = slice(hx * hd + a0, hx * hd + a1)
                x = xh[:, a0:a1]
                r = _sigmoid(pre_r[:, a0:a1] + br_ref[:, sl])
                ig = _sigmoid(pre_i[:, a0:a1] + bi_ref[:, sl])
                log_a = (-RG_C * r) * _softplus(-lam_ref[:, sl])
                a = jnp.exp(log_a)
                b = jnp.sqrt(_one_minus_sq(log_a, a)) * (ig * x)
                h = _scan_rows(a, b, carry[:, sl], rev, tms)
                carry[:, sl] = h[0:1, :] if rev else h[tms - 1:tms, :]
                h_out[:, sl] = h
                r_out[:, sl] = r
                i_out[:, sl] = ig

    tile = pl.BlockSpec((tms, C), lambda i: (_scan_order(i, rev, nT), 0))
    wspec = pl.BlockSpec((nH, hd, hd), lambda i: (0, 0, 0))
    vec = pl.BlockSpec((1, C), lambda i: (0, 0))
    shp = jax.ShapeDtypeStruct((T, C), F32)
    return pl.pallas_call(
        body, name=name, grid=(nT,), in_specs=[tile, wspec, wspec, vec, vec, vec], out_specs=(tile, tile, tile),
        out_shape=(shp, shp, shp), scratch_shapes=[pltpu.VMEM((1, C), F32)], compiler_params=_params(("arbitrary",)),
    )(xc, wr, wi, br, bi, lam)


def _lru_bwd(dh, h, r, ig, xc, wr, wi, lam, *, rev, tms, name):
    T, C = xc.shape
    nH, hd, _ = wr.shape
    nT = T // tms
    r8 = tms // 8
    sw = min(hd, LANES)

    def tile_of(ip):
        return _scan_order(nT - 1 - ip, rev, nT)

    def halo_of(ip):
        i = nT - 1 - ip
        if not rev:
            return jnp.maximum(i * r8 - 1, 0)
        return jnp.where(i <= 1, 0, (nT - i + 1) * r8)

    def body(dh_ref, h_ref, hh_ref, r_ref, i_ref, xc_ref, wr_ref, wi_ref, lam_ref,
             dxc_out, dwr_out, dwi_out, dbr_out, dbi_out, dlam_out, ucarry):
        ip = pl.program_id(0)
        pos = nT - 1 - ip

        @pl.when(ip == 0)
        def _():
            ucarry[...] = jnp.zeros_like(ucarry)
            for o in (dwr_out, dwi_out, dbr_out, dbi_out, dlam_out):
                o[...] = jnp.zeros_like(o)

        rows = lax.broadcasted_iota(jnp.int32, (tms, 1), 0)
        nt = (((1,), (1,)), ((), ()))
        tn = (((0,), (0,)), ((), ()))
        for hx in range(nH):
            dpr_parts, dpi_parts, direct_parts = [], [], []
            for a0, a1 in _strips(hd, sw):
                sl = slice(hx * hd + a0, hx * hd + a1)
                x, rr, ii, hh, dhv = xc_ref[:, sl], r_ref[:, sl], i_ref[:, sl], h_ref[:, sl], dh_ref[:, sl]
                sp = _softplus(-lam_ref[:, sl])
                log_a = (-RG_C * rr) * sp
                a = jnp.exp(log_a)
                s = jnp.sqrt(_one_minus_sq(log_a, a))
                u_in = ucarry[:, sl]
                u = _scan_rows(a, a * dhv, u_in, not rev, tms)
                if rev:
                    u_next = jnp.where(rows == 0, u_in, pltpu.roll(u, 1, 0))
                    ucarry[:, sl] = u[tms - 1:tms, :]
                    h_halo = jnp.where(pos == 0, 0.0, hh_ref[0:1, sl])
                    h_prev = jnp.where(rows == tms - 1, h_halo, pltpu.roll(hh, tms - 1, 0))
                else:
                    u_next = jnp.where(rows == tms - 1, u_in, pltpu.roll(u, tms - 1, 0))
                    ucarry[:, sl] = u[0:1, :]
                    h_halo = jnp.where(pos == 0, 0.0, hh_ref[7:8, sl])
                    h_prev = jnp.where(rows == 0, h_halo, pltpu.roll(hh, 1, 0))
                g = dhv + u_next
                dgated = g * s
                dlog_a = (g * h_prev) * a - (g * (ii * x)) * (a * a) / s
                dlam_out[:, sl] += _colsum(dlog_a * (-RG_C * rr)) * (-_sigmoid(-lam_ref[:, sl]))
                dpr = (dlog_a * (-RG_C * sp)) * rr * (1.0 - rr)
                dpi = (dgated * x) * ii * (1.0 - ii)
                dbr_out[:, sl] += _colsum(dpr)
                dbi_out[:, sl] += _colsum(dpi)
                dpr_parts.append(dpr.astype(MXU_DTYPE))
                dpi_parts.append(dpi.astype(MXU_DTYPE))
                direct_parts.append(dgated * ii)
            cat = (lambda p: p[0] if len(p) == 1 else jnp.concatenate(p, axis=1))
            dprh, dpih = cat(dpr_parts), cat(dpi_parts)
            xh = xc_ref[:, hx * hd:(hx + 1) * hd].astype(MXU_DTYPE)
            dxc_out[:, hx * hd:(hx + 1) * hd] = (cat(direct_parts)
                                                + lax.dot_general(dprh, wr_ref[hx], nt, preferred_element_type=F32)
                                                + lax.dot_general(dpih, wi_ref[hx], nt, preferred_element_type=F32))
            dwr_out[hx] += lax.dot_general(xh, dprh, tn, preferred_element_type=F32)
            dwi_out[hx] += lax.dot_general(xh, dpih, tn, preferred_element_type=F32)

    tile = pl.BlockSpec((tms, C), lambda ip: (tile_of(ip), 0))
    wspec = pl.BlockSpec((nH, hd, hd), lambda ip: (0, 0, 0))
    vec = pl.BlockSpec((1, C), lambda ip: (0, 0))
    wshape = jax.ShapeDtypeStruct((nH, hd, hd), F32)
    vshape = jax.ShapeDtypeStruct((1, C), F32)
    return pl.pallas_call(
        body, name=name, grid=(nT,),
        in_specs=[tile, tile, pl.BlockSpec((8, C), lambda ip: (halo_of(ip), 0)), tile, tile, tile, wspec, wspec, vec],
        out_specs=(tile, wspec, wspec, vec, vec, vec),
        out_shape=(jax.ShapeDtypeStruct((T, C), F32), wshape, wshape, vshape, vshape, vshape),
        scratch_shapes=[pltpu.VMEM((1, C), F32)], compiler_params=_params(("arbitrary",)),
    )(dh, h, h, r, ig, xc, wr, wi, lam)


CONV_PAD = 16


def _shift_bank(u):
    pad = jnp.zeros((CONV_PAD, u.shape[1]), u.dtype)
    ext = jnp.concatenate([pad, u, pad], axis=0)
    return [ext] + [pltpu.roll(ext, r, 0) for r in range(1, 8)]


def _shifted_rows(bank, o, rows):
    a = -((-o) // 8)
    start = CONV_PAD + 8 * a
    return bank[8 * a - o][start:start + rows]


def _convmod_fwd(hf, hb, z, w31, b31, clg, clb, *, tms, name):
    S, T, C = z.shape
    K = w31.shape[0]
    nT = T // tms

    sw = min(C, LANES)
    G = GRID_W

    def body(hf_ref, hb_ref, gr_ref, cv_ref, cg_ref, w_ref, b_ref, g_ref, bb_ref, y_out, uc_out):
        i = pl.program_id(0)
        for a0, a1 in _strips(C, sw):
            y_out[:, a0:a1] = ((hf_ref[:, a0:a1] + hb_ref[:, a0:a1]) * _gelu(gr_ref[:, a0:a1])).astype(MXU_DTYPE)

        def conv(seg):
            for r0 in range(0, tms, seg):
                rs = slice(r0, r0 + seg)
                s1 = jnp.zeros((seg, 1), F32)
                for a0, a1 in _strips(C, sw):
                    bank = _shift_bank(cv_ref[rs, a0:a1] * _sigmoid(cg_ref[rs, a0:a1]))
                    acc = jnp.zeros((seg, sw), F32) + b_ref[:, a0:a1]
                    for k in range(K):
                        acc = acc + w_ref[k:k + 1, a0:a1] * _shifted_rows(bank, k - K // 2, seg)
                    uc_out[rs, a0:a1] = acc
                    s1 = s1 + jnp.sum(acc, axis=-1, keepdims=True)
                mu = s1 / C
                s2 = jnp.zeros((seg, 1), F32)
                for a0, a1 in _strips(C, sw):
                    d = uc_out[rs, a0:a1] - mu
                    s2 = s2 + jnp.sum(d * d, axis=-1, keepdims=True)
                rstd = lax.rsqrt(s2 / C + LN_EPS)
                for a0, a1 in _strips(C, sw):
                    yl = (uc_out[rs, a0:a1] - mu) * rstd * g_ref[:, a0:a1] + bb_ref[:, a0:a1]
                    y_out[rs, C + a0:C + a1] = (yl * _sigmoid(yl)).astype(MXU_DTYPE)

        @pl.when(i == 0)
        def _():
            conv(tms)

        @pl.when(i != 0)
        def _():
            conv(G)

    tile = pl.BlockSpec((tms, C), lambda i: (i, 0))
    vec = pl.BlockSpec((1, C), lambda i: (0, 0))
    zs = [pl.BlockSpec((None, tms, C), functools.partial(lambda i, s: (s, i, 0), s=s)) for s in (1, 2, 3)]
    return pl.pallas_call(
        body, name=name, grid=(nT,),
        in_specs=[tile, tile, *zs, pl.BlockSpec((K, C), lambda i: (0, 0)), vec, vec, vec],
        out_specs=(pl.BlockSpec((tms, 2 * C), lambda i: (i, 0)), tile),
        out_shape=(jax.ShapeDtypeStruct((T, 2 * C), MXU_DTYPE), jax.ShapeDtypeStruct((T, C), F32)),
        compiler_params=_params(("arbitrary",)),
    )(hf, hb, z, z, z, w31, b31, clg, clb)


def _convmod_bwd(dymix, hf, hb, z, uc, w31, clg, clb, *, tms, name):
    S, T, C = z.shape
    K = w31.shape[0]
    nT = T // tms

    sw = min(C, LANES)

    def body(dy_ref, hf_ref, hb_ref, gr_ref, cv_ref, cg_ref, uc_ref, w_ref, g_ref, bb_ref,
             dhs_out, dz_out, dw_out, db_out, dg_out, dbb_out, dxh_buf):
        i = pl.program_id(0)

        @pl.when(i == 0)
        def _():
            for o in (dw_out, db_out, dg_out, dbb_out):
                o[...] = jnp.zeros_like(o)

        for a0, a1 in _strips(C, sw):
            dyr, gr = dy_ref[:, a0:a1], gr_ref[:, a0:a1]
            dhs_out[:, a0:a1] = dyr * _gelu(gr)
            dz_out[0, :, a0:a1] = jnp.zeros((tms, sw), MXU_DTYPE)
            dz_out[1, :, a0:a1] = (dyr * (hf_ref[:, a0:a1] + hb_ref[:, a0:a1]) * _gelu_grad(gr)).astype(MXU_DTYPE)

        def conv(seg):
            for r0 in range(0, tms, seg):
                rs = slice(r0, r0 + seg)
                s1 = jnp.zeros((seg, 1), F32)
                for a0, a1 in _strips(C, sw):
                    s1 = s1 + jnp.sum(uc_ref[rs, a0:a1], axis=-1, keepdims=True)
                mu = s1 / C
                s2 = jnp.zeros((seg, 1), F32)
                for a0, a1 in _strips(C, sw):
                    d = uc_ref[rs, a0:a1] - mu
                    s2 = s2 + jnp.sum(d * d, axis=-1, keepdims=True)
                rstd = lax.rsqrt(s2 / C + LN_EPS)
                m1 = jnp.zeros((seg, 1), F32)
                m2 = jnp.zeros((seg, 1), F32)
                for a0, a1 in _strips(C, sw):
                    xh = (uc_ref[rs, a0:a1] - mu) * rstd
                    yl = xh * g_ref[:, a0:a1] + bb_ref[:, a0:a1]
                    sg = _sigmoid(yl)
                    dyl = dy_ref[rs, C + a0:C + a1] * (sg * (1.0 + yl * (1.0 - sg)))
                    dg_out[:, a0:a1] += _colsum(dyl * xh)
                    dbb_out[:, a0:a1] += _colsum(dyl)
                    dxh = dyl * g_ref[:, a0:a1]
                    dxh_buf[rs, a0:a1] = dxh
                    m1 = m1 + jnp.sum(dxh, axis=-1, keepdims=True)
                    m2 = m2 + jnp.sum(dxh * xh, axis=-1, keepdims=True)
                m1, m2 = m1 / C, m2 / C
                for a0, a1 in _strips(C, sw):
                    xh = (uc_ref[rs, a0:a1] - mu) * rstd
                    duc = rstd * (dxh_buf[rs, a0:a1] - m1 - xh * m2)
                    db_out[:, a0:a1] += _colsum(duc)
                    cv, sc = cv_ref[rs, a0:a1], _sigmoid(cg_ref[rs, a0:a1])
                    bank_u, bank_d = _shift_bank(cv * sc), _shift_bank(duc)
                    du = jnp.zeros((seg, sw), F32)
                    for k in range(K):
                        o = k - K // 2
                        du = du + w_ref[k:k + 1, a0:a1] * _shifted_rows(bank_d, -o, seg)
                        dw_out[k:k + 1, a0:a1] += _colsum(duc * _shifted_rows(bank_u, o, seg))
                    dz_out[2, rs, a0:a1] = (du * sc).astype(MXU_DTYPE)
                    dz_out[3, rs, a0:a1] = (du * cv * sc * (1.0 - sc)).astype(MXU_DTYPE)

        @pl.when(i == 0)
        def _():
            conv(tms)

        @pl.when(i != 0)
        def _():
            conv(GRID_W)

    tile = pl.BlockSpec((tms, C), lambda i: (i, 0))
    vec = pl.BlockSpec((1, C), lambda i: (0, 0))
    kc = pl.BlockSpec((K, C), lambda i: (0, 0))
    zs = [pl.BlockSpec((None, tms, C), functools.partial(lambda i, s: (s, i, 0), s=s)) for s in (1, 2, 3)]
    vshape = jax.ShapeDtypeStruct((1, C), F32)
    return pl.pallas_call(
        body, name=name, grid=(nT,),
        in_specs=[pl.BlockSpec((tms, 2 * C), lambda i: (i, 0)), tile, tile, *zs, tile, kc, vec, vec],
        out_specs=(tile, pl.BlockSpec((S, tms, C), lambda i: (0, i, 0)), kc, vec, vec, vec),
        out_shape=(jax.ShapeDtypeStruct((T, C), F32), jax.ShapeDtypeStruct((S, T, C), MXU_DTYPE),
                   jax.ShapeDtypeStruct((K, C), F32), vshape, vshape, vshape),
        scratch_shapes=[pltpu.VMEM((tms, C), F32)], compiler_params=_params(("arbitrary",)),
    )(dymix, hf, hb, z, z, z, uc, w31, clg, clb)


def _loss_head(xhat, g, b, target, *, tc, tms, name):
    T, D = xhat.shape
    nT = T // tms
    nc = tc // tms

    def body(x_ref, g_ref, b_ref, t_ref, dy_out, loss_out):
        i = pl.program_id(0)

        @pl.when(i == 0)
        def _():
            loss_out[...] = jnp.zeros_like(loss_out)

        err = jnp.where(i < nc, 0.0, x_ref[...] * g_ref[...] + b_ref[...] - t_ref[...])
        dy_out[...] = err / D
        loss_out[...] += 0.5 * jnp.sum(jnp.sum(err * err, axis=-1, keepdims=True) / D)

    vec = pl.BlockSpec((1, D), lambda i: (0, 0))
    tile = pl.BlockSpec((tms, D), lambda i: (i, 0))
    return pl.pallas_call(
        body, name=name, grid=(nT,),
        in_specs=[tile, vec, vec, pl.BlockSpec((tms, D), lambda i: (jnp.maximum(i - nc, 0), 0))],
        out_specs=(tile, pl.BlockSpec((8, 128), lambda i: (0, 0))),
        out_shape=(jax.ShapeDtypeStruct((T, D), F32), jax.ShapeDtypeStruct((8, 128), F32)),
        compiler_params=_params(("arbitrary",)),
    )(xhat, g, b, target)


def _ada_fwd(s16, w_ada, b_cols, *, name):
    L, D, Na = w_ada.shape
    tn = _pick(Na, 512)

    def body(s_ref, w_ref, b_ref, o_ref):
        o_ref[...] = jnp.dot(s_ref[...], w_ref[...].astype(MXU_DTYPE), preferred_element_type=F32) + b_ref[...]

    return pl.pallas_call(
        body, name=name, grid=(L, Na // tn),
        in_specs=[pl.BlockSpec((16, D), lambda l, n: (0, 0)), pl.BlockSpec((None, D, tn), lambda l, n: (l, 0, n)),
                  pl.BlockSpec((None, 1, tn), lambda l, n: (l, 0, n))],
        out_specs=pl.BlockSpec((None, 16, tn), lambda l, n: (l, 0, n)),
        out_shape=jax.ShapeDtypeStruct((L, 16, Na), F32), compiler_params=_params(("arbitrary", "arbitrary")),
    )(s16, w_ada, b_cols.reshape(L, 1, Na))


def _ada_bwd(s16, dm16, w_ada, *, name):
    L, D, Na = w_ada.shape
    tn = _pick(Na, 512)

    def body(s_ref, dm_ref, w_ref, dw_out, ds_out):
        @pl.when((pl.program_id(0) == 0) & (pl.program_id(1) == 0))
        def _():
            ds_out[...] = jnp.zeros_like(ds_out)

        dm = dm_ref[...].astype(MXU_DTYPE)
        dw_out[...] = lax.dot_general(s_ref[...], dm, (((0,), (0,)), ((), ())), preferred_element_type=F32)
        ds_out[...] += lax.dot_general(dm, w_ref[...].astype(MXU_DTYPE), (((1,), (1,)), ((), ())), preferred_element_type=F32)

    return pl.pallas_call(
        body, name=name, grid=(L, Na // tn),
        in_specs=[pl.BlockSpec((16, D), lambda l, n: (0, 0)), pl.BlockSpec((None, 16, tn), lambda l, n: (l, 0, n)),
                  pl.BlockSpec((None, D, tn), lambda l, n: (l, 0, n))],
        out_specs=(pl.BlockSpec((None, D, tn), lambda l, n: (l, 0, n)), pl.BlockSpec((16, D), lambda l, n: (0, 0))),
        out_shape=(jax.ShapeDtypeStruct((L, D, Na), F32), jax.ShapeDtypeStruct((16, D), F32)),
        compiler_params=_params(("arbitrary", "arbitrary")),
    )(s16, dm16, w_ada)


def _silu_rows(cvec, *, name):
    R, D = cvec.shape

    def body(c_ref, s_out, ds_out):
        c = c_ref[...]
        sg = _sigmoid(c)
        s_out[...] = (c * sg).astype(MXU_DTYPE)
        ds_out[...] = sg * (1.0 + c * (1.0 - sg))

    return pl.pallas_call(
        body, name=name, out_shape=(jax.ShapeDtypeStruct((R, D), MXU_DTYPE), jax.ShapeDtypeStruct((R, D), F32)),
    )(cvec)


def _sum_leading(v, order, *, name, scale_by=None):
    N, R, C = v.shape
    tr = _pick(R, max(8, ADAM_BLOCK_ELEMS // C), 8)

    def body(v_ref, *rest):
        acc = v_ref[order[0]]
        for j in order[1:]:
            acc = acc + v_ref[j]
        if scale_by is not None:
            acc = acc * rest[0][...]
        rest[-1][...] = acc

    in_specs = [pl.BlockSpec((N, tr, C), lambda i: (0, i, 0))]
    args = [v]
    if scale_by is not None:
        in_specs.append(pl.BlockSpec((tr, C), lambda i: (i, 0)))
        args.append(scale_by)
    return pl.pallas_call(
        body, name=name, grid=(R // tr,), in_specs=in_specs, out_specs=pl.BlockSpec((tr, C), lambda i: (i, 0)),
        out_shape=jax.ShapeDtypeStruct((R, C), F32), compiler_params=_params(("arbitrary",)),
    )(*args)


def _sum_scattered(g, land, chip_arr, l, n_layers, prev, *, name):
    S, R, C = g.shape
    tr = _pick(R, max(8, ADAM_BLOCK_ELEMS // C), 8)

    def body(ch_ref, g_ref, a_ref, b_ref, c_ref, *rest):
        rest[-1][...] = ((g_ref[...] + a_ref[...]) + b_ref[...]) + c_ref[...]

    def slot(d):
        return pl.BlockSpec((None, tr, C), lambda i, ch: ((ch[0] + d) % S, i, 0))

    in_specs = [slot(0), slot(1), slot(2), slot(3)]
    args, aliases = [chip_arr, g, land, land, land], {}
    if prev is not None:
        in_specs.append(pl.BlockSpec(memory_space=pl.ANY))
        args.append(prev)
        aliases = {5: 0}
    grid_spec = pltpu.PrefetchScalarGridSpec(
        num_scalar_prefetch=1, grid=(R // tr,), in_specs=in_specs,
        out_specs=pl.BlockSpec((None, tr, C), lambda i, ch: (l, i, 0)))
    return pl.pallas_call(
        body, name=name, grid_spec=grid_spec, out_shape=jax.ShapeDtypeStruct((n_layers, R, C), F32),
        input_output_aliases=aliases, compiler_params=_params(("arbitrary",)),
    )(*args)


def _adamw(w, g, m, v, g_other=None, *, name):
    shape = w.shape
    C = shape[-1]
    R = w.size // C
    two = g_other is not None
    ins = [t.reshape(R, C) for t in ((w, g, m, v, g_other) if two else (w, g, m, v))]
    tr = _pick(R, max(8, ADAM_BLOCK_ELEMS // C), 8)

    def body(w_ref, g_ref, m_ref, v_ref, *rest):
        d_out, m_out, v_out = rest[-3:]
        gg = g_ref[...]
        if two:
            gg = gg + rest[0][...]
            rest[1][...] = gg
        mn = ADAM_B1 * m_ref[...] + (1.0 - ADAM_B1) * gg
        vn = ADAM_B2 * v_ref[...] + (1.0 - ADAM_B2) * (gg * gg)
        m_hat = mn / (1.0 - ADAM_B1 ** ADAM_STEP)
        v_hat = vn / (1.0 - ADAM_B2 ** ADAM_STEP)
        d_out[...] = -ADAM_LR * (m_hat / (jnp.sqrt(v_hat) + ADAM_EPS) + ADAM_WD * w_ref[...])
        m_out[...] = mn
        v_out[...] = vn

    blk = pl.BlockSpec((tr, C), lambda i: (i, 0))
    shp = jax.ShapeDtypeStruct((R, C), F32)
    n_out = 4 if two else 3
    outs = pl.pallas_call(
        body, name=name, grid=(R // tr,), in_specs=[blk] * len(ins), out_specs=(blk,) * n_out, out_shape=(shp,) * n_out,
        compiler_params=_params(("arbitrary",)),
    )(*ins)
    return tuple(o.reshape(shape) for o in outs)


def _place():
    x, y, c = lax.axis_index("x"), lax.axis_index("y"), lax.axis_index("c")
    return x, y, c, [(1 - x, y), (x, 1 - y), (1 - x, 1 - y)]


def _allgather_small(v, *, name, after=None):
    m_per, n = v.shape

    def body(x_ref, *rest):
        out_ref, send_sems, recv_sems, local_sem = rest[-4:]
        x, y, c, chips = _place()
        me, sibling = (x, y, c), (x, y, 1 - c)

        def rows(px, py, pc):
            return out_ref.at[pl.ds((4 * px + 2 * py + pc) * m_per, m_per), :]

        def copy(k, block, to, src=None):
            return pltpu.make_async_remote_copy(
                src_ref=rows(*block) if src is None else src, dst_ref=rows(*block),
                send_sem=send_sems.at[k], recv_sem=recv_sems.at[k], device_id=to, device_id_type=MESH)

        mine = pltpu.make_async_copy(x_ref, rows(*me), local_sem)
        mine.start()
        first = [copy(0, me, sibling, src=x_ref)]
        first += [copy(1 + j, me, (*chip, c), src=x_ref) for j, chip in enumerate(chips)]
        for cp in first:
            cp.start()
        passed = [copy(4 + j, (*chip, c), sibling) for j, chip in enumerate(chips)]
        for j, chip in enumerate(chips):
            copy(1 + j, (*chip, c), me).wait_recv()
            passed[j].start()
        copy(0, sibling, me).wait_recv()
        for j, chip in enumerate(chips):
            copy(4 + j, (*chip, 1 - c), me).wait_recv()
        for cp in first + passed:
            cp.wait_send()
        mine.wait()

    in_specs, args = [pl.BlockSpec(memory_space=pltpu.VMEM)], [v]
    if after is not None:
        in_specs.append(pl.BlockSpec(memory_space=pl.ANY))
        args.append(after)
    return pl.pallas_call(
        body, name=name, out_shape=jax.ShapeDtypeStruct((N_DEV * m_per, n), v.dtype),
        in_specs=in_specs, out_specs=pl.BlockSpec(memory_space=pltpu.VMEM),
        scratch_shapes=[pltpu.SemaphoreType.DMA((7,)), pltpu.SemaphoreType.DMA((7,)), pltpu.SemaphoreType.DMA],
        compiler_params=pltpu.CompilerParams(vmem_limit_bytes=VMEM_LIMIT_BYTES),
    )(*args)


_HBM = pl.BlockSpec(memory_space=pltpu.HBM)
_SEM = pl.BlockSpec(memory_space=pltpu.SEMAPHORE)
_ANY = pl.BlockSpec(memory_space=pl.ANY)
_TOKEN = jax.ShapeDtypeStruct((8, 128), F32)


def _in_hbm(a):
    return pltpu.with_memory_space_constraint(a, pltpu.HBM)


def _place_cast(w, l, chip_arr, *, name):
    L, C = w.shape[0], w.shape[-1]
    R = w.size // (L * C)
    tr = _pick(R, max(8, ADAM_BLOCK_ELEMS // C), 16)

    def body(ch_ref, w_ref, o_ref):
        o_ref[...] = w_ref[...].astype(MXU_DTYPE)

    grid_spec = pltpu.PrefetchScalarGridSpec(
        num_scalar_prefetch=1, grid=(R // tr,),
        in_specs=[pl.BlockSpec((None, tr, C), lambda i, ch: (l, i, 0))],
        out_specs=pl.BlockSpec((None, tr, C), lambda i, ch: (ch[0], i, 0)))
    return pl.pallas_call(
        body, name=name, grid_spec=grid_spec, out_shape=jax.ShapeDtypeStruct((N_CHIPS, R, C), MXU_DTYPE),
        compiler_params=_params(("arbitrary",)),
    )(chip_arr, w.reshape(L, R, C))


def _chip_copies(srcs, lands, sends, recvs, k, j, px, py):
    x, y, c, _ = _place()
    me, peer = 2 * x + y, 2 * px + py
    src = srcs[k].at[peer] if srcs else lands[k].at[me]
    return pltpu.make_async_remote_copy(
        src_ref=src, dst_ref=lands[k].at[me], send_sem=sends[k].at[j], recv_sem=recvs[k].at[j],
        device_id=(px, py, c), device_id_type=MESH), peer


def _exchange_start(srcs, lands, after, *, name):
    ns, n = len(srcs), len(lands)
    arrays = (*srcs, *lands)

    def body(*refs):
        outs = refs[-(2 * n + ns + n + 1):]
        chips = _place()[3]
        for k in range(n):
            for j, (px, py) in enumerate(chips):
                _chip_copies(refs[:ns], refs[ns:ns + n], outs[:n], outs[n:2 * n], k, j, px, py)[0].start()
        outs[-1][...] = jnp.zeros_like(outs[-1])

    sem = pltpu.SemaphoreType.DMA((N_CHIPS - 1,))
    args = [_in_hbm(a) for a in arrays]
    in_specs = [_HBM] * (ns + n)
    if after is not None:
        args.append(after)
        in_specs.append(_ANY)
    outs = pl.pallas_call(
        body, name=name,
        out_shape=(*[sem] * (2 * n), *[pltpu.HBM(a.shape, a.dtype) for a in arrays], _TOKEN),
        in_specs=in_specs, out_specs=(*[_SEM] * (2 * n), *[_HBM] * (ns + n), pl.BlockSpec(memory_space=pltpu.VMEM)),
        input_output_aliases={k: 2 * n + k for k in range(ns + n)},
        compiler_params=pltpu.CompilerParams(has_side_effects=pltpu.SideEffectType.DATAFLOW_SIDE_EFFECTING),
    )(*args)
    return outs[:n], outs[n:2 * n], outs[2 * n:2 * n + ns], outs[2 * n + ns:2 * n + ns + n], outs[-1]


def _exchange_wait(sends, recvs, srcs, lands, after, *, name):
    ns, n = len(srcs), len(lands)

    def body(*refs):
        srcs_r, lands_r = refs[:ns], refs[ns:ns + n]
        sends_r, recvs_r = refs[ns + n:ns + 2 * n], refs[ns + 2 * n:ns + 3 * n]
        chips = _place()[3]
        for k in range(n):
            for j, (px, py) in enumerate(chips):
                cp, peer = _chip_copies(srcs_r, lands_r, sends_r, recvs_r, k, j, px, py)
                cp.wait_send()
                pltpu.make_async_remote_copy(
                    src_ref=lands_r[k].at[peer], dst_ref=lands_r[k].at[peer], send_sem=sends_r[k].at[j],
                    recv_sem=recvs_r[k].at[j], device_id=(px, py, _place()[2]), device_id_type=MESH).wait_recv()

    outs = pl.pallas_call(
        body, name=name, out_shape=[pltpu.HBM(a.shape, a.dtype) for a in (*srcs, *lands)],
        in_specs=[*[_HBM] * (ns + n), *[_SEM] * (2 * n), _ANY], out_specs=[_HBM] * (ns + n),
        input_output_aliases={k: k for k in range(ns + n)},
        compiler_params=pltpu.CompilerParams(has_side_effects=pltpu.SideEffectType.DATAFLOW_SIDE_EFFECTING),
    )(*srcs, *lands, *sends, *recvs, after)
    return outs[:ns], outs[ns:]


def _sibling_copy(srcs, lands, sends, recvs, k):
    x, y, c, _ = _place()
    return pltpu.make_async_remote_copy(src_ref=srcs[k], dst_ref=lands[k], send_sem=sends[k], recv_sem=recvs[k],
                                        device_id=(x, y, 1 - c), device_id_type=MESH)


def _sibling_start(ps, after, *, name):
    n = len(ps)
    lands = [lax.empty(p.shape, p.dtype) for p in ps]

    def body(*refs):
        outs = refs[-(4 * n + 1):]
        for k in range(n):
            _sibling_copy(refs[:n], refs[n:2 * n], outs[:n], outs[n:2 * n], k).start()
        outs[-1][...] = jnp.zeros_like(outs[-1])

    sem = pltpu.SemaphoreType.DMA(())
    outs = pl.pallas_call(
        body, name=name,
        out_shape=(*[sem] * (2 * n), *[pltpu.HBM(a.shape, a.dtype) for a in (*ps, *lands)], _TOKEN),
        in_specs=[*[_HBM] * (2 * n), _ANY],
        out_specs=(*[_SEM] * (2 * n), *[_HBM] * (2 * n), pl.BlockSpec(memory_space=pltpu.VMEM)),
        input_output_aliases={k: 2 * n + k for k in range(2 * n)},
        compiler_params=pltpu.CompilerParams(has_side_effects=pltpu.SideEffectType.DATAFLOW_SIDE_EFFECTING),
    )(*[_in_hbm(a) for a in (*ps, *lands)], after)
    return outs[:n], outs[n:2 * n], outs[2 * n:3 * n], outs[3 * n:4 * n], outs[-1]


def _sibling_wait(sends, recvs, ps, lands, after, *, name):
    n = len(ps)

    def body(*refs):
        for k in range(n):
            cp = _sibling_copy(refs[:n], refs[n:2 * n], refs[2 * n:3 * n], refs[3 * n:4 * n], k)
            cp.wait_send()
            cp.wait_recv()

    outs = pl.pallas_call(
        body, name=name, out_shape=[pltpu.HBM(a.shape, a.dtype) for a in (*ps, *lands)],
        in_specs=[*[_HBM] * (2 * n), *[_SEM] * (2 * n), _ANY], out_specs=[_HBM] * (2 * n),
        input_output_aliases={k: k for k in range(2 * n)},
        compiler_params=pltpu.CompilerParams(has_side_effects=pltpu.SideEffectType.DATAFLOW_SIDE_EFFECTING),
    )(*ps, *lands, *sends, *recvs, after)
    return outs[:n], outs[n:]


def kernel(x, c, ctx, c_ctx, w_ada, b_ada, ln_g, ln_b, ff1_in, ff1_out, ff2_in, ff2_out, w_in, conv4_w, conv4_b, w_rg, b_rg, w_ig, b_ig, lam, conv31_w, conv31_b, cln_g, cln_b, w_out, b_out, loss_target, m_c_ctx, m_w_ada, m_b_ada, m_ln_g, m_ln_b, m_ff1_in, m_ff1_out, m_ff2_in, m_ff2_out, m_w_in, m_conv4_w, m_conv4_b, m_w_rg, m_b_rg, m_w_ig, m_b_ig, m_lam, m_conv31_w, m_conv31_b, m_cln_g, m_cln_b, m_w_out, m_b_out, v_c_ctx, v_w_ada, v_b_ada, v_ln_g, v_ln_b, v_ff1_in, v_ff1_out, v_ff2_in, v_ff2_out, v_w_in, v_conv4_w, v_conv4_b, v_w_rg, v_b_rg, v_w_ig, v_b_ig, v_lam, v_conv31_w, v_conv31_b, v_cln_g, v_cln_b, v_w_out, v_b_out):
    weights = dict(c_ctx=c_ctx, w_ada=w_ada, b_ada=b_ada, ln_g=ln_g, ln_b=ln_b, ff1_in=ff1_in, ff1_out=ff1_out,
                   ff2_in=ff2_in, ff2_out=ff2_out, w_in=w_in, conv4_w=conv4_w, conv4_b=conv4_b, w_rg=w_rg, b_rg=b_rg,
                   w_ig=w_ig, b_ig=b_ig, lam=lam, conv31_w=conv31_w, conv31_b=conv31_b, cln_g=cln_g, cln_b=cln_b,
                   w_out=w_out, b_out=b_out)
    m_in = dict(c_ctx=m_c_ctx, w_ada=m_w_ada, b_ada=m_b_ada, ln_g=m_ln_g, ln_b=m_ln_b, ff1_in=m_ff1_in, ff1_out=m_ff1_out,
                ff2_in=m_ff2_in, ff2_out=m_ff2_out, w_in=m_w_in, conv4_w=m_conv4_w, conv4_b=m_conv4_b, w_rg=m_w_rg,
                b_rg=m_b_rg, w_ig=m_w_ig, b_ig=m_b_ig, lam=m_lam, conv31_w=m_conv31_w, conv31_b=m_conv31_b,
                cln_g=m_cln_g, cln_b=m_cln_b, w_out=m_w_out, b_out=m_b_out)
    v_in = dict(c_ctx=v_c_ctx, w_ada=v_w_ada, b_ada=v_b_ada, ln_g=v_ln_g, ln_b=v_ln_b, ff1_in=v_ff1_in, ff1_out=v_ff1_out,
                ff2_in=v_ff2_in, ff2_out=v_ff2_out, w_in=v_w_in, conv4_w=v_conv4_w, conv4_b=v_conv4_b, w_rg=v_w_rg,
                b_rg=v_b_rg, w_ig=v_w_ig, b_ig=v_b_ig, lam=v_lam, conv31_w=v_conv31_w, conv31_b=v_conv31_b,
                cln_g=v_cln_g, cln_b=v_cln_b, w_out=v_w_out, b_out=v_b_out)
    order = list(weights)

    ax, ay, ac = lax.axis_index("x"), lax.axis_index("y"), lax.axis_index("c")
    chip = 2 * ax + ay
    dev = 4 * ax + 2 * ay + ac
    chip_arr = jnp.reshape(chip, (1,)).astype(jnp.int32)

    L, D, Na = w_ada.shape
    Tl, Tc = x.shape[1], ctx.shape[1]
    T = Tc + Tl
    tms = Tc
    C = conv4_w.shape[2] * N_CHIPS
    nH, hds, hd = w_rg.shape[2], w_rg.shape[3], w_rg.shape[4]
    K31 = conv31_w.shape[1]
    assert L == 2 and Tl % tms == 0 and tms % GRID_W == 0 and tms % 8 == 0 and tms & (tms - 1) == 0
    assert hds * N_CHIPS == hd and nH * hd == C and D == 2 * C and K31 // 2 < CONV_PAD
    alpha = (2 * L) ** 0.25
    n_mod = N_CHIPS * Na // D

    def shard_cols(full, width):
        return lax.dynamic_slice_in_dim(full, chip * width, width, axis=full.ndim - 1)

    c8 = jnp.zeros((8, D), F32).at[0].set(c[0]).at[1].set(c_ctx)
    c_all = _allgather_small(c8, name="gather_cond").reshape(N_DEV, 8, D)
    c16 = jnp.concatenate([c_all[:, 0], c_ctx[None], jnp.zeros((7, D), F32)], axis=0)
    s16, ds16 = _silu_rows(c16, name="silu_cond")
    mod_part = _ada_fwd(s16, w_ada, shard_cols(b_ada, Na), name="ada_fwd")
    mod_all = _allgather_small(mod_part.reshape(L * 16, Na), name="gather_mod").reshape(N_DEV, L, 16, Na)
    mod_full = jnp.transpose(mod_all[0::2], (1, 2, 0, 3)).reshape(L, 16, N_CHIPS * Na)
    mod_rows = jnp.stack([mod_full[:, 8], lax.dynamic_index_in_dim(mod_full, dev, axis=1, keepdims=False)], axis=1)
    mod = mod_rows.reshape(L, 2, n_mod, D)

    def mvec(l, k):
        return mod[l, :, k, :]

    def full_gate(g):
        return jnp.transpose(g, (1, 2, 0, 3, 4)).reshape(2, nH, hd, hd)

    small_sharded = ("ln_g", "ln_b", "conv4_w", "b_rg", "b_ig", "lam", "conv31_w")
    pieces = {n: weights[n].reshape(-1, weights[n].shape[-1]) for n in small_sharded}
    widths = {n: p.shape[1] for n, p in pieces.items()}
    rows_of = {n: p.shape[0] for n, p in pieces.items()}
    wcat = max(widths.values())
    cat = jnp.concatenate([jnp.pad(p, ((0, 0), (0, wcat - p.shape[1]))) for p in pieces.values()], axis=0)
    rpad = -cat.shape[0] % 8
    cat_all = _allgather_small(jnp.pad(cat, ((0, rpad), (0, 0))), name="gather_small").reshape(N_DEV, -1, wcat)
    full_small, r0 = {}, 0
    for n in small_sharded:
        blk = cat_all[0::2, r0:r0 + rows_of[n], :widths[n]]
        full_small[n] = jnp.transpose(blk, (1, 0, 2)).reshape(rows_of[n], N_CHIPS * widths[n])
        r0 += rows_of[n]
    ln_g_f = full_small["ln_g"].reshape(L, 3, 1, D)
    ln_b_f = full_small["ln_b"].reshape(L, 3, 1, D)
    conv4_w_f = full_small["conv4_w"].reshape(L, 4, C)
    b_rg_f = full_small["b_rg"].reshape(L, 2, 1, C)
    b_ig_f = full_small["b_ig"].reshape(L, 2, 1, C)
    lam_f = full_small["lam"].reshape(L, 2, 1, C)
    conv31_w_f = full_small["conv31_w"].reshape(L, K31, C)

    ones, zeros = jnp.ones((1, D), F32), jnp.zeros((1, D), F32)

    big_names = ("ff1_in", "ff1_out", "w_in", "w_rg", "w_ig", "w_out", "ff2_in", "ff2_out")
    inflight, tok = [], mod[0, 0, 0, :1] + cat_all[0, 0, :1]
    for l in range(L):
        lands = [_place_cast(weights[n], l, chip_arr, name=f"place_{n}_{l}") for n in big_names]
        sends, recvs, _, lands_t, tok = _exchange_start([], lands, tok, name=f"gather_start_{l}")
        inflight.append({n: (sends[k], recvs[k], lands_t[k]) for k, n in enumerate(big_names)})
    gw = [{} for _ in range(L)]

    def gathered(l, n, after):
        s, r, land = inflight[l][n]
        land = _exchange_wait([s], [r], [], [land], after, name=f"gather_wait_{n}_{l}")[1][0]
        gw[l][n] = land.reshape(N_CHIPS, *weights[n].shape[1:])
        return gw[l][n]

    s0 = jnp.concatenate([ctx[0], x[0]], axis=0)
    cur = (s0, ones, zeros)
    saved = []
    for l in range(L):
        sv = {"in": cur}
        w = gathered(l, "ff1_in", tok if l == 0 else cur[0])
        h1, gu1, act1 = _in_proj(*cur, mvec(l, 0), mvec(l, 1), w, tc=Tc, swiglu=True, name=f"ffn1_in_{l}")
        xh1, rs1, f1 = _out_proj_ln(act1, gathered(l, "ff1_out", act1), zeros, *cur, mvec(l, 2), 0.5, alpha, tc=Tc, name=f"ffn1_out_{l}")
        sv.update(h1=h1, gu1=gu1, act1=act1, xh1=xh1, rs1=rs1, f1=f1)
        cur1 = (xh1, ln_g_f[l, 0], ln_b_f[l, 0])
        h2, z = _in_proj(*cur1, mvec(l, 3), mvec(l, 4), gathered(l, "w_in", xh1), tc=Tc, swiglu=False, name=f"mix_in_{l}")
        xc = _conv4_fwd(z, conv4_w_f[l], conv4_b[l][None], tms=tms, name=f"conv4_{l}")
        wr_l, wi_l = full_gate(gathered(l, "w_rg", xc)), full_gate(gathered(l, "w_ig", xc))
        sv.update(wr=wr_l, wi=wi_l)
        rec = []
        for d in range(2):
            rec.append(_lru_fwd(xc, wr_l[d], wi_l[d], b_rg_f[l, d], b_ig_f[l, d], lam_f[l, d],
                                rev=bool(d), tms=tms, name=f"lru_{l}_{d}"))
        ymix, uc = _convmod_fwd(rec[0][0], rec[1][0], z, conv31_w_f[l], conv31_b[l][None], cln_g[l][None], cln_b[l][None],
                                tms=tms, name=f"convmod_{l}")
        xh2, rs2, f2 = _out_proj_ln(ymix, gathered(l, "w_out", ymix), b_out[l][None], *cur1, mvec(l, 5), 1.0, alpha, tc=Tc, name=f"mix_out_{l}")
        sv.update(h2=h2, z=z, xc=xc, rec=rec, ymix=ymix, uc=uc, xh2=xh2, rs2=rs2, f2=f2)
        cur2 = (xh2, ln_g_f[l, 1], ln_b_f[l, 1])
        h3, gu3, act3 = _in_proj(*cur2, mvec(l, 6), mvec(l, 7), gathered(l, "ff2_in", xh2), tc=Tc, swiglu=True, name=f"ffn2_in_{l}")
        xh3, rs3, f3 = _out_proj_ln(act3, gathered(l, "ff2_out", act3), zeros, *cur2, mvec(l, 8), 0.5, alpha, tc=Tc, name=f"ffn2_out_{l}")
        sv.update(h3=h3, gu3=gu3, act3=act3, xh3=xh3, rs3=rs3, f3=f3)
        cur = (xh3, ln_g_f[l, 2], ln_b_f[l, 2])
        saved.append(sv)

    ds, loss_blk = _loss_head(*cur, loss_target[0], tc=Tc, tms=tms, name="loss_head")

    reduced = {n: None for n in big_names}
    pending = []

    def start_group(l, names, gs):
        gs = [g.reshape(N_CHIPS, -1, g.shape[-1]) for g in gs]
        lands = [lax.empty(g.shape, F32) for g in gs]
        sends, recvs, srcs_t, lands_t, token = _exchange_start(gs, lands, None, name=f"grad_start_{names[0]}_{l}")
        pending.append((sends, recvs, srcs_t, lands_t, names, l))
        return token

    def finish_group(after):
        sends, recvs, srcs_t, lands_t, names, l = pending.pop(0)
        gs, lands = _exchange_wait(sends, recvs, srcs_t, lands_t, after, name=f"grad_wait_{names[0]}_{l}")
        for k, n in enumerate(names):
            reduced[n] = _sum_scattered(gs[k], lands[k], chip_arr, l, L, reduced[n], name=f"grad_sum_{n}_{l}")

    def finish_older(keep, after):
        while len(pending) > keep:
            finish_group(after)

    dmod = [[None] * n_mod for _ in range(L)]
    d_ln_g = [[None] * 3 for _ in range(L)]
    d_ln_b = [[None] * 3 for _ in range(L)]
    small = {n: [None] * L for n in ("conv4_w", "conv4_b", "conv31_w", "conv31_b", "cln_g", "cln_b", "b_out")}
    gate_w = {n: [[None, None] for _ in range(L)] for n in ("w_rg", "w_ig", "b_rg", "b_ig", "lam")}

    def ffn_bwd(ds, l, k, names, sv_in, sfx, after, before_dx=None):
        sv = saved[l]
        dy, dres, d_ln_g[l][k], d_ln_b[l][k], dmod[l][3 * k + 2], _ = _ln_bwd(
            ds, sv["xh" + sfx], sv["rs" + sfx], ln_g_f[l, k], sv["f" + sfx], mvec(l, 3 * k + 2), 0.5, alpha, after,
            tc=Tc, name=f"ffn{sfx}_ln_bwd_{l}")
        dg = _nt(dy, gw[l][names[1]], sv["gu" + sfx], name=f"ffn{sfx}_dact_{l}")
        g_in = _wgrad(sv["h" + sfx], dg, cols_sharded=True, name=f"ffn{sfx}_wgrad_in_{l}")
        t_in = start_group(l, (names[0],), (g_in,))
        g_out = _wgrad(sv["act" + sfx], dy, cols_sharded=False, name=f"ffn{sfx}_wgrad_out_{l}", after=t_in)
        t_out = start_group(l, (names[1],), (g_out,))
        if before_dx is not None:
            t_out = before_dx(t_out)
        ds_new, dmod[l][3 * k + 1], dmod[l][3 * k] = _dx_modbwd(
            dg, gw[l][names[0]], dres, *sv_in, mvec(l, 3 * k + 1), tc=Tc, name=f"ffn{sfx}_dx_{l}", after=t_out)
        return ds_new, t_out

    def gate_slots(g):
        g = g.reshape(2, nH, N_CHIPS, hds, hd)
        return jnp.transpose(g, (2, 0, 1, 3, 4)).reshape(N_CHIPS, 2 * nH * hds, hd)

    delta, new_m, new_v, grads, swaps = {}, {}, {}, {}, {}

    def start_swap(names, tag, after):
        return (names, tag, *_sibling_start([reduced[n] for n in names], after, name=f"grad_swap_start_{tag}"))

    def finish_weights(swap, after):
        names, tag, sends, recvs, own, lands, _ = swap
        own, others = _sibling_wait(sends, recvs, own, lands, after, name=f"grad_swap_wait_{tag}")
        for n, mine, other in zip(names, own, others):
            shp = weights[n].shape
            grads[n], delta[n], new_m[n], new_v[n] = _adamw(weights[n], mine.reshape(shp), m_in[n], v_in[n],
                                                            other.reshape(shp), name=f"adamw_{n}")

    def swap_early(tok):
        finish_older(2, tok)
        swaps["early"] = start_swap(("ff2_in", "ff2_out", "w_in", "w_out", "w_rg", "w_ig"), "early", tok)
        return swaps["early"][-1]

    token = None
    for l in reversed(range(L)):
        sv = saved[l]
        cur1 = (sv["xh1"], ln_g_f[l, 0], ln_b_f[l, 0])
        cur2 = (sv["xh2"], ln_g_f[l, 1], ln_b_f[l, 1])
        ds, token = ffn_bwd(ds, l, 2, ("ff2_in", "ff2_out"), cur2, "3", token)
        finish_older(2, ds)
        dy, dres, d_ln_g[l][1], d_ln_b[l][1], dmod[l][5], small["b_out"][l] = _ln_bwd(
            ds, sv["xh2"], sv["rs2"], ln_g_f[l, 1], sv["f2"], mvec(l, 5), 1.0, alpha, token, tc=Tc, name=f"mix_ln_bwd_{l}")
        dymix = _nt(dy, gw[l]["w_out"], None, name=f"mix_dy_{l}")
        g_w_out = _wgrad(sv["ymix"], dy, cols_sharded=False, name=f"mix_wgrad_out_{l}")
        dhs, dz, small["conv31_w"][l], small["conv31_b"][l], small["cln_g"][l], small["cln_b"][l] = _convmod_bwd(
            dymix, sv["rec"][0][0], sv["rec"][1][0], sv["z"], sv["uc"], conv31_w_f[l], cln_g[l][None], cln_b[l][None],
            tms=tms, name=f"convmod_bwd_{l}")
        dxc = []
        for d in range(2):
            hd_, rd_, id_ = sv["rec"][d]
            o = _lru_bwd(dhs, hd_, rd_, id_, sv["xc"], sv["wr"][d], sv["wi"][d], lam_f[l, d],
                         rev=bool(d), tms=tms, name=f"lru_bwd_{l}_{d}")
            dxc.append(o[0])
            for n, val in zip(("w_rg", "w_ig", "b_rg", "b_ig", "lam"), o[1:]):
                gate_w[n][l][d] = val
        dz, small["conv4_w"][l], small["conv4_b"][l] = _conv4_bwd(dxc[0], dxc[1], sv["z"], conv4_w_f[l], dz, tms=tms, name=f"conv4_bwd_{l}")
        ds, dmod[l][4], dmod[l][3] = _dx_modbwd(dz, gw[l]["w_in"], dres, *cur1, mvec(l, 4), tc=Tc, name=f"mix_dx_{l}")
        g_w_in = _wgrad(sv["h2"], dz, cols_sharded=True, name=f"mix_wgrad_in_{l}")
        token = start_group(l, ("w_out", "w_in", "w_rg", "w_ig"),
                            (g_w_out, g_w_in, gate_slots(jnp.stack(gate_w["w_rg"][l])), gate_slots(jnp.stack(gate_w["w_ig"][l]))))
        finish_older(1, ds)
        ds, token = ffn_bwd(ds, l, 0, ("ff1_in", "ff1_out"), sv["in"], "1", token, before_dx=None if l else swap_early)
        finish_older(2, ds)

    grad_x = ds[Tc:][None]

    finish_weights(swaps["early"], ds)
    finish_older(0, new_v["ff2_in"])
    swaps["late"] = start_swap(("ff1_in", "ff1_out"), "late", new_v["ff2_in"])


    dmod_arr = jnp.stack([jnp.stack(dmod[l], axis=1) for l in range(L)])
    dm_ctx = dmod_arr[:, 0].reshape(L, n_mod * D)
    dm_lat = dmod_arr[:, 1].reshape(L, n_mod * D)
    summed = {
        "loss": loss_blk[0:1, 0:1],
        "dm_ctx": dm_ctx,
        "ln_g": jnp.stack([jnp.concatenate(d_ln_g[l], axis=0) for l in range(L)]),
        "ln_b": jnp.stack([jnp.concatenate(d_ln_b[l], axis=0) for l in range(L)]),
        "conv4_w": jnp.stack(small["conv4_w"]),
        "conv4_b": jnp.concatenate(small["conv4_b"], axis=0),
        "b_rg": jnp.stack([jnp.concatenate(gate_w["b_rg"][l], axis=0) for l in range(L)]),
        "b_ig": jnp.stack([jnp.concatenate(gate_w["b_ig"][l], axis=0) for l in range(L)]),
        "lam": jnp.stack([jnp.concatenate(gate_w["lam"][l], axis=0) for l in range(L)]),
        "conv31_w": jnp.stack(small["conv31_w"]),
        "conv31_b": jnp.concatenate(small["conv31_b"], axis=0),
        "cln_g": jnp.concatenate(small["cln_g"], axis=0),
        "cln_b": jnp.concatenate(small["cln_b"], axis=0),
        "b_out": jnp.concatenate(small["b_out"], axis=0),
        "dm_lat": dm_lat,
    }
    flat = jnp.concatenate([v.reshape(-1) for v in summed.values()])
    n_flat = flat.shape[0]
    n_rows = -(-n_flat // 128)
    n_rows += -n_rows % 8
    vec = jnp.pad(flat, (0, n_rows * 128 - n_flat)).reshape(n_rows, 128)
    vec_all = _allgather_small(vec, name="gather_small_grads", after=swaps["late"][-1]).reshape(N_DEV, n_rows, 128)
    vec_sum = _sum_leading(vec_all, tuple(range(N_DEV)), name="sum_small_grads").reshape(-1)
    tot, off = {}, 0
    for n, v in summed.items():
        tot[n] = vec_sum[off:off + v.size].reshape(v.shape)
        if n == "dm_lat":
            dm_lat_all = vec_all.reshape(N_DEV, -1)[:, off:off + v.size].reshape(N_DEV, L, n_mod * D)
        off += v.size
    loss = tot["loss"].reshape(())

    dm16 = jnp.concatenate([jnp.transpose(dm_lat_all, (1, 0, 2)), tot["dm_ctx"][:, None], jnp.zeros((L, 7, n_mod * D), F32)], axis=1)
    g_w_ada, ds16_part = _ada_bwd(s16, shard_cols(dm16, Na), w_ada, name="ada_bwd")
    ds_all = _allgather_small(ds16_part[8:16], name="gather_dcond").reshape(N_DEV, 8, D)
    g_c_ctx = _sum_leading(ds_all[:, 0:1], (0, 2, 4, 6), name="sum_dcond", scale_by=ds16[8:9]).reshape(D)
    g_b_ada = _sum_leading(jnp.stack([tot["dm_lat"], tot["dm_ctx"]]), (0, 1), name="sum_b_ada")

    finish_weights(swaps["late"], g_b_ada)
    grads.update(c_ctx=g_c_ctx, w_ada=g_w_ada, b_ada=g_b_ada)
    for n in ("ln_g", "ln_b", "conv4_w", "b_rg", "b_ig", "lam", "conv31_w"):
        grads[n] = shard_cols(tot[n], weights[n].shape[-1])
    for n in ("conv4_b", "conv31_b", "cln_g", "cln_b", "b_out"):
        grads[n] = tot[n]
    for n in order:
        if n not in reduced:
            delta[n], new_m[n], new_v[n] = _adamw(weights[n], grads[n], m_in[n], v_in[n], name=f"adamw_{n}")
    return (loss, grad_x, *[grads[n] for n in order], *[delta[n] for n in order],
            *[new_m[n] for n in order], *[new_v[n] for n in order])
```

```python
import functools

import jax
import jax.numpy as jnp
from jax import lax
from jax.experimental import pallas as pl
from jax.experimental.pallas import tpu as pltpu

F32 = jnp.float32
MXU_DTYPE = jnp.bfloat16
GRID_W = 64
RG_C = 8.0
LN_EPS = 1e-6
ADAM_LR, ADAM_B1, ADAM_B2, ADAM_EPS, ADAM_WD, ADAM_STEP = 0.001, 0.9, 0.999, 1e-08, 0.01, 10
LANES = 128
N_CHIPS = 4
N_DEV = 8
VMEM_LIMIT_BYTES = 56 * 1024 * 1024
TM_IN, TN_IN = 768, 256
TM_OUT, TK_OUT = 528, 1408
ACC_CHUNK = 512
TM_NT, TN_NT, NT_CHUNK = 768, 1408, 256
TM_DX, TK_DX = 528, 1408
TK_WG, TB_WG = 768, 1408
ADAM_BLOCK_ELEMS = 256 * 1024
MESH = pl.DeviceIdType.MESH


def _pick(total, target, mult=128):
    for d in range(min(total, target), 0, -1):
        if total % d == 0 and d % mult == 0:
            return d
    return total


def _params(sem=None):
    kw = dict(vmem_limit_bytes=VMEM_LIMIT_BYTES)
    if sem is not None:
        kw["dimension_semantics"] = sem
    return pltpu.CompilerParams(**kw)


def _sigmoid(x):
    return 1.0 / (1.0 + jnp.exp(-x))


def _gelu(x):
    k = 0.7978845608028654
    t = jnp.tanh(k * (x + 0.044715 * (x * x * x)))
    return 0.5 * x * (1.0 + t)


def _gelu_grad(x):
    k = 0.7978845608028654
    t = jnp.tanh(k * (x + 0.044715 * (x * x * x)))
    return 0.5 * (1.0 + t) + 0.5 * x * (1.0 - t * t) * (k * (1.0 + 3.0 * 0.044715 * x * x))


def _log1p(e):
    u = 1.0 + e
    return jnp.where(u == 1.0, e, jnp.log(u) * (e / jnp.where(u == 1.0, 1.0, u - 1.0)))


def _softplus(y):
    return jnp.maximum(y, 0.0) + _log1p(jnp.exp(-jnp.abs(y)))


def _one_minus_sq(log_a, a):
    x = 2.0 * log_a
    series = -x * (1.0 + x * (0.5 + x * (1.0 / 6.0 + x * (1.0 / 24.0))))
    return jnp.where(x > -0.01, series, 1.0 - a * a)


def _strips(width, sw):
    return [(j * sw, (j + 1) * sw) for j in range(width // sw)]


def _rows(i, tm):
    return i * tm + lax.broadcasted_iota(jnp.int32, (tm, 1), 0)


def _sel(isctx, ref):
    return jnp.where(isctx, ref[0:1, :], ref[1:2, :])


def _colsum(v):
    return jnp.sum(v, axis=0, keepdims=True)


def _in_proj(xhat, g_in, b_in, shift, scale, w, *, tc, swiglu, name):
    T, D = xhat.shape
    S, _, Ns = w.shape
    tm, tn = _pick(T, TM_IN, 8), _pick(Ns, TN_IN)
    nps = Ns // tn
    nI = T // tm
    nN = (S // 2 if swiglu else S) * nps

    def body(x_ref, g_ref, b_ref, sh_ref, sc_ref, w_ref, h_out, *rest):
        i, n = pl.program_id(0), pl.program_id(1)
        h_scr = rest[-1]

        @pl.when(n == 0)
        def _():
            s = x_ref[...] * g_ref[...] + b_ref[...]
            isctx = _rows(i, tm) < tc
            h = (s * (1.0 + _sel(isctx, sc_ref)) + _sel(isctx, sh_ref)).astype(MXU_DTYPE)
            h_scr[...] = h
            h_out[...] = h

        h = h_scr[...]
        if swiglu:
            gu_out, act_out = rest[0], rest[1]
            gt = jnp.dot(h, w_ref[0], preferred_element_type=F32)
            ut = jnp.dot(h, w_ref[1], preferred_element_type=F32)
            gu_out[0] = gt.astype(MXU_DTYPE)
            gu_out[1] = ut.astype(MXU_DTYPE)
            act_out[...] = ((gt * _sigmoid(gt)) * ut).astype(MXU_DTYPE)
        else:
            rest[0][...] = jnp.dot(h, w_ref[...], preferred_element_type=F32)

    vec = pl.BlockSpec((1, D), lambda i, n: (0, 0))
    vec2 = pl.BlockSpec((2, D), lambda i, n: (0, 0))
    in_specs = [pl.BlockSpec((tm, D), lambda i, n: (i, 0)), vec, vec, vec2, vec2]
    h_shape = jax.ShapeDtypeStruct((T, D), MXU_DTYPE)
    h_spec = pl.BlockSpec((tm, D), lambda i, n: (i, 0))
    if swiglu:
        F = (S // 2) * Ns
        wv = w.reshape(2, S // 2, *w.shape[1:])
        in_specs.append(pl.BlockSpec((2, None, D, tn), lambda i, n: (0, n // nps, 0, n % nps)))
        out_shape = (h_shape, jax.ShapeDtypeStruct((2, T, F), MXU_DTYPE), jax.ShapeDtypeStruct((T, F), MXU_DTYPE))
        out_specs = (h_spec, pl.BlockSpec((2, tm, tn), lambda i, n: (0, i, n)), pl.BlockSpec((tm, tn), lambda i, n: (i, n)))
    else:
        wv = w
        in_specs.append(pl.BlockSpec((None, D, tn), lambda i, n: (n // nps, 0, n % nps)))
        out_shape = (h_shape, jax.ShapeDtypeStruct((S, T, Ns), F32))
        out_specs = (h_spec, pl.BlockSpec((None, tm, tn), lambda i, n: (n // nps, i, n % nps)))
    return pl.pallas_call(
        body, name=name, grid=(nI, nN), in_specs=in_specs, out_specs=out_specs, out_shape=out_shape,
        scratch_shapes=[pltpu.VMEM((tm, D), MXU_DTYPE)], compiler_params=_params(("arbitrary", "arbitrary")),
    )(xhat, g_in, b_in, shift, scale, wv)


def _out_proj_ln(a, w, bias, xin, g_in, b_in, gvec, gscale, alpha, *, tc, name):
    T, K = a.shape
    S, Ks, D = w.shape
    tm, tk = _pick(T, TM_OUT, 8), _pick(Ks, TK_OUT)
    kps = Ks // tk
    nK = S * kps

    def body(a_ref, w_ref, bias_ref, xin_ref, gi_ref, bi_ref, gv_ref, xh_out, rstd_out, f_out, acc):
        i, k = pl.program_id(0), pl.program_id(1)

        @pl.when(k == 0)
        def _():
            acc[...] = jnp.zeros_like(acc)

        av = a_ref[...]
        for c0 in range(0, D, ACC_CHUNK):
            c1 = min(c0 + ACC_CHUNK, D)
            acc[:, c0:c1] += jnp.dot(av, w_ref[:, c0:c1], preferred_element_type=F32)

        @pl.when(k == nK - 1)
        def _():
            f = acc[...] + bias_ref[...]
            f_out[...] = f.astype(MXU_DTYPE)
            s = xin_ref[...] * gi_ref[...] + bi_ref[...]
            gv = gscale * _sel(_rows(i, tm) < tc, gv_ref)
            r = alpha * s + gv * f
            mu = jnp.mean(r, axis=-1, keepdims=True)
            d = r - mu
            var = jnp.mean(d * d, axis=-1, keepdims=True)
            rstd = lax.rsqrt(var + LN_EPS)
            xh_out[...] = d * rstd
            rstd_out[...] = rstd

    vec = pl.BlockSpec((1, D), lambda i, k: (0, 0))
    row = pl.BlockSpec((tm, D), lambda i, k: (i, 0))
    return pl.pallas_call(
        body, name=name, grid=(T // tm, nK),
        in_specs=[pl.BlockSpec((tm, tk), lambda i, k: (i, k)),
                  pl.BlockSpec((None, tk, D), lambda i, k: (k // kps, k % kps, 0)),
                  vec, row, vec, vec, pl.BlockSpec((2, D), lambda i, k: (0, 0))],
        out_specs=(row, pl.BlockSpec((tm, 1), lambda i, k: (i, 0)), row),
        out_shape=(jax.ShapeDtypeStruct((T, D), F32), jax.ShapeDtypeStruct((T, 1), F32),
                   jax.ShapeDtypeStruct((T, D), MXU_DTYPE)),
        scratch_shapes=[pltpu.VMEM((tm, D), F32)], compiler_params=_params(("arbitrary", "arbitrary")),
    )(a, w, bias, xin, g_in, b_in, gvec)


def _ln_bwd(ds, xhat, rstd, g_ln, f, gvec, gscale, alpha, after, *, tc, name):
    T, D = ds.shape
    tm = _pick(T, TM_OUT, 8)

    def body(ds_ref, xh_ref, rs_ref, gl_ref, f_ref, gv_ref, *rest):
        dy_out, dres_out, dgl_out, dbl_out, dgv_out, dbias_out = rest[-6:]
        i = pl.program_id(0)

        @pl.when(i == 0)
        def _():
            dgl_out[...] = jnp.zeros_like(dgl_out)
            dbl_out[...] = jnp.zeros_like(dbl_out)
            dgv_out[...] = jnp.zeros_like(dgv_out)
            dbias_out[...] = jnp.zeros_like(dbias_out)

        dsv, xh = ds_ref[...], xh_ref[...]
        dgl_out[...] += _colsum(dsv * xh)
        dbl_out[...] += _colsum(dsv)
        dxh = dsv * gl_ref[...]
        m1 = jnp.mean(dxh, axis=-1, keepdims=True)
        m2 = jnp.mean(dxh * xh, axis=-1, keepdims=True)
        dr = rs_ref[...] * (dxh - m1 - xh * m2)
        dres_out[...] = alpha * dr
        isctx = _rows(i, tm) < tc
        dyv = (gscale * _sel(isctx, gv_ref)) * dr
        dy_out[...] = dyv.astype(MXU_DTYPE)
        dbias_out[...] += _colsum(dyv)
        p = gscale * (dr * f_ref[...].astype(F32))
        dgv_out[0:1, :] += _colsum(jnp.where(isctx, p, 0.0))
        dgv_out[1:2, :] += _colsum(jnp.where(isctx, 0.0, p))

    vec = pl.BlockSpec((1, D), lambda i: (0, 0))
    vec2 = pl.BlockSpec((2, D), lambda i: (0, 0))
    row = pl.BlockSpec((tm, D), lambda i: (i, 0))
    in_specs = [row, row, pl.BlockSpec((tm, 1), lambda i: (i, 0)), vec, row, vec2]
    args = [ds, xhat, rstd, g_ln, f, gvec]
    if after is not None:
        in_specs.append(pl.BlockSpec(memory_space=pl.ANY))
        args.append(after)
    return pl.pallas_call(
        body, name=name, grid=(T // tm,), in_specs=in_specs,
        out_specs=(row, row, vec, vec, vec2, vec),
        out_shape=(jax.ShapeDtypeStruct((T, D), MXU_DTYPE), jax.ShapeDtypeStruct((T, D), F32),
                   jax.ShapeDtypeStruct((1, D), F32), jax.ShapeDtypeStruct((1, D), F32),
                   jax.ShapeDtypeStruct((2, D), F32), jax.ShapeDtypeStruct((1, D), F32)),
        compiler_params=_params(("arbitrary",)),
    )(*args)


def _nt(dy, w, gate_up, *, name):
    T, D = dy.shape
    S, Ks, _ = w.shape
    tm, tn = _pick(T, TM_NT, 8), _pick(Ks, TN_NT)
    nps = Ks // tn
    nN = S * nps
    F = S * Ks

    def body(dy_ref, w_ref, *rest):
        dyv = dy_ref[...]
        for c0 in range(0, tn, NT_CHUNK):
            c1 = min(c0 + NT_CHUNK, tn)
            d = lax.dot_general(dyv, w_ref[c0:c1, :], (((1,), (1,)), ((), ())), preferred_element_type=F32)
            if gate_up is None:
                rest[0][:, c0:c1] = d
            else:
                gu_ref, dg_out = rest
                g, u = gu_ref[0, :, c0:c1].astype(F32), gu_ref[1, :, c0:c1].astype(F32)
                sg = _sigmoid(g)
                dg_out[0, :, c0:c1] = (d * u * (sg * (1.0 + g * (1.0 - sg)))).astype(MXU_DTYPE)
                dg_out[1, :, c0:c1] = (d * (g * sg)).astype(MXU_DTYPE)

    in_specs = [pl.BlockSpec((tm, D), lambda i, n: (i, 0)),
                pl.BlockSpec((None, tn, D), lambda i, n: (n // nps, n % nps, 0))]
    args = [dy, w]
    if gate_up is None:
        out_shape = jax.ShapeDtypeStruct((T, F), F32)
        out_specs = pl.BlockSpec((tm, tn), lambda i, n: (i, n))
    else:
        Ns = 2 * F // S
        q = Ns // tn
        in_specs.append(pl.BlockSpec((2, tm, tn), lambda i, n: (0, i, n)))
        args.append(gate_up)
        out_shape = jax.ShapeDtypeStruct((2, S // 2, T, Ns), MXU_DTYPE)
        out_specs = pl.BlockSpec((2, None, tm, tn), lambda i, n: (0, n // q, i, n % q))
    out = pl.pallas_call(
        body, name=name, grid=(T // tm, nN), in_specs=in_specs, out_specs=out_specs, out_shape=out_shape,
        compiler_params=_params(("arbitrary", "arbitrary")),
    )(*args)
    return out if gate_up is None else out.reshape(S, T, out.shape[-1])


def _dx_modbwd(dg, w, dres, xhat_in, g_in, b_in, scale, *, tc, name, after=None):
    S, T, Ns = dg.shape
    D = w.shape[1]
    tm, tk = _pick(T, TM_DX, 8), _pick(Ns, TK_DX)
    kps = Ns // tk
    nK = S * kps

    def body(dg_ref, w_ref, dres_ref, xin_ref, gi_ref, bi_ref, sc_ref, *rest):
        ds_out, dsc_out, dsh_out, acc = rest[-4:]
        i, k = pl.program_id(0), pl.program_id(1)

        @pl.when((i == 0) & (k == 0))
        def _():
            dsc_out[...] = jnp.zeros_like(dsc_out)
            dsh_out[...] = jnp.zeros_like(dsh_out)

        @pl.when(k == 0)
        def _():
            acc[...] = jnp.zeros_like(acc)

        dgv = dg_ref[...]
        for c0 in range(0, D, ACC_CHUNK):
            c1 = min(c0 + ACC_CHUNK, D)
            acc[:, c0:c1] += lax.dot_general(dgv, w_ref[c0:c1, :], (((1,), (1,)), ((), ())), preferred_element_type=F32)

        @pl.when(k == nK - 1)
        def _():
            dh = acc[...]
            isctx = _rows(i, tm) < tc
            ds_out[...] = dres_ref[...] + dh * (1.0 + _sel(isctx, sc_ref))
            s = xin_ref[...] * gi_ref[...] + bi_ref[...]
            p = dh * s
            dsc_out[0:1, :] += _colsum(jnp.where(isctx, p, 0.0))
            dsc_out[1:2, :] += _colsum(jnp.where(isctx, 0.0, p))
            dsh_out[0:1, :] += _colsum(jnp.where(isctx, dh, 0.0))
            dsh_out[1:2, :] += _colsum(jnp.where(isctx, 0.0, dh))

    vec = pl.BlockSpec((1, D), lambda i, k: (0, 0))
    vec2 = pl.BlockSpec((2, D), lambda i, k: (0, 0))
    row = pl.BlockSpec((tm, D), lambda i, k: (i, 0))
    in_specs = [pl.BlockSpec((None, tm, tk), lambda i, k: (k // kps, i, k % kps)),
                pl.BlockSpec((None, D, tk), lambda i, k: (k // kps, 0, k % kps)),
                row, row, vec, vec, vec2]
    args = [dg, w, dres, xhat_in, g_in, b_in, scale]
    if after is not None:
        in_specs.append(pl.BlockSpec(memory_space=pl.ANY))
        args.append(after)
    return pl.pallas_call(
        body, name=name, grid=(T // tm, nK), in_specs=in_specs,
        out_specs=(row, vec2, vec2),
        out_shape=(jax.ShapeDtypeStruct((T, D), F32), jax.ShapeDtypeStruct((2, D), F32), jax.ShapeDtypeStruct((2, D), F32)),
        scratch_shapes=[pltpu.VMEM((tm, D), F32)], compiler_params=_params(("arbitrary", "arbitrary")),
    )(*args)


def _wgrad(a, b, *, cols_sharded, name, after=None):
    T = a.shape[0]
    tk = _pick(T, TK_WG, 8)
    if cols_sharded:
        S, _, Ns = b.shape
        D = a.shape[1]
        tb = _pick(Ns, TB_WG)
        grid = (S, Ns // tb, T // tk)
        in_specs = [pl.BlockSpec((tk, D), lambda s, j, k: (k, 0)), pl.BlockSpec((None, tk, tb), lambda s, j, k: (s, k, j))]
        out_shape = jax.ShapeDtypeStruct((S, D, Ns), F32)
        out_specs = pl.BlockSpec((None, D, tb), lambda s, j, k: (s, 0, j))
    else:
        D = b.shape[1]
        S = N_CHIPS
        Ks = a.shape[1] // S
        ta = _pick(Ks, TB_WG)
        q = Ks // ta
        grid = (S, q, T // tk)
        in_specs = [pl.BlockSpec((tk, ta), lambda s, j, k: (k, s * q + j)), pl.BlockSpec((tk, D), lambda s, j, k: (k, 0))]
        out_shape = jax.ShapeDtypeStruct((S, Ks, D), F32)
        out_specs = pl.BlockSpec((None, ta, D), lambda s, j, k: (s, j, 0))

    def body(a_ref, b_ref, *rest):
        o_ref = rest[-1]

        @pl.when(pl.program_id(2) == 0)
        def _():
            o_ref[...] = jnp.zeros_like(o_ref)

        o_ref[...] += lax.dot_general(a_ref[...], b_ref[...], (((0,), (0,)), ((), ())), preferred_element_type=F32)

    args = [a, b]
    if after is not None:
        in_specs.append(pl.BlockSpec(memory_space=pl.ANY))
        args.append(after)
    return pl.pallas_call(
        body, name=name, grid=grid, in_specs=in_specs, out_specs=out_specs, out_shape=out_shape,
        compiler_params=_params(("arbitrary", "arbitrary", "arbitrary")),
    )(*args)


def _halo_specs(tms, T, C, slot=None):
    r8 = tms // 8
    last8 = T // 8 - 1
    if slot is None:
        return (pl.BlockSpec((8, C), lambda i: (jnp.maximum(i * r8 - 1, 0), 0)),
                pl.BlockSpec((tms, C), lambda i: (i, 0)),
                pl.BlockSpec((8, C), lambda i: (jnp.minimum((i + 1) * r8, last8), 0)))
    return (pl.BlockSpec((None, 8, C), lambda i: (slot, jnp.maximum(i * r8 - 1, 0), 0)),
            pl.BlockSpec((None, tms, C), lambda i: (slot, i, 0)),
            pl.BlockSpec((None, 8, C), lambda i: (slot, jnp.minimum((i + 1) * r8, last8), 0)))


def _extended(prev_ref, cur, next_ref, i, n_tiles):
    first = (i == 0) | (i == 1)
    last = (i == 0) | (i == n_tiles - 1)
    pv = jnp.where(first, 0.0, prev_ref[...])
    nx = jnp.where(last, 0.0, next_ref[...])
    return jnp.concatenate([pv, cur, nx], axis=0)


def _shifted(ext, o, tms):
    n = tms + 16
    return pltpu.roll(ext, (-o) % n, 0)[8:8 + tms]


def _conv4_fwd(z, w4, b4, *, tms, name):
    S, T, C = z.shape
    nT = T // tms

    def body(p_ref, c_ref, n_ref, w_ref, b_ref, o_ref):
        i = pl.program_id(0)
        ext = _extended(p_ref, c_ref[...], n_ref, i, nT)
        acc = jnp.zeros((tms, C), F32) + b_ref[...]
        for k in range(4):
            acc = acc + w_ref[k:k + 1, :] * _shifted(ext, k - 2, tms)
        o_ref[...] = acc

    return pl.pallas_call(
        body, name=name, grid=(nT,),
        in_specs=[*_halo_specs(tms, T, C, 0), pl.BlockSpec((4, C), lambda i: (0, 0)), pl.BlockSpec((1, C), lambda i: (0, 0))],
        out_specs=pl.BlockSpec((tms, C), lambda i: (i, 0)), out_shape=jax.ShapeDtypeStruct((T, C), F32),
        compiler_params=_params(("arbitrary",)),
    )(z, z, z, w4, b4)


def _conv4_bwd(dxa, dxb, z, w4, dz, *, tms, name):
    S, T, C = z.shape
    nT = T // tms

    def body(pa, ca, na, pb, cb, nb, px, cx, nx, w_ref, dz_in, dz_out, dw_out, db_out):
        i = pl.program_id(0)

        @pl.when(i == 0)
        def _():
            dw_out[...] = jnp.zeros_like(dw_out)
            db_out[...] = jnp.zeros_like(db_out)

        dcur = ca[...] + cb[...]
        first = (i == 0) | (i == 1)
        last = (i == 0) | (i == nT - 1)
        dext = jnp.concatenate([jnp.where(first, 0.0, pa[...] + pb[...]), dcur, jnp.where(last, 0.0, na[...] + nb[...])], axis=0)
        xext = _extended(px, cx[...], nx, i, nT)
        acc = jnp.zeros((tms, C), F32)
        for k in range(4):
            acc = acc + w_ref[k:k + 1, :] * _shifted(dext, 2 - k, tms)
            dw_out[k:k + 1, :] += _colsum(dcur * _shifted(xext, k - 2, tms))
        db_out[...] += _colsum(dcur)
        dz_out[...] = acc.astype(MXU_DTYPE)

    h = _halo_specs(tms, T, C)
    return pl.pallas_call(
        body, name=name, grid=(nT,),
        in_specs=[*h, *h, *_halo_specs(tms, T, C, 0), pl.BlockSpec((4, C), lambda i: (0, 0)), pl.BlockSpec(memory_space=pl.ANY)],
        out_specs=(pl.BlockSpec((None, tms, C), lambda i: (0, i, 0)), pl.BlockSpec((4, C), lambda i: (0, 0)),
                   pl.BlockSpec((1, C), lambda i: (0, 0))),
        out_shape=(jax.ShapeDtypeStruct(dz.shape, dz.dtype), jax.ShapeDtypeStruct((4, C), F32), jax.ShapeDtypeStruct((1, C), F32)),
        input_output_aliases={10: 0}, compiler_params=_params(("arbitrary",)),
    )(dxa, dxa, dxa, dxb, dxb, dxb, z, z, z, w4, dz)


def _scan_rows(a, b, h_in, rev, n):
    sub = lax.broadcasted_iota(jnp.int32, (n, 1), 0) & 7
    for sft in (1, 2, 4):
        if rev:
            a_sh, b_sh, valid = pltpu.roll(a, n - sft, 0), pltpu.roll(b, n - sft, 0), sub < 8 - sft
        else:
            a_sh, b_sh, valid = pltpu.roll(a, sft, 0), pltpu.roll(b, sft, 0), sub >= sft
        b = a * jnp.where(valid, b_sh, 0.0) + b
        a = a * jnp.where(valid, a_sh, 1.0)
    out, c = [None] * (n // 8), h_in
    for g in (reversed(range(n // 8)) if rev else range(n // 8)):
        out[g] = a[8 * g:8 * g + 8] * c + b[8 * g:8 * g + 8]
        c = out[g][0:1] if rev else out[g][7:8]
    return jnp.concatenate(out, axis=0)


def _scan_order(i, rev, nT):
    if not rev:
        return i
    return jnp.where(i == 0, 0, nT - i)


def _lru_fwd(xc, wr, wi, br, bi, lam, *, rev, tms, name):
    T, C = xc.shape
    nH, hd, _ = wr.shape
    nT = T // tms
    sw = min(hd, LANES)

    def body(xc_ref, wr_ref, wi_ref, br_ref, bi_ref, lam_ref, h_out, r_out, i_out, carry):
        @pl.when(pl.program_id(0) == 0)
        def _():
            carry[...] = jnp.zeros_like(carry)

        for hx in range(nH):
            xh = xc_ref[:, hx * hd:(hx + 1) * hd]
            xb = xh.astype(MXU_DTYPE)
            pre_r = jnp.dot(xb, wr_ref[hx], preferred_element_type=F32)
            pre_i = jnp.dot(xb, wi_ref[hx], preferred_element_type=F32)
            for a0, a1 in _strips(hd, sw):
                sl = slice(hx * hd + a0, hx * hd + a1)
                x = xh[:, a0:a1]
                r = _sigmoid(pre_r[:, a0:a1] + br_ref[:, sl])
                ig = _sigmoid(pre_i[:, a0:a1] + bi_ref[:, sl])
                log_a = (-RG_C * r) * _softplus(-lam_ref[:, sl])
                a = jnp.exp(log_a)
                b = jnp.sqrt(_one_minus_sq(log_a, a)) * (ig * x)
                h = _scan_rows(a, b, carry[:, sl], rev, tms)
                carry[:, sl] = h[0:1, :] if rev else h[tms - 1:tms, :]
                h_out[:, sl] = h
                r_out[:, sl] = r
                i_out[:, sl] = ig

    tile = pl.BlockSpec((tms, C), lambda i: (_scan_order(i, rev, nT), 0))
    wspec = pl.BlockSpec((nH, hd, hd), lambda i: (0, 0, 0))
    vec = pl.BlockSpec((1, C), lambda i: (0, 0))
    shp = jax.ShapeDtypeStruct((T, C), F32)
    return pl.pallas_call(
        body, name=name, grid=(nT,), in_specs=[tile, wspec, wspec, vec, vec, vec], out_specs=(tile, tile, tile),
        out_shape=(shp, shp, shp), scratch_shapes=[pltpu.VMEM((1, C), F32)], compiler_params=_params(("arbitrary",)),
    )(xc, wr, wi, br, bi, lam)


def _lru_bwd(dh, h, r, ig, xc, wr, wi, lam, *, rev, tms, name):
    T, C = xc.shape
    nH, hd, _ = wr.shape
    nT = T // tms
    r8 = tms // 8
    sw = min(hd, LANES)

    def tile_of(ip):
        return _scan_order(nT - 1 - ip, rev, nT)

    def halo_of(ip):
        i = nT - 1 - ip
        if not rev:
            return jnp.maximum(i * r8 - 1, 0)
        return jnp.where(i <= 1, 0, (nT - i + 1) * r8)

    def body(dh_ref, h_ref, hh_ref, r_ref, i_ref, xc_ref, wr_ref, wi_ref, lam_ref,
             dxc_out, dwr_out, dwi_out, dbr_out, dbi_out, dlam_out, ucarry):
        ip = pl.program_id(0)
        pos = nT - 1 - ip

        @pl.when(ip == 0)
        def _():
            ucarry[...] = jnp.zeros_like(ucarry)
            for o in (dwr_out, dwi_out, dbr_out, dbi_out, dlam_out):
                o[...] = jnp.zeros_like(o)

        rows = lax.broadcasted_iota(jnp.int32, (tms, 1), 0)
        nt = (((1,), (1,)), ((), ()))
        tn = (((0,), (0,)), ((), ()))
        for hx in range(nH):
            dpr_parts, dpi_parts, direct_parts = [], [], []
            for a0, a1 in _strips(hd, sw):
                sl = slice(hx * hd + a0, hx * hd + a1)
                x, rr, ii, hh, dhv = xc_ref[:, sl], r_ref[:, sl], i_ref[:, sl], h_ref[:, sl], dh_ref[:, sl]
                sp = _softplus(-lam_ref[:, sl])
                log_a = (-RG_C * rr) * sp
                a = jnp.exp(log_a)
                s = jnp.sqrt(_one_minus_sq(log_a, a))
                u_in = ucarry[:, sl]
                u = _scan_rows(a, a * dhv, u_in, not rev, tms)
                if rev:
                    u_next = jnp.where(rows == 0, u_in, pltpu.roll(u, 1, 0))
                    ucarry[:, sl] = u[tms - 1:tms, :]
                    h_halo = jnp.where(pos == 0, 0.0, hh_ref[0:1, sl])
                    h_prev = jnp.where(rows == tms - 1, h_halo, pltpu.roll(hh, tms - 1, 0))
                else:
                    u_next = jnp.where(rows == tms - 1, u_in, pltpu.roll(u, tms - 1, 0))
                    ucarry[:, sl] = u[0:1, :]
                    h_halo = jnp.where(pos == 0, 0.0, hh_ref[7:8, sl])
                    h_prev = jnp.where(rows == 0, h_halo, pltpu.roll(hh, 1, 0))
                g = dhv + u_next
                dgated = g * s
                dlog_a = (g * h_prev) * a - (g * (ii * x)) * (a * a) / s
                dlam_out[:, sl] += _colsum(dlog_a * (-RG_C * rr)) * (-_sigmoid(-lam_ref[:, sl]))
                dpr = (dlog_a * (-RG_C * sp)) * rr * (1.0 - rr)
                dpi = (dgated * x) * ii * (1.0 - ii)
                dbr_out[:, sl] += _colsum(dpr)
                dbi_out[:, sl] += _colsum(dpi)
                dpr_parts.append(dpr.astype(MXU_DTYPE))
                dpi_parts.append(dpi.astype(MXU_DTYPE))
                direct_parts.append(dgated * ii)
            cat = (lambda p: p[0] if len(p) == 1 else jnp.concatenate(p, axis=1))
            dprh, dpih = cat(dpr_parts), cat(dpi_parts)
            xh = xc_ref[:, hx * hd:(hx + 1) * hd].astype(MXU_DTYPE)
            dxc_out[:, hx * hd:(hx + 1) * hd] = (cat(direct_parts)
                                                + lax.dot_general(dprh, wr_ref[hx], nt, preferred_element_type=F32)
                                                + lax.dot_general(dpih, wi_ref[hx], nt, preferred_element_type=F32))
            dwr_out[hx] += lax.dot_general(xh, dprh, tn, preferred_element_type=F32)
            dwi_out[hx] += lax.dot_general(xh, dpih, tn, preferred_element_type=F32)

    tile = pl.BlockSpec((tms, C), lambda ip: (tile_of(ip), 0))
    wspec = pl.BlockSpec((nH, hd, hd), lambda ip: (0, 0, 0))
    vec = pl.BlockSpec((1, C), lambda ip: (0, 0))
    wshape = jax.ShapeDtypeStruct((nH, hd, hd), F32)
    vshape = jax.ShapeDtypeStruct((1, C), F32)
    return pl.pallas_call(
        body, name=name, grid=(nT,),
        in_specs=[tile, tile, pl.BlockSpec((8, C), lambda ip: (halo_of(ip), 0)), tile, tile, tile, wspec, wspec, vec],
        out_specs=(tile, wspec, wspec, vec, vec, vec),
        out_shape=(jax.ShapeDtypeStruct((T, C), F32), wshape, wshape, vshape, vshape, vshape),
        scratch_shapes=[pltpu.VMEM((1, C), F32)], compiler_params=_params(("arbitrary",)),
    )(dh, h, h, r, ig, xc, wr, wi, lam)


CONV_PAD = 16


def _shift_bank(u):
    pad = jnp.zeros((CONV_PAD, u.shape[1]), u.dtype)
    ext = jnp.concatenate([pad, u, pad], axis=0)
    return [ext] + [pltpu.roll(ext, r, 0) for r in range(1, 8)]


def _shifted_rows(bank, o, rows):
    a = -((-o) // 8)
    start = CONV_PAD + 8 * a
    return bank[8 * a - o][start:start + rows]


def _convmod_fwd(hf, hb, z, w31, b31, clg, clb, *, tms, name):
    S, T, C = z.shape
    K = w31.shape[0]
    nT = T // tms

    sw = min(C, LANES)
    G = GRID_W

    def body(hf_ref, hb_ref, gr_ref, cv_ref, cg_ref, w_ref, b_ref, g_ref, bb_ref, y_out, uc_out):
        i = pl.program_id(0)
        for a0, a1 in _strips(C, sw):
            y_out[:, a0:a1] = ((hf_ref[:, a0:a1] + hb_ref[:, a0:a1]) * _gelu(gr_ref[:, a0:a1])).astype(MXU_DTYPE)

        def conv(seg):
            for r0 in range(0, tms, seg):
                rs = slice(r0, r0 + seg)
                s1 = jnp.zeros((seg, 1), F32)
                for a0, a1 in _strips(C, sw):
                    bank = _shift_bank(cv_ref[rs, a0:a1] * _sigmoid(cg_ref[rs, a0:a1]))
                    acc = jnp.zeros((seg, sw), F32) + b_ref[:, a0:a1]
                    for k in range(K):
                        acc = acc + w_ref[k:k + 1, a0:a1] * _shifted_rows(bank, k - K // 2, seg)
                    uc_out[rs, a0:a1] = acc
                    s1 = s1 + jnp.sum(acc, axis=-1, keepdims=True)
                mu = s1 / C
                s2 = jnp.zeros((seg, 1), F32)
                for a0, a1 in _strips(C, sw):
                    d = uc_out[rs, a0:a1] - mu
                    s2 = s2 + jnp.sum(d * d, axis=-1, keepdims=True)
                rstd = lax.rsqrt(s2 / C + LN_EPS)
                for a0, a1 in _strips(C, sw):
                    yl = (uc_out[rs, a0:a1] - mu) * rstd * g_ref[:, a0:a1] + bb_ref[:, a0:a1]
                    y_out[rs, C + a0:C + a1] = (yl * _sigmoid(yl)).astype(MXU_DTYPE)

        @pl.when(i == 0)
        def _():
            conv(tms)

        @pl.when(i != 0)
        def _():
            conv(G)

    tile = pl.BlockSpec((tms, C), lambda i: (i, 0))
    vec = pl.BlockSpec((1, C), lambda i: (0, 0))
    zs = [pl.BlockSpec((None, tms, C), functools.partial(lambda i, s: (s, i, 0), s=s)) for s in (1, 2, 3)]
    return pl.pallas_call(
        body, name=name, grid=(nT,),
        in_specs=[tile, tile, *zs, pl.BlockSpec((K, C), lambda i: (0, 0)), vec, vec, vec],
        out_specs=(pl.BlockSpec((tms, 2 * C), lambda i: (i, 0)), tile),
        out_shape=(jax.ShapeDtypeStruct((T, 2 * C), MXU_DTYPE), jax.ShapeDtypeStruct((T, C), F32)),
        compiler_params=_params(("arbitrary",)),
    )(hf, hb, z, z, z, w31, b31, clg, clb)


def _convmod_bwd(dymix, hf, hb, z, uc, w31, clg, clb, *, tms, name):
    S, T, C = z.shape
    K = w31.shape[0]
    nT = T // tms

    sw = min(C, LANES)

    def body(dy_ref, hf_ref, hb_ref, gr_ref, cv_ref, cg_ref, uc_ref, w_ref, g_ref, bb_ref,
             dhs_out, dz_out, dw_out, db_out, dg_out, dbb_out, dxh_buf):
        i = pl.program_id(0)

        @pl.when(i == 0)
        def _():
            for o in (dw_out, db_out, dg_out, dbb_out):
                o[...] = jnp.zeros_like(o)

        for a0, a1 in _strips(C, sw):
            dyr, gr = dy_ref[:, a0:a1], gr_ref[:, a0:a1]
            dhs_out[:, a0:a1] = dyr * _gelu(gr)
            dz_out[0, :, a0:a1] = jnp.zeros((tms, sw), MXU_DTYPE)
            dz_out[1, :, a0:a1] = (dyr * (hf_ref[:, a0:a1] + hb_ref[:, a0:a1]) * _gelu_grad(gr)).astype(MXU_DTYPE)

        def conv(seg):
            for r0 in range(0, tms, seg):
                rs = slice(r0, r0 + seg)
                s1 = jnp.zeros((seg, 1), F32)
                for a0, a1 in _strips(C, sw):
                    s1 = s1 + jnp.sum(uc_ref[rs, a0:a1], axis=-1, keepdims=True)
                mu = s1 / C
                s2 = jnp.zeros((seg, 1), F32)
                for a0, a1 in _strips(C, sw):
                    d = uc_ref[rs, a0:a1] - mu
                    s2 = s2 + jnp.sum(d * d, axis=-1, keepdims=True)
                rstd = lax.rsqrt(s2 / C + LN_EPS)
                m1 = jnp.zeros((seg, 1), F32)
                m2 = jnp.zeros((seg, 1), F32)
                for a0, a1 in _strips(C, sw):
                    xh = (uc_ref[rs, a0:a1] - mu) * rstd
                    yl = xh * g_ref[:, a0:a1] + bb_ref[:, a0:a1]
                    sg = _sigmoid(yl)
                    dyl = dy_ref[rs, C + a0:C + a1] * (sg * (1.0 + yl * (1.0 - sg)))
                    dg_out[:, a0:a1] += _colsum(dyl * xh)
                    dbb_out[:, a0:a1] += _colsum(dyl)
                    dxh = dyl * g_ref[:, a0:a1]
                    dxh_buf[rs, a0:a1] = dxh
                    m1 = m1 + jnp.sum(dxh, axis=-1, keepdims=True)
                    m2 = m2 + jnp.sum(dxh * xh, axis=-1, keepdims=True)
                m1, m2 = m1 / C, m2 / C
                for a0, a1 in _strips(C, sw):
                    xh = (uc_ref[rs, a0:a1] - mu) * rstd
                    duc = rstd * (dxh_buf[rs, a0:a1] - m1 - xh * m2)
                    db_out[:, a0:a1] += _colsum(duc)
                    cv, sc = cv_ref[rs, a0:a1], _sigmoid(cg_ref[rs, a0:a1])
                    bank_u, bank_d = _shift_bank(cv * sc), _shift_bank(duc)
                    du = jnp.zeros((seg, sw), F32)
                    for k in range(K):
                        o = k - K // 2
                        du = du + w_ref[k:k + 1, a0:a1] * _shifted_rows(bank_d, -o, seg)
                        dw_out[k:k + 1, a0:a1] += _colsum(duc * _shifted_rows(bank_u, o, seg))
                    dz_out[2, rs, a0:a1] = (du * sc).astype(MXU_DTYPE)
                    dz_out[3, rs, a0:a1] = (du * cv * sc * (1.0 - sc)).astype(MXU_DTYPE)

        @pl.when(i == 0)
        def _():
            conv(tms)

        @pl.when(i != 0)
        def _():
            conv(GRID_W)

    tile = pl.BlockSpec((tms, C), lambda i: (i, 0))
    vec = pl.BlockSpec((1, C), lambda i: (0, 0))
    kc = pl.BlockSpec((K, C), lambda i: (0, 0))
    zs = [pl.BlockSpec((None, tms, C), functools.partial(lambda i, s: (s, i, 0), s=s)) for s in (1, 2, 3)]
    vshape = jax.ShapeDtypeStruct((1, C), F32)
    return pl.pallas_call(
        body, name=name, grid=(nT,),
        in_specs=[pl.BlockSpec((tms, 2 * C), lambda i: (i, 0)), tile, tile, *zs, tile, kc, vec, vec],
        out_specs=(tile, pl.BlockSpec((S, tms, C), lambda i: (0, i, 0)), kc, vec, vec, vec),
        out_shape=(jax.ShapeDtypeStruct((T, C), F32), jax.ShapeDtypeStruct((S, T, C), MXU_DTYPE),
                   jax.ShapeDtypeStruct((K, C), F32), vshape, vshape, vshape),
        scratch_shapes=[pltpu.VMEM((tms, C), F32)], compiler_params=_params(("arbitrary",)),
    )(dymix, hf, hb, z, z, z, uc, w31, clg, clb)


def _loss_head(xhat, g, b, target, *, tc, tms, name):
    T, D = xhat.shape
    nT = T // tms
    nc = tc // tms

    def body(x_ref, g_ref, b_ref, t_ref, dy_out, loss_out):
        i = pl.program_id(0)

        @pl.when(i == 0)
        def _():
            loss_out[...] = jnp.zeros_like(loss_out)

        err = jnp.where(i < nc, 0.0, x_ref[...] * g_ref[...] + b_ref[...] - t_ref[...])
        dy_out[...] = err / D
        loss_out[...] += 0.5 * jnp.sum(jnp.sum(err * err, axis=-1, keepdims=True) / D)

    vec = pl.BlockSpec((1, D), lambda i: (0, 0))
    tile = pl.BlockSpec((tms, D), lambda i: (i, 0))
    return pl.pallas_call(
        body, name=name, grid=(nT,),
        in_specs=[tile, vec, vec, pl.BlockSpec((tms, D), lambda i: (jnp.maximum(i - nc, 0), 0))],
        out_specs=(tile, pl.BlockSpec((8, 128), lambda i: (0, 0))),
        out_shape=(jax.ShapeDtypeStruct((T, D), F32), jax.ShapeDtypeStruct((8, 128), F32)),
        compiler_params=_params(("arbitrary",)),
    )(xhat, g, b, target)


def _ada_fwd(s16, w_ada, b_cols, *, name):
    L, D, Na = w_ada.shape
    tn = _pick(Na, 512)

    def body(s_ref, w_ref, b_ref, o_ref):
        o_ref[...] = jnp.dot(s_ref[...], w_ref[...].astype(MXU_DTYPE), preferred_element_type=F32) + b_ref[...]

    return pl.pallas_call(
        body, name=name, grid=(L, Na // tn),
        in_specs=[pl.BlockSpec((16, D), lambda l, n: (0, 0)), pl.BlockSpec((None, D, tn), lambda l, n: (l, 0, n)),
                  pl.BlockSpec((None, 1, tn), lambda l, n: (l, 0, n))],
        out_specs=pl.BlockSpec((None, 16, tn), lambda l, n: (l, 0, n)),
        out_shape=jax.ShapeDtypeStruct((L, 16, Na), F32), compiler_params=_params(("arbitrary", "arbitrary")),
    )(s16, w_ada, b_cols.reshape(L, 1, Na))


def _ada_bwd(s16, dm16, w_ada, *, name):
    L, D, Na = w_ada.shape
    tn = _pick(Na, 512)

    def body(s_ref, dm_ref, w_ref, dw_out, ds_out):
        @pl.when((pl.program_id(0) == 0) & (pl.program_id(1) == 0))
        def _():
            ds_out[...] = jnp.zeros_like(ds_out)

        dm = dm_ref[...].astype(MXU_DTYPE)
        dw_out[...] = lax.dot_general(s_ref[...], dm, (((0,), (0,)), ((), ())), preferred_element_type=F32)
        ds_out[...] += lax.dot_general(dm, w_ref[...].astype(MXU_DTYPE), (((1,), (1,)), ((), ())), preferred_element_type=F32)

    return pl.pallas_call(
        body, name=name, grid=(L, Na // tn),
        in_specs=[pl.BlockSpec((16, D), lambda l, n: (0, 0)), pl.BlockSpec((None, 16, tn), lambda l, n: (l, 0, n)),
                  pl.BlockSpec((None, D, tn), lambda l, n: (l, 0, n))],
        out_specs=(pl.BlockSpec((None, D, tn), lambda l, n: (l, 0, n)), pl.BlockSpec((16, D), lambda l, n: (0, 0))),
        out_shape=(jax.ShapeDtypeStruct((L, D, Na), F32), jax.ShapeDtypeStruct((16, D), F32)),
        compiler_params=_params(("arbitrary", "arbitrary")),
    )(s16, dm16, w_ada)


def _silu_rows(cvec, *, name):
    R, D = cvec.shape

    def body(c_ref, s_out, ds_out):
        c = c_ref[...]
        sg = _sigmoid(c)
        s_out[...] = (c * sg).astype(MXU_DTYPE)
        ds_out[...] = sg * (1.0 + c * (1.0 - sg))

    return pl.pallas_call(
        body, name=name, out_shape=(jax.ShapeDtypeStruct((R, D), MXU_DTYPE), jax.ShapeDtypeStruct((R, D), F32)),
    )(cvec)


def _sum_leading(v, order, *, name, scale_by=None):
    N, R, C = v.shape
    tr = _pick(R, max(8, ADAM_BLOCK_ELEMS // C), 8)

    def body(v_ref, *rest):
        acc = v_ref[order[0]]
        for j in order[1:]:
            acc = acc + v_ref[j]
        if scale_by is not None:
            acc = acc * rest[0][...]
        rest[-1][...] = acc

    in_specs = [pl.BlockSpec((N, tr, C), lambda i: (0, i, 0))]
    args = [v]
    if scale_by is not None:
        in_specs.append(pl.BlockSpec((tr, C), lambda i: (i, 0)))
        args.append(scale_by)
    return pl.pallas_call(
        body, name=name, grid=(R // tr,), in_specs=in_specs, out_specs=pl.BlockSpec((tr, C), lambda i: (i, 0)),
        out_shape=jax.ShapeDtypeStruct((R, C), F32), compiler_params=_params(("arbitrary",)),
    )(*args)


def _sum_scattered(g, land, chip_arr, l, n_layers, prev, *, name):
    S, R, C = g.shape
    tr = _pick(R, max(8, ADAM_BLOCK_ELEMS // C), 8)

    def body(ch_ref, g_ref, a_ref, b_ref, c_ref, *rest):
        rest[-1][...] = ((g_ref[...] + a_ref[...]) + b_ref[...]) + c_ref[...]

    def slot(d):
        return pl.BlockSpec((None, tr, C), lambda i, ch: ((ch[0] + d) % S, i, 0))

    in_specs = [slot(0), slot(1), slot(2), slot(3)]
    args, aliases = [chip_arr, g, land, land, land], {}
    if prev is not None:
        in_specs.append(pl.BlockSpec(memory_space=pl.ANY))
        args.append(prev)
        aliases = {5: 0}
    grid_spec = pltpu.PrefetchScalarGridSpec(
        num_scalar_prefetch=1, grid=(R // tr,), in_specs=in_specs,
        out_specs=pl.BlockSpec((None, tr, C), lambda i, ch: (l, i, 0)))
    return pl.pallas_call(
        body, name=name, grid_spec=grid_spec, out_shape=jax.ShapeDtypeStruct((n_layers, R, C), F32),
        input_output_aliases=aliases, compiler_params=_params(("arbitrary",)),
    )(*args)


def _adamw(w, g, m, v, g_other=None, *, name):
    shape = w.shape
    C = shape[-1]
    R = w.size // C
    two = g_other is not None
    ins = [t.reshape(R, C) for t in ((w, g, m, v, g_other) if two else (w, g, m, v))]
    tr = _pick(R, max(8, ADAM_BLOCK_ELEMS // C), 8)

    def body(w_ref, g_ref, m_ref, v_ref, *rest):
        d_out, m_out, v_out = rest[-3:]
        gg = g_ref[...]
        if two:
            gg = gg + rest[0][...]
            rest[1][...] = gg
        mn = ADAM_B1 * m_ref[...] + (1.0 - ADAM_B1) * gg
        vn = ADAM_B2 * v_ref[...] + (1.0 - ADAM_B2) * (gg * gg)
        m_hat = mn / (1.0 - ADAM_B1 ** ADAM_STEP)
        v_hat = vn / (1.0 - ADAM_B2 ** ADAM_STEP)
        d_out[...] = -ADAM_LR * (m_hat / (jnp.sqrt(v_hat) + ADAM_EPS) + ADAM_WD * w_ref[...])
        m_out[...] = mn
        v_out[...] = vn

    blk = pl.BlockSpec((tr, C), lambda i: (i, 0))
    shp = jax.ShapeDtypeStruct((R, C), F32)
    n_out = 4 if two else 3
    outs = pl.pallas_call(
        body, name=name, grid=(R // tr,), in_specs=[blk] * len(ins), out_specs=(blk,) * n_out, out_shape=(shp,) * n_out,
        compiler_params=_params(("arbitrary",)),
    )(*ins)
    return tuple(o.reshape(shape) for o in outs)


def _place():
    x, y, c = lax.axis_index("x"), lax.axis_index("y"), lax.axis_index("c")
    return x, y, c, [(1 - x, y), (x, 1 - y), (1 - x, 1 - y)]


def _allgather_small(v, *, name, after=None):
    m_per, n = v.shape

    def body(x_ref, *rest):
        out_ref, send_sems, recv_sems, local_sem = rest[-4:]
        x, y, c, chips = _place()
        me, sibling = (x, y, c), (x, y, 1 - c)

        def rows(px, py, pc):
            return out_ref.at[pl.ds((4 * px + 2 * py + pc) * m_per, m_per), :]

        def copy(k, block, to, src=None):
            return pltpu.make_async_remote_copy(
                src_ref=rows(*block) if src is None else src, dst_ref=rows(*block),
                send_sem=send_sems.at[k], recv_sem=recv_sems.at[k], device_id=to, device_id_type=MESH)

        mine = pltpu.make_async_copy(x_ref, rows(*me), local_sem)
        mine.start()
        first = [copy(0, me, sibling, src=x_ref)]
        first += [copy(1 + j, me, (*chip, c), src=x_ref) for j, chip in enumerate(chips)]
        for cp in first:
            cp.start()
        passed = [copy(4 + j, (*chip, c), sibling) for j, chip in enumerate(chips)]
        for j, chip in enumerate(chips):
            copy(1 + j, (*chip, c), me).wait_recv()
            passed[j].start()
        copy(0, sibling, me).wait_recv()
        for j, chip in enumerate(chips):
            copy(4 + j, (*chip, 1 - c), me).wait_recv()
        for cp in first + passed:
            cp.wait_send()
        mine.wait()

    in_specs, args = [pl.BlockSpec(memory_space=pltpu.VMEM)], [v]
    if after is not None:
        in_specs.append(pl.BlockSpec(memory_space=pl.ANY))
        args.append(after)
    return pl.pallas_call(
        body, name=name, out_shape=jax.ShapeDtypeStruct((N_DEV * m_per, n), v.dtype),
        in_specs=in_specs, out_specs=pl.BlockSpec(memory_space=pltpu.VMEM),
        scratch_shapes=[pltpu.SemaphoreType.DMA((7,)), pltpu.SemaphoreType.DMA((7,)), pltpu.SemaphoreType.DMA],
        compiler_params=pltpu.CompilerParams(vmem_limit_bytes=VMEM_LIMIT_BYTES),
    )(*args)


_HBM = pl.BlockSpec(memory_space=pltpu.HBM)
_SEM = pl.BlockSpec(memory_space=pltpu.SEMAPHORE)
_ANY = pl.BlockSpec(memory_space=pl.ANY)
_TOKEN = jax.ShapeDtypeStruct((8, 128), F32)


def _in_hbm(a):
    return pltpu.with_memory_space_constraint(a, pltpu.HBM)


def _place_cast(w, l, chip_arr, *, name):
    L, C = w.shape[0], w.shape[-1]
    R = w.size // (L * C)
    tr = _pick(R, max(8, ADAM_BLOCK_ELEMS // C), 16)

    def body(ch_ref, w_ref, o_ref):
        o_ref[...] = w_ref[...].astype(MXU_DTYPE)

    grid_spec = pltpu.PrefetchScalarGridSpec(
        num_scalar_prefetch=1, grid=(R // tr,),
        in_specs=[pl.BlockSpec((None, tr, C), lambda i, ch: (l, i, 0))],
        out_specs=pl.BlockSpec((None, tr, C), lambda i, ch: (ch[0], i, 0)))
    return pl.pallas_call(
        body, name=name, grid_spec=grid_spec, out_shape=jax.ShapeDtypeStruct((N_CHIPS, R, C), MXU_DTYPE),
        compiler_params=_params(("arbitrary",)),
    )(chip_arr, w.reshape(L, R, C))


def _chip_copies(srcs, lands, sends, recvs, k, j, px, py):
    x, y, c, _ = _place()
    me, peer = 2 * x + y, 2 * px + py
    src = srcs[k].at[peer] if srcs else lands[k].at[me]
    return pltpu.make_async_remote_copy(
        src_ref=src, dst_ref=lands[k].at[me], send_sem=sends[k].at[j], recv_sem=recvs[k].at[j],
        device_id=(px, py, c), device_id_type=MESH), peer


def _exchange_start(srcs, lands, after, *, name):
    ns, n = len(srcs), len(lands)
    arrays = (*srcs, *lands)

    def body(*refs):
        outs = refs[-(2 * n + ns + n + 1):]
        chips = _place()[3]
        for k in range(n):
            for j, (px, py) in enumerate(chips):
                _chip_copies(refs[:ns], refs[ns:ns + n], outs[:n], outs[n:2 * n], k, j, px, py)[0].start()
        outs[-1][...] = jnp.zeros_like(outs[-1])

    sem = pltpu.SemaphoreType.DMA((N_CHIPS - 1,))
    args = [_in_hbm(a) for a in arrays]
    in_specs = [_HBM] * (ns + n)
    if after is not None:
        args.append(after)
        in_specs.append(_ANY)
    outs = pl.pallas_call(
        body, name=name,
        out_shape=(*[sem] * (2 * n), *[pltpu.HBM(a.shape, a.dtype) for a in arrays], _TOKEN),
        in_specs=in_specs, out_specs=(*[_SEM] * (2 * n), *[_HBM] * (ns + n), pl.BlockSpec(memory_space=pltpu.VMEM)),
        input_output_aliases={k: 2 * n + k for k in range(ns + n)},
        compiler_params=pltpu.CompilerParams(has_side_effects=pltpu.SideEffectType.DATAFLOW_SIDE_EFFECTING),
    )(*args)
    return outs[:n], outs[n:2 * n], outs[2 * n:2 * n + ns], outs[2 * n + ns:2 * n + ns + n], outs[-1]


def _exchange_wait(sends, recvs, srcs, lands, after, *, name):
    ns, n = len(srcs), len(lands)

    def body(*refs):
        srcs_r, lands_r = refs[:ns], refs[ns:ns + n]
        sends_r, recvs_r = refs[ns + n:ns + 2 * n], refs[ns + 2 * n:ns + 3 * n]
        chips = _place()[3]
        for k in range(n):
            for j, (px, py) in enumerate(chips):
                cp, peer = _chip_copies(srcs_r, lands_r, sends_r, recvs_r, k, j, px, py)
                cp.wait_send()
                pltpu.make_async_remote_copy(
                    src_ref=lands_r[k].at[peer], dst_ref=lands_r[k].at[peer], send_sem=sends_r[k].at[j],
                    recv_sem=recvs_r[k].at[j], device_id=(px, py, _place()[2]), device_id_type=MESH).wait_recv()

    outs = pl.pallas_call(
        body, name=name, out_shape=[pltpu.HBM(a.shape, a.dtype) for a in (*srcs, *lands)],
        in_specs=[*[_HBM] * (ns + n), *[_SEM] * (2 * n), _ANY], out_specs=[_HBM] * (ns + n),
        input_output_aliases={k: k for k in range(ns + n)},
        compiler_params=pltpu.CompilerParams(has_side_effects=pltpu.SideEffectType.DATAFLOW_SIDE_EFFECTING),
    )(*srcs, *lands, *sends, *recvs, after)
    return outs[:ns], outs[ns:]


def _sibling_copy(srcs, lands, sends, recvs, k):
    x, y, c, _ = _place()
    return pltpu.make_async_remote_copy(src_ref=srcs[k], dst_ref=lands[k], send_sem=sends[k], recv_sem=recvs[k],
                                        device_id=(x, y, 1 - c), device_id_type=MESH)


def _sibling_start(ps, after, *, name):
    n = len(ps)
    lands = [lax.empty(p.shape, p.dtype) for p in ps]

    def body(*refs):
        outs = refs[-(4 * n + 1):]
        for k in range(n):
            _sibling_copy(refs[:n], refs[n:2 * n], outs[:n], outs[n:2 * n], k).start()
        outs[-1][...] = jnp.zeros_like(outs[-1])

    sem = pltpu.SemaphoreType.DMA(())
    outs = pl.pallas_call(
        body, name=name,
        out_shape=(*[sem] * (2 * n), *[pltpu.HBM(a.shape, a.dtype) for a in (*ps, *lands)], _TOKEN),
        in_specs=[*[_HBM] * (2 * n), _ANY],
        out_specs=(*[_SEM] * (2 * n), *[_HBM] * (2 * n), pl.BlockSpec(memory_space=pltpu.VMEM)),
        input_output_aliases={k: 2 * n + k for k in range(2 * n)},
        compiler_params=pltpu.CompilerParams(has_side_effects=pltpu.SideEffectType.DATAFLOW_SIDE_EFFECTING),
    )(*[_in_hbm(a) for a in (*ps, *lands)], after)
    return outs[:n], outs[n:2 * n], outs[2 * n:3 * n], outs[3 * n:4 * n], outs[-1]


def _sibling_wait(sends, recvs, ps, lands, after, *, name):
    n = len(ps)

    def body(*refs):
        for k in range(n):
            cp = _sibling_copy(refs[:n], refs[n:2 * n], refs[2 * n:3 * n], refs[3 * n:4 * n], k)
            cp.wait_send()
            cp.wait_recv()

    outs = pl.pallas_call(
        body, name=name, out_shape=[pltpu.HBM(a.shape, a.dtype) for a in (*ps, *lands)],
        in_specs=[*[_HBM] * (2 * n), *[_SEM] * (2 * n), _ANY], out_specs=[_HBM] * (2 * n),
        input_output_aliases={k: k for k in range(2 * n)},
        compiler_params=pltpu.CompilerParams(has_side_effects=pltpu.SideEffectType.DATAFLOW_SIDE_EFFECTING),
    )(*ps, *lands, *sends, *recvs, after)
    return outs[:n], outs[n:]


def kernel(x, c, ctx, c_ctx, w_ada, b_ada, ln_g, ln_b, ff1_in, ff1_out, ff2_in, ff2_out, w_in, conv4_w, conv4_b, w_rg, b_rg, w_ig, b_ig, lam, conv31_w, conv31_b, cln_g, cln_b, w_out, b_out, loss_target, m_c_ctx, m_w_ada, m_b_ada, m_ln_g, m_ln_b, m_ff1_in, m_ff1_out, m_ff2_in, m_ff2_out, m_w_in, m_conv4_w, m_conv4_b, m_w_rg, m_b_rg, m_w_ig, m_b_ig, m_lam, m_conv31_w, m_conv31_b, m_cln_g, m_cln_b, m_w_out, m_b_out, v_c_ctx, v_w_ada, v_b_ada, v_ln_g, v_ln_b, v_ff1_in, v_ff1_out, v_ff2_in, v_ff2_out, v_w_in, v_conv4_w, v_conv4_b, v_w_rg, v_b_rg, v_w_ig, v_b_ig, v_lam, v_conv31_w, v_conv31_b, v_cln_g, v_cln_b, v_w_out, v_b_out):
    weights = dict(c_ctx=c_ctx, w_ada=w_ada, b_ada=b_ada, ln_g=ln_g, ln_b=ln_b, ff1_in=ff1_in, ff1_out=ff1_out,
                   ff2_in=ff2_in, ff2_out=ff2_out, w_in=w_in, conv4_w=conv4_w, conv4_b=conv4_b, w_rg=w_rg, b_rg=b_rg,
                   w_ig=w_ig, b_ig=b_ig, lam=lam, conv31_w=conv31_w, conv31_b=conv31_b, cln_g=cln_g, cln_b=cln_b,
                   w_out=w_out, b_out=b_out)
    m_in = dict(c_ctx=m_c_ctx, w_ada=m_w_ada, b_ada=m_b_ada, ln_g=m_ln_g, ln_b=m_ln_b, ff1_in=m_ff1_in, ff1_out=m_ff1_out,
                ff2_in=m_ff2_in, ff2_out=m_ff2_out, w_in=m_w_in, conv4_w=m_conv4_w, conv4_b=m_conv4_b, w_rg=m_w_rg,
                b_rg=m_b_rg, w_ig=m_w_ig, b_ig=m_b_ig, lam=m_lam, conv31_w=m_conv31_w, conv31_b=m_conv31_b,
                cln_g=m_cln_g, cln_b=m_cln_b, w_out=m_w_out, b_out=m_b_out)
    v_in = dict(c_ctx=v_c_ctx, w_ada=v_w_ada, b_ada=v_b_ada, ln_g=v_ln_g, ln_b=v_ln_b, ff1_in=v_ff1_in, ff1_out=v_ff1_out,
                ff2_in=v_ff2_in, ff2_out=v_ff2_out, w_in=v_w_in, conv4_w=v_conv4_w, conv4_b=v_conv4_b, w_rg=v_w_rg,
                b_rg=v_b_rg, w_ig=v_w_ig, b_ig=v_b_ig, lam=v_lam, conv31_w=v_conv31_w, conv31_b=v_conv31_b,
                cln_g=v_cln_g, cln_b=v_cln_b, w_out=v_w_out, b_out=v_b_out)
    order = list(weights)

    ax, ay, ac = lax.axis_index("x"), lax.axis_index("y"), lax.axis_index("c")
    chip = 2 * ax + ay
    dev = 4 * ax + 2 * ay + ac
    chip_arr = jnp.reshape(chip, (1,)).astype(jnp.int32)

    L, D, Na = w_ada.shape
    Tl, Tc = x.shape[1], ctx.shape[1]
    T = Tc + Tl
    tms = Tc
    C = conv4_w.shape[2] * N_CHIPS
    nH, hds, hd = w_rg.shape[2], w_rg.shape[3], w_rg.shape[4]
    K31 = conv31_w.shape[1]
    assert L == 2 and Tl % tms == 0 and tms % GRID_W == 0 and tms % 8 == 0 and tms & (tms - 1) == 0
    assert hds * N_CHIPS == hd and nH * hd == C and D == 2 * C and K31 // 2 < CONV_PAD
    alpha = (2 * L) ** 0.25
    n_mod = N_CHIPS * Na // D

    def shard_cols(full, width):
        return lax.dynamic_slice_in_dim(full, chip * width, width, axis=full.ndim - 1)

    c8 = jnp.zeros((8, D), F32).at[0].set(c[0]).at[1].set(c_ctx)
    c_all = _allgather_small(c8, name="gather_cond").reshape(N_DEV, 8, D)
    c16 = jnp.concatenate([c_all[:, 0], c_ctx[None], jnp.zeros((7, D), F32)], axis=0)
    s16, ds16 = _silu_rows(c16, name="silu_cond")
    mod_part = _ada_fwd(s16, w_ada, shard_cols(b_ada, Na), name="ada_fwd")
    mod_all = _allgather_small(mod_part.reshape(L * 16, Na), name="gather_mod").reshape(N_DEV, L, 16, Na)
    mod_full = jnp.transpose(mod_all[0::2], (1, 2, 0, 3)).reshape(L, 16, N_CHIPS * Na)
    mod_rows = jnp.stack([mod_full[:, 8], lax.dynamic_index_in_dim(mod_full, dev, axis=1, keepdims=False)], axis=1)
    mod = mod_rows.reshape(L, 2, n_mod, D)

    def mvec(l, k):
        return mod[l, :, k, :]

    def full_gate(g):
        return jnp.transpose(g, (1, 2, 0, 3, 4)).reshape(2, nH, hd, hd)

    small_sharded = ("ln_g", "ln_b", "conv4_w", "b_rg", "b_ig", "lam", "conv31_w")
    pieces = {n: weights[n].reshape(-1, weights[n].shape[-1]) for n in small_sharded}
    widths = {n: p.shape[1] for n, p in pieces.items()}
    rows_of = {n: p.shape[0] for n, p in pieces.items()}
    wcat = max(widths.values())
    cat = jnp.concatenate([jnp.pad(p, ((0, 0), (0, wcat - p.shape[1]))) for p in pieces.values()], axis=0)
    rpad = -cat.shape[0] % 8
    cat_all = _allgather_small(jnp.pad(cat, ((0, rpad), (0, 0))), name="gather_small").reshape(N_DEV, -1, wcat)
    full_small, r0 = {}, 0
    for n in small_sharded:
        blk = cat_all[0::2, r0:r0 + rows_of[n], :widths[n]]
        full_small[n] = jnp.transpose(blk, (1, 0, 2)).reshape(rows_of[n], N_CHIPS * widths[n])
        r0 += rows_of[n]
    ln_g_f = full_small["ln_g"].reshape(L, 3, 1, D)
    ln_b_f = full_small["ln_b"].reshape(L, 3, 1, D)
    conv4_w_f = full_small["conv4_w"].reshape(L, 4, C)
    b_rg_f = full_small["b_rg"].reshape(L, 2, 1, C)
    b_ig_f = full_small["b_ig"].reshape(L, 2, 1, C)
    lam_f = full_small["lam"].reshape(L, 2, 1, C)
    conv31_w_f = full_small["conv31_w"].reshape(L, K31, C)

    ones, zeros = jnp.ones((1, D), F32), jnp.zeros((1, D), F32)

    big_names = ("ff1_in", "ff1_out", "w_in", "w_rg", "w_ig", "w_out", "ff2_in", "ff2_out")
    inflight, tok = [], mod[0, 0, 0, :1] + cat_all[0, 0, :1]
    for l in range(L):
        lands = [_place_cast(weights[n], l, chip_arr, name=f"place_{n}_{l}") for n in big_names]
        sends, recvs, _, lands_t, tok = _exchange_start([], lands, tok, name=f"gather_start_{l}")
        inflight.append({n: (sends[k], recvs[k], lands_t[k]) for k, n in enumerate(big_names)})
    gw = [{} for _ in range(L)]

    def gathered(l, n, after):
        s, r, land = inflight[l][n]
        land = _exchange_wait([s], [r], [], [land], after, name=f"gather_wait_{n}_{l}")[1][0]
        gw[l][n] = land.reshape(N_CHIPS, *weights[n].shape[1:])
        return gw[l][n]

    s0 = jnp.concatenate([ctx[0], x[0]], axis=0)
    cur = (s0, ones, zeros)
    saved = []
    for l in range(L):
        sv = {"in": cur}
        w = gathered(l, "ff1_in", tok if l == 0 else cur[0])
        h1, gu1, act1 = _in_proj(*cur, mvec(l, 0), mvec(l, 1), w, tc=Tc, swiglu=True, name=f"ffn1_in_{l}")
        xh1, rs1, f1 = _out_proj_ln(act1, gathered(l, "ff1_out", act1), zeros, *cur, mvec(l, 2), 0.5, alpha, tc=Tc, name=f"ffn1_out_{l}")
        sv.update(h1=h1, gu1=gu1, act1=act1, xh1=xh1, rs1=rs1, f1=f1)
        cur1 = (xh1, ln_g_f[l, 0], ln_b_f[l, 0])
        h2, z = _in_proj(*cur1, mvec(l, 3), mvec(l, 4), gathered(l, "w_in", xh1), tc=Tc, swiglu=False, name=f"mix_in_{l}")
        xc = _conv4_fwd(z, conv4_w_f[l], conv4_b[l][None], tms=tms, name=f"conv4_{l}")
        wr_l, wi_l = full_gate(gathered(l, "w_rg", xc)), full_gate(gathered(l, "w_ig", xc))
        sv.update(wr=wr_l, wi=wi_l)
        rec = []
        for d in range(2):
            rec.append(_lru_fwd(xc, wr_l[d], wi_l[d], b_rg_f[l, d], b_ig_f[l, d], lam_f[l, d],
                                rev=bool(d), tms=tms, name=f"lru_{l}_{d}"))
        ymix, uc = _convmod_fwd(rec[0][0], rec[1][0], z, conv31_w_f[l], conv31_b[l][None], cln_g[l][None], cln_b[l][None],
                                tms=tms, name=f"convmod_{l}")
        xh2, rs2, f2 = _out_proj_ln(ymix, gathered(l, "w_out", ymix), b_out[l][None], *cur1, mvec(l, 5), 1.0, alpha, tc=Tc, name=f"mix_out_{l}")
        sv.update(h2=h2, z=z, xc=xc, rec=rec, ymix=ymix, uc=uc, xh2=xh2, rs2=rs2, f2=f2)
        cur2 = (xh2, ln_g_f[l, 1], ln_b_f[l, 1])
        h3, gu3, act3 = _in_proj(*cur2, mvec(l, 6), mvec(l, 7), gathered(l, "ff2_in", xh2), tc=Tc, swiglu=True, name=f"ffn2_in_{l}")
        xh3, rs3, f3 = _out_proj_ln(act3, gathered(l, "ff2_out", act3), zeros, *cur2, mvec(l, 8), 0.5, alpha, tc=Tc, name=f"ffn2_out_{l}")
        sv.update(h3=h3, gu3=gu3, act3=act3, xh3=xh3, rs3=rs3, f3=f3)
        cur = (xh3, ln_g_f[l, 2], ln_b_f[l, 2])
        saved.append(sv)

    ds, loss_blk = _loss_head(*cur, loss_target[0], tc=Tc, tms=tms, name="loss_head")

    reduced = {n: None for n in big_names}
    pending = []

    def start_group(l, names, gs):
        gs = [g.reshape(N_CHIPS, -1, g.shape[-1]) for g in gs]
        lands = [lax.empty(g.shape, F32) for g in gs]
        sends, recvs, srcs_t, lands_t, token = _exchange_start(gs, lands, None, name=f"grad_start_{names[0]}_{l}")
        pending.append((sends, recvs, srcs_t, lands_t, names, l))
        return token

    def finish_group(after):
        sends, recvs, srcs_t, lands_t, names, l = pending.pop(0)
        gs, lands = _exchange_wait(sends, recvs, srcs_t, lands_t, after, name=f"grad_wait_{names[0]}_{l}")
        for k, n in enumerate(names):
            reduced[n] = _sum_scattered(gs[k], lands[k], chip_arr, l, L, reduced[n], name=f"grad_sum_{n}_{l}")

    def finish_older(keep, after):
        while len(pending) > keep:
            finish_group(after)

    dmod = [[None] * n_mod for _ in range(L)]
    d_ln_g = [[None] * 3 for _ in range(L)]
    d_ln_b = [[None] * 3 for _ in range(L)]
    small = {n: [None] * L for n in ("conv4_w", "conv4_b", "conv31_w", "conv31_b", "cln_g", "cln_b", "b_out")}
    gate_w = {n: [[None, None] for _ in range(L)] for n in ("w_rg", "w_ig", "b_rg", "b_ig", "lam")}

    def ffn_bwd(ds, l, k, names, sv_in, sfx, after, before_dx=None):
        sv = saved[l]
        dy, dres, d_ln_g[l][k], d_ln_b[l][k], dmod[l][3 * k + 2], _ = _ln_bwd(
            ds, sv["xh" + sfx], sv["rs" + sfx], ln_g_f[l, k], sv["f" + sfx], mvec(l, 3 * k + 2), 0.5, alpha, after,
            tc=Tc, name=f"ffn{sfx}_ln_bwd_{l}")
        dg = _nt(dy, gw[l][names[1]], sv["gu" + sfx], name=f"ffn{sfx}_dact_{l}")
        g_in = _wgrad(sv["h" + sfx], dg, cols_sharded=True, name=f"ffn{sfx}_wgrad_in_{l}")
        t_in = start_group(l, (names[0],), (g_in,))
        g_out = _wgrad(sv["act" + sfx], dy, cols_sharded=False, name=f"ffn{sfx}_wgrad_out_{l}", after=t_in)
        t_out = start_group(l, (names[1],), (g_out,))
        if before_dx is not None:
            t_out = before_dx(t_out)
        ds_new, dmod[l][3 * k + 1], dmod[l][3 * k] = _dx_modbwd(
            dg, gw[l][names[0]], dres, *sv_in, mvec(l, 3 * k + 1), tc=Tc, name=f"ffn{sfx}_dx_{l}", after=t_out)
        return ds_new, t_out

    def gate_slots(g):
        g = g.reshape(2, nH, N_CHIPS, hds, hd)
        return jnp.transpose(g, (2, 0, 1, 3, 4)).reshape(N_CHIPS, 2 * nH * hds, hd)

    delta, new_m, new_v, grads, swaps = {}, {}, {}, {}, {}

    def start_swap(names, tag, after):
        return (names, tag, *_sibling_start([reduced[n] for n in names], after, name=f"grad_swap_start_{tag}"))

    def finish_weights(swap, after):
        names, tag, sends, recvs, own, lands, _ = swap
        own, others = _sibling_wait(sends, recvs, own, lands, after, name=f"grad_swap_wait_{tag}")
        for n, mine, other in zip(names, own, others):
            shp = weights[n].shape
            grads[n], delta[n], new_m[n], new_v[n] = _adamw(weights[n], mine.reshape(shp), m_in[n], v_in[n],
                                                            other.reshape(shp), name=f"adamw_{n}")

    def swap_early(tok):
        finish_older(2, tok)
        swaps["early"] = start_swap(("ff2_in", "ff2_out", "w_in", "w_out", "w_rg", "w_ig"), "early", tok)
        return swaps["early"][-1]

    token = None
    for l in reversed(range(L)):
        sv = saved[l]
        cur1 = (sv["xh1"], ln_g_f[l, 0], ln_b_f[l, 0])
        cur2 = (sv["xh2"], ln_g_f[l, 1], ln_b_f[l, 1])
        ds, token = ffn_bwd(ds, l, 2, ("ff2_in", "ff2_out"), cur2, "3", token)
        finish_older(2, ds)
        dy, dres, d_ln_g[l][1], d_ln_b[l][1], dmod[l][5], small["b_out"][l] = _ln_bwd(
            ds, sv["xh2"], sv["rs2"], ln_g_f[l, 1], sv["f2"], mvec(l, 5), 1.0, alpha, token, tc=Tc, name=f"mix_ln_bwd_{l}")
        dymix = _nt(dy, gw[l]["w_out"], None, name=f"mix_dy_{l}")
        g_w_out = _wgrad(sv["ymix"], dy, cols_sharded=False, name=f"mix_wgrad_out_{l}")
        dhs, dz, small["conv31_w"][l], small["conv31_b"][l], small["cln_g"][l], small["cln_b"][l] = _convmod_bwd(
            dymix, sv["rec"][0][0], sv["rec"][1][0], sv["z"], sv["uc"], conv31_w_f[l], cln_g[l][None], cln_b[l][None],
            tms=tms, name=f"convmod_bwd_{l}")
        dxc = []
        for d in range(2):
            hd_, rd_, id_ = sv["rec"][d]
            o = _lru_bwd(dhs, hd_, rd_, id_, sv["xc"], sv["wr"][d], sv["wi"][d], lam_f[l, d],
                         rev=bool(d), tms=tms, name=f"lru_bwd_{l}_{d}")
            dxc.append(o[0])
            for n, val in zip(("w_rg", "w_ig", "b_rg", "b_ig", "lam"), o[1:]):
                gate_w[n][l][d] = val
        dz, small["conv4_w"][l], small["conv4_b"][l] = _conv4_bwd(dxc[0], dxc[1], sv["z"], conv4_w_f[l], dz, tms=tms, name=f"conv4_bwd_{l}")
        ds, dmod[l][4], dmod[l][3] = _dx_modbwd(dz, gw[l]["w_in"], dres, *cur1, mvec(l, 4), tc=Tc, name=f"mix_dx_{l}")
        g_w_in = _wgrad(sv["h2"], dz, cols_sharded=True, name=f"mix_wgrad_in_{l}")
        token = start_group(l, ("w_out", "w_in", "w_rg", "w_ig"),
                            (g_w_out, g_w_in, gate_slots(jnp.stack(gate_w["w_rg"][l])), gate_slots(jnp.stack(gate_w["w_ig"][l]))))
        finish_older(1, ds)
        ds, token = ffn_bwd(ds, l, 0, ("ff1_in", "ff1_out"), sv["in"], "1", token, before_dx=None if l else swap_early)
        finish_older(2, ds)

    grad_x = ds[Tc:][None]

    finish_weights(swaps["early"], ds)
    finish_older(0, new_v["ff2_in"])
    swaps["late"] = start_swap(("ff1_in", "ff1_out"), "late", new_v["ff2_in"])


    dmod_arr = jnp.stack([jnp.stack(dmod[l], axis=1) for l in range(L)])
    dm_ctx = dmod_arr[:, 0].reshape(L, n_mod * D)
    dm_lat = dmod_arr[:, 1].reshape(L, n_mod * D)
    summed = {
        "loss": loss_blk[0:1, 0:1],
        "dm_ctx": dm_ctx,
        "ln_g": jnp.stack([jnp.concatenate(d_ln_g[l], axis=0) for l in range(L)]),
        "ln_b": jnp.stack([jnp.concatenate(d_ln_b[l], axis=0) for l in range(L)]),
        "conv4_w": jnp.stack(small["conv4_w"]),
        "conv4_b": jnp.concatenate(small["conv4_b"], axis=0),
        "b_rg": jnp.stack([jnp.concatenate(gate_w["b_rg"][l], axis=0) for l in range(L)]),
        "b_ig": jnp.stack([jnp.concatenate(gate_w["b_ig"][l], axis=0) for l in range(L)]),
        "lam": jnp.stack([jnp.concatenate(gate_w["lam"][l], axis=0) for l in range(L)]),
        "conv31_w": jnp.stack(small["conv31_w"]),
        "conv31_b": jnp.concatenate(small["conv31_b"], axis=0),
        "cln_g": jnp.concatenate(small["cln_g"], axis=0),
        "cln_b": jnp.concatenate(small["cln_b"], axis=0),
        "b_out": jnp.concatenate(small["b_out"], axis=0),
        "dm_lat": dm_lat,
    }
    flat = jnp.concatenate([v.reshape(-1) for v in summed.values()])
    n_flat = flat.shape[0]
    n_rows = -(-n_flat // 128)
    n_rows += -n_rows % 8
    vec = jnp.pad(flat, (0, n_rows * 128 - n_flat)).reshape(n_rows, 128)
    vec_all = _allgather_small(vec, name="gather_small_grads", after=swaps["late"][-1]).reshape(N_DEV, n_rows, 128)
    vec_sum = _sum_leading(vec_all, tuple(range(N_DEV)), name="sum_small_grads").reshape(-1)
    tot, off = {}, 0
    for n, v in summed.items():
        tot[n] = vec_sum[off:off + v.size].reshape(v.shape)
        if n == "dm_lat":
            dm_lat_all = vec_all.reshape(N_DEV, -1)[:, off:off + v.size].reshape(N_DEV, L, n_mod * D)
        off += v.size
    loss = tot["loss"].reshape(())

    dm16 = jnp.concatenate([jnp.transpose(dm_lat_all, (1, 0, 2)), tot["dm_ctx"][:, None], jnp.zeros((L, 7, n_mod * D), F32)], axis=1)
    g_w_ada, ds16_part = _ada_bwd(s16, shard_cols(dm16, Na), w_ada, name="ada_bwd")
    ds_all = _allgather_small(ds16_part[8:16], name="gather_dcond").reshape(N_DEV, 8, D)
    g_c_ctx = _sum_leading(ds_all[:, 0:1], (0, 2, 4, 6), name="sum_dcond", scale_by=ds16[8:9]).reshape(D)
    g_b_ada = _sum_leading(jnp.stack([tot["dm_lat"], tot["dm_ctx"]]), (0, 1), name="sum_b_ada")

    finish_weights(swaps["late"], g_b_ada)
    grads.update(c_ctx=g_c_ctx, w_ada=g_w_ada, b_ada=g_b_ada)
    for n in ("ln_g", "ln_b", "conv4_w", "b_rg", "b_ig", "lam", "conv31_w"):
        grads[n] = shard_cols(tot[n], weights[n].shape[-1])
    for n in ("conv4_b", "conv31_b", "cln_g", "cln_b", "b_out"):
        grads[n] = tot[n]
    for n in order:
        if n not in reduced:
            delta[n], new_m[n], new_v[n] = _adamw(weights[n], grads[n], m_in[n], v_in[n], name=f"adamw_{n}")
    return (loss, grad_x, *[grads[n] for n in order], *[delta[n] for n in order],
            *[new_m[n] for n in order], *[new_v[n] for n in order])
```

```python
import functools

import jax
import jax.numpy as jnp
from jax import lax
from jax.experimental import pallas as pl
from jax.experimental.pallas import tpu as pltpu

F32 = jnp.float32
MXU_DTYPE = jnp.bfloat16
GRID_W = 64
RG_C = 8.0
LN_EPS = 1e-6
ADAM_LR, ADAM_B1, ADAM_B2, ADAM_EPS, ADAM_WD, ADAM_STEP = 0.001, 0.9, 0.999, 1e-08, 0.01, 10
LANES = 128
N_CHIPS = 4
N_DEV = 8
VMEM_LIMIT_BYTES = 56 * 1024 * 1024
TM_IN, TN_IN = 1056, 256
TM_OUT, TK_OUT = 528, 1408
ACC_CHUNK = 512
TM_NT, TN_NT, NT_CHUNK = 768, 1408, 256
TM_DX, TK_DX = 528, 1408
TK_WG, TB_WG = 768, 1408
ADAM_BLOCK_ELEMS = 256 * 1024
MESH = pl.DeviceIdType.MESH


def _pick(total, target, mult=128):
    for d in range(min(total, target), 0, -1):
        if total % d == 0 and d % mult == 0:
            return d
    return total


def _params(sem=None):
    kw = dict(vmem_limit_bytes=VMEM_LIMIT_BYTES)
    if sem is not None:
        kw["dimension_semantics"] = sem
    return pltpu.CompilerParams(**kw)


def _sigmoid(x):
    return 1.0 / (1.0 + jnp.exp(-x))


def _gelu(x):
    k = 0.7978845608028654
    t = jnp.tanh(k * (x + 0.044715 * (x * x * x)))
    return 0.5 * x * (1.0 + t)


def _gelu_grad(x):
    k = 0.7978845608028654
    t = jnp.tanh(k * (x + 0.044715 * (x * x * x)))
    return 0.5 * (1.0 + t) + 0.5 * x * (1.0 - t * t) * (k * (1.0 + 3.0 * 0.044715 * x * x))


def _log1p(e):
    u = 1.0 + e
    return jnp.where(u == 1.0, e, jnp.log(u) * (e / jnp.where(u == 1.0, 1.0, u - 1.0)))


def _softplus(y):
    return jnp.maximum(y, 0.0) + _log1p(jnp.exp(-jnp.abs(y)))


def _one_minus_sq(log_a, a):
    x = 2.0 * log_a
    series = -x * (1.0 + x * (0.5 + x * (1.0 / 6.0 + x * (1.0 / 24.0))))
    return jnp.where(x > -0.01, series, 1.0 - a * a)


def _strips(width, sw):
    return [(j * sw, (j + 1) * sw) for j in range(width // sw)]


def _rows(i, tm):
    return i * tm + lax.broadcasted_iota(jnp.int32, (tm, 1), 0)


def _sel(isctx, ref):
    return jnp.where(isctx, ref[0:1, :], ref[1:2, :])


def _colsum(v):
    return jnp.sum(v, axis=0, keepdims=True)


def _in_proj(xhat, g_in, b_in, shift, scale, w, *, tc, swiglu, name):
    T, D = xhat.shape
    S, _, Ns = w.shape
    tm, tn = _pick(T, TM_IN, 8), _pick(Ns, TN_IN)
    nps = Ns // tn
    nI = T // tm
    nN = (S // 2 if swiglu else S) * nps

    def body(x_ref, g_ref, b_ref, sh_ref, sc_ref, w_ref, h_out, *rest):
        i, n = pl.program_id(0), pl.program_id(1)
        h_scr = rest[-1]

        @pl.when(n == 0)
        def _():
            s = x_ref[...] * g_ref[...] + b_ref[...]
            isctx = _rows(i, tm) < tc
            h = (s * (1.0 + _sel(isctx, sc_ref)) + _sel(isctx, sh_ref)).astype(MXU_DTYPE)
            h_scr[...] = h
            h_out[...] = h

        h = h_scr[...]
        if swiglu:
            gu_out, act_out = rest[0], rest[1]
            gt = jnp.dot(h, w_ref[0], preferred_element_type=F32)
            ut = jnp.dot(h, w_ref[1], preferred_element_type=F32)
            gu_out[0] = gt.astype(MXU_DTYPE)
            gu_out[1] = ut.astype(MXU_DTYPE)
            act_out[...] = ((gt * _sigmoid(gt)) * ut).astype(MXU_DTYPE)
        else:
            rest[0][...] = jnp.dot(h, w_ref[...], preferred_element_type=F32)

    vec = pl.BlockSpec((1, D), lambda i, n: (0, 0))
    vec2 = pl.BlockSpec((2, D), lambda i, n: (0, 0))
    in_specs = [pl.BlockSpec((tm, D), lambda i, n: (i, 0)), vec, vec, vec2, vec2]
    h_shape = jax.ShapeDtypeStruct((T, D), MXU_DTYPE)
    h_spec = pl.BlockSpec((tm, D), lambda i, n: (i, 0))
    if swiglu:
        F = (S // 2) * Ns
        wv = w.reshape(2, S // 2, *w.shape[1:])
        in_specs.append(pl.BlockSpec((2, None, D, tn), lambda i, n: (0, n // nps, 0, n % nps)))
        out_shape = (h_shape, jax.ShapeDtypeStruct((2, T, F), MXU_DTYPE), jax.ShapeDtypeStruct((T, F), MXU_DTYPE))
        out_specs = (h_spec, pl.BlockSpec((2, tm, tn), lambda i, n: (0, i, n)), pl.BlockSpec((tm, tn), lambda i, n: (i, n)))
    else:
        wv = w
        in_specs.append(pl.BlockSpec((None, D, tn), lambda i, n: (n // nps, 0, n % nps)))
        out_shape = (h_shape, jax.ShapeDtypeStruct((S, T, Ns), F32))
        out_specs = (h_spec, pl.BlockSpec((None, tm, tn), lambda i, n: (n // nps, i, n % nps)))
    return pl.pallas_call(
        body, name=name, grid=(nI, nN), in_specs=in_specs, out_specs=out_specs, out_shape=out_shape,
        scratch_shapes=[pltpu.VMEM((tm, D), MXU_DTYPE)], compiler_params=_params(("arbitrary", "arbitrary")),
    )(xhat, g_in, b_in, shift, scale, wv)


def _out_proj_ln(a, w, bias, xin, g_in, b_in, gvec, gscale, alpha, *, tc, name):
    T, K = a.shape
    S, Ks, D = w.shape
    tm, tk = _pick(T, TM_OUT, 8), _pick(Ks, TK_OUT)
    kps = Ks // tk
    nK = S * kps

    def body(a_ref, w_ref, bias_ref, xin_ref, gi_ref, bi_ref, gv_ref, xh_out, rstd_out, f_out, acc):
        i, k = pl.program_id(0), pl.program_id(1)

        @pl.when(k == 0)
        def _():
            acc[...] = jnp.zeros_like(acc)

        av = a_ref[...]
        for c0 in range(0, D, ACC_CHUNK):
            c1 = min(c0 + ACC_CHUNK, D)
            acc[:, c0:c1] += jnp.dot(av, w_ref[:, c0:c1], preferred_element_type=F32)

        @pl.when(k == nK - 1)
        def _():
            f = acc[...] + bias_ref[...]
            f_out[...] = f.astype(MXU_DTYPE)
            s = xin_ref[...] * gi_ref[...] + bi_ref[...]
            gv = gscale * _sel(_rows(i, tm) < tc, gv_ref)
            r = alpha * s + gv * f
            mu = jnp.mean(r, axis=-1, keepdims=True)
            d = r - mu
            var = jnp.mean(d * d, axis=-1, keepdims=True)
            rstd = lax.rsqrt(var + LN_EPS)
            xh_out[...] = d * rstd
            rstd_out[...] = rstd

    vec = pl.BlockSpec((1, D), lambda i, k: (0, 0))
    row = pl.BlockSpec((tm, D), lambda i, k: (i, 0))
    return pl.pallas_call(
        body, name=name, grid=(T // tm, nK),
        in_specs=[pl.BlockSpec((tm, tk), lambda i, k: (i, k)),
                  pl.BlockSpec((None, tk, D), lambda i, k: (k // kps, k % kps, 0)),
                  vec, row, vec, vec, pl.BlockSpec((2, D), lambda i, k: (0, 0))],
        out_specs=(row, pl.BlockSpec((tm, 1), lambda i, k: (i, 0)), row),
        out_shape=(jax.ShapeDtypeStruct((T, D), F32), jax.ShapeDtypeStruct((T, 1), F32),
                   jax.ShapeDtypeStruct((T, D), MXU_DTYPE)),
        scratch_shapes=[pltpu.VMEM((tm, D), F32)], compiler_params=_params(("arbitrary", "arbitrary")),
    )(a, w, bias, xin, g_in, b_in, gvec)


def _ln_bwd(ds, xhat, rstd, g_ln, f, gvec, gscale, alpha, after, *, tc, name):
    T, D = ds.shape
    tm = _pick(T, TM_OUT, 8)

    def body(ds_ref, xh_ref, rs_ref, gl_ref, f_ref, gv_ref, *rest):
        dy_out, dres_out, dgl_out, dbl_out, dgv_out, dbias_out = rest[-6:]
        i = pl.program_id(0)

        @pl.when(i == 0)
        def _():
            dgl_out[...] = jnp.zeros_like(dgl_out)
            dbl_out[...] = jnp.zeros_like(dbl_out)
            dgv_out[...] = jnp.zeros_like(dgv_out)
            dbias_out[...] = jnp.zeros_like(dbias_out)

        dsv, xh = ds_ref[...], xh_ref[...]
        dgl_out[...] += _colsum(dsv * xh)
        dbl_out[...] += _colsum(dsv)
        dxh = dsv * gl_ref[...]
        m1 = jnp.mean(dxh, axis=-1, keepdims=True)
        m2 = jnp.mean(dxh * xh, axis=-1, keepdims=True)
        dr = rs_ref[...] * (dxh - m1 - xh * m2)
        dres_out[...] = alpha * dr
        isctx = _rows(i, tm) < tc
        dyv = (gscale * _sel(isctx, gv_ref)) * dr
        dy_out[...] = dyv.astype(MXU_DTYPE)
        dbias_out[...] += _colsum(dyv)
        p = gscale * (dr * f_ref[...].astype(F32))
        dgv_out[0:1, :] += _colsum(jnp.where(isctx, p, 0.0))
        dgv_out[1:2, :] += _colsum(jnp.where(isctx, 0.0, p))

    vec = pl.BlockSpec((1, D), lambda i: (0, 0))
    vec2 = pl.BlockSpec((2, D), lambda i: (0, 0))
    row = pl.BlockSpec((tm, D), lambda i: (i, 0))
    in_specs = [row, row, pl.BlockSpec((tm, 1), lambda i: (i, 0)), vec, row, vec2]
    args = [ds, xhat, rstd, g_ln, f, gvec]
    if after is not None:
        in_specs.append(pl.BlockSpec(memory_space=pl.ANY))
        args.append(after)
    return pl.pallas_call(
        body, name=name, grid=(T // tm,), in_specs=in_specs,
        out_specs=(row, row, vec, vec, vec2, vec),
        out_shape=(jax.ShapeDtypeStruct((T, D), MXU_DTYPE), jax.ShapeDtypeStruct((T, D), F32),
                   jax.ShapeDtypeStruct((1, D), F32), jax.ShapeDtypeStruct((1, D), F32),
                   jax.ShapeDtypeStruct((2, D), F32), jax.ShapeDtypeStruct((1, D), F32)),
        compiler_params=_params(("arbitrary",)),
    )(*args)


def _nt(dy, w, gate_up, *, name):
    T, D = dy.shape
    S, Ks, _ = w.shape
    tm, tn = _pick(T, TM_NT, 8), _pick(Ks, TN_NT)
    nps = Ks // tn
    nN = S * nps
    F = S * Ks

    def body(dy_ref, w_ref, *rest):
        dyv = dy_ref[...]
        for c0 in range(0, tn, NT_CHUNK):
            c1 = min(c0 + NT_CHUNK, tn)
            d = lax.dot_general(dyv, w_ref[c0:c1, :], (((1,), (1,)), ((), ())), preferred_element_type=F32)
            if gate_up is None:
                rest[0][:, c0:c1] = d
            else:
                gu_ref, dg_out = rest
                g, u = gu_ref[0, :, c0:c1].astype(F32), gu_ref[1, :, c0:c1].astype(F32)
                sg = _sigmoid(g)
                dg_out[0, :, c0:c1] = (d * u * (sg * (1.0 + g * (1.0 - sg)))).astype(MXU_DTYPE)
                dg_out[1, :, c0:c1] = (d * (g * sg)).astype(MXU_DTYPE)

    in_specs = [pl.BlockSpec((tm, D), lambda i, n: (i, 0)),
                pl.BlockSpec((None, tn, D), lambda i, n: (n // nps, n % nps, 0))]
    args = [dy, w]
    if gate_up is None:
        out_shape = jax.ShapeDtypeStruct((T, F), F32)
        out_specs = pl.BlockSpec((tm, tn), lambda i, n: (i, n))
    else:
        Ns = 2 * F // S
        q = Ns // tn
        in_specs.append(pl.BlockSpec((2, tm, tn), lambda i, n: (0, i, n)))
        args.append(gate_up)
        out_shape = jax.ShapeDtypeStruct((2, S // 2, T, Ns), MXU_DTYPE)
        out_specs = pl.BlockSpec((2, None, tm, tn), lambda i, n: (0, n // q, i, n % q))
    out = pl.pallas_call(
        body, name=name, grid=(T // tm, nN), in_specs=in_specs, out_specs=out_specs, out_shape=out_shape,
        compiler_params=_params(("arbitrary", "arbitrary")),
    )(*args)
    return out if gate_up is None else out.reshape(S, T, out.shape[-1])


def _dx_modbwd(dg, w, dres, xhat_in, g_in, b_in, scale, *, tc, name, after=None):
    S, T, Ns = dg.shape
    D = w.shape[1]
    tm, tk = _pick(T, TM_DX, 8), _pick(Ns, TK_DX)
    kps = Ns // tk
    nK = S * kps

    def body(dg_ref, w_ref, dres_ref, xin_ref, gi_ref, bi_ref, sc_ref, *rest):
        ds_out, dsc_out, dsh_out, acc = rest[-4:]
        i, k = pl.program_id(0), pl.program_id(1)

        @pl.when((i == 0) & (k == 0))
        def _():
            dsc_out[...] = jnp.zeros_like(dsc_out)
            dsh_out[...] = jnp.zeros_like(dsh_out)

        @pl.when(k == 0)
        def _():
            acc[...] = jnp.zeros_like(acc)

        dgv = dg_ref[...]
        for c0 in range(0, D, ACC_CHUNK):
            c1 = min(c0 + ACC_CHUNK, D)
            acc[:, c0:c1] += lax.dot_general(dgv, w_ref[c0:c1, :], (((1,), (1,)), ((), ())), preferred_element_type=F32)

        @pl.when(k == nK - 1)
        def _():
            dh = acc[...]
            isctx = _rows(i, tm) < tc
            ds_out[...] = dres_ref[...] + dh * (1.0 + _sel(isctx, sc_ref))
            s = xin_ref[...] * gi_ref[...] + bi_ref[...]
            p = dh * s
            dsc_out[0:1, :] += _colsum(jnp.where(isctx, p, 0.0))
            dsc_out[1:2, :] += _colsum(jnp.where(isctx, 0.0, p))
            dsh_out[0:1, :] += _colsum(jnp.where(isctx, dh, 0.0))
            dsh_out[1:2, :] += _colsum(jnp.where(isctx, 0.0, dh))

    vec = pl.BlockSpec((1, D), lambda i, k: (0, 0))
    vec2 = pl.BlockSpec((2, D), lambda i, k: (0, 0))
    row = pl.BlockSpec((tm, D), lambda i, k: (i, 0))
    in_specs = [pl.BlockSpec((None, tm, tk), lambda i, k: (k // kps, i, k % kps)),
                pl.BlockSpec((None, D, tk), lambda i, k: (k // kps, 0, k % kps)),
                row, row, vec, vec, vec2]
    args = [dg, w, dres, xhat_in, g_in, b_in, scale]
    if after is not None:
        in_specs.append(pl.BlockSpec(memory_space=pl.ANY))
        args.append(after)
    return pl.pallas_call(
        body, name=name, grid=(T // tm, nK), in_specs=in_specs,
        out_specs=(row, vec2, vec2),
        out_shape=(jax.ShapeDtypeStruct((T, D), F32), jax.ShapeDtypeStruct((2, D), F32), jax.ShapeDtypeStruct((2, D), F32)),
        scratch_shapes=[pltpu.VMEM((tm, D), F32)], compiler_params=_params(("arbitrary", "arbitrary")),
    )(*args)


def _wgrad(a, b, *, cols_sharded, name, after=None):
    T = a.shape[0]
    tk = _pick(T, TK_WG, 8)
    if cols_sharded:
        S, _, Ns = b.shape
        D = a.shape[1]
        tb = _pick(Ns, TB_WG)
        grid = (S, Ns // tb, T // tk)
        in_specs = [pl.BlockSpec((tk, D), lambda s, j, k: (k, 0)), pl.BlockSpec((None, tk, tb), lambda s, j, k: (s, k, j))]
        out_shape = jax.ShapeDtypeStruct((S, D, Ns), F32)
        out_specs = pl.BlockSpec((None, D, tb), lambda s, j, k: (s, 0, j))
    else:
        D = b.shape[1]
        S = N_CHIPS
        Ks = a.shape[1] // S
        ta = _pick(Ks, TB_WG)
        q = Ks // ta
        grid = (S, q, T // tk)
        in_specs = [pl.BlockSpec((tk, ta), lambda s, j, k: (k, s * q + j)), pl.BlockSpec((tk, D), lambda s, j, k: (k, 0))]
        out_shape = jax.ShapeDtypeStruct((S, Ks, D), F32)
        out_specs = pl.BlockSpec((None, ta, D), lambda s, j, k: (s, j, 0))

    def body(a_ref, b_ref, *rest):
        o_ref = rest[-1]

        @pl.when(pl.program_id(2) == 0)
        def _():
            o_ref[...] = jnp.zeros_like(o_ref)

        o_ref[...] += lax.dot_general(a_ref[...], b_ref[...], (((0,), (0,)), ((), ())), preferred_element_type=F32)

    args = [a, b]
    if after is not None:
        in_specs.append(pl.BlockSpec(memory_space=pl.ANY))
        args.append(after)
    return pl.pallas_call(
        body, name=name, grid=grid, in_specs=in_specs, out_specs=out_specs, out_shape=out_shape,
        compiler_params=_params(("arbitrary", "arbitrary", "arbitrary")),
    )(*args)


def _halo_specs(tms, T, C, slot=None):
    r8 = tms // 8
    last8 = T // 8 - 1
    if slot is None:
        return (pl.BlockSpec((8, C), lambda i: (jnp.maximum(i * r8 - 1, 0), 0)),
                pl.BlockSpec((tms, C), lambda i: (i, 0)),
                pl.BlockSpec((8, C), lambda i: (jnp.minimum((i + 1) * r8, last8), 0)))
    return (pl.BlockSpec((None, 8, C), lambda i: (slot, jnp.maximum(i * r8 - 1, 0), 0)),
            pl.BlockSpec((None, tms, C), lambda i: (slot, i, 0)),
            pl.BlockSpec((None, 8, C), lambda i: (slot, jnp.minimum((i + 1) * r8, last8), 0)))


def _extended(prev_ref, cur, next_ref, i, n_tiles):
    first = (i == 0) | (i == 1)
    last = (i == 0) | (i == n_tiles - 1)
    pv = jnp.where(first, 0.0, prev_ref[...])
    nx = jnp.where(last, 0.0, next_ref[...])
    return jnp.concatenate([pv, cur, nx], axis=0)


def _shifted(ext, o, tms):
    n = tms + 16
    return pltpu.roll(ext, (-o) % n, 0)[8:8 + tms]


def _conv4_fwd(z, w4, b4, *, tms, name):
    S, T, C = z.shape
    nT = T // tms

    def body(p_ref, c_ref, n_ref, w_ref, b_ref, o_ref):
        i = pl.program_id(0)
        ext = _extended(p_ref, c_ref[...], n_ref, i, nT)
        acc = jnp.zeros((tms, C), F32) + b_ref[...]
        for k in range(4):
            acc = acc + w_ref[k:k + 1, :] * _shifted(ext, k - 2, tms)
        o_ref[...] = acc

    return pl.pallas_call(
        body, name=name, grid=(nT,),
        in_specs=[*_halo_specs(tms, T, C, 0), pl.BlockSpec((4, C), lambda i: (0, 0)), pl.BlockSpec((1, C), lambda i: (0, 0))],
        out_specs=pl.BlockSpec((tms, C), lambda i: (i, 0)), out_shape=jax.ShapeDtypeStruct((T, C), F32),
        compiler_params=_params(("arbitrary",)),
    )(z, z, z, w4, b4)


def _conv4_bwd(dxa, dxb, z, w4, dz, *, tms, name):
    S, T, C = z.shape
    nT = T // tms

    def body(pa, ca, na, pb, cb, nb, px, cx, nx, w_ref, dz_in, dz_out, dw_out, db_out):
        i = pl.program_id(0)

        @pl.when(i == 0)
        def _():
            dw_out[...] = jnp.zeros_like(dw_out)
            db_out[...] = jnp.zeros_like(db_out)

        dcur = ca[...] + cb[...]
        first = (i == 0) | (i == 1)
        last = (i == 0) | (i == nT - 1)
        dext = jnp.concatenate([jnp.where(first, 0.0, pa[...] + pb[...]), dcur, jnp.where(last, 0.0, na[...] + nb[...])], axis=0)
        xext = _extended(px, cx[...], nx, i, nT)
        acc = jnp.zeros((tms, C), F32)
        for k in range(4):
            acc = acc + w_ref[k:k + 1, :] * _shifted(dext, 2 - k, tms)
            dw_out[k:k + 1, :] += _colsum(dcur * _shifted(xext, k - 2, tms))
        db_out[...] += _colsum(dcur)
        dz_out[...] = acc.astype(MXU_DTYPE)

    h = _halo_specs(tms, T, C)
    return pl.pallas_call(
        body, name=name, grid=(nT,),
        in_specs=[*h, *h, *_halo_specs(tms, T, C, 0), pl.BlockSpec((4, C), lambda i: (0, 0)), pl.BlockSpec(memory_space=pl.ANY)],
        out_specs=(pl.BlockSpec((None, tms, C), lambda i: (0, i, 0)), pl.BlockSpec((4, C), lambda i: (0, 0)),
                   pl.BlockSpec((1, C), lambda i: (0, 0))),
        out_shape=(jax.ShapeDtypeStruct(dz.shape, dz.dtype), jax.ShapeDtypeStruct((4, C), F32), jax.ShapeDtypeStruct((1, C), F32)),
        input_output_aliases={10: 0}, compiler_params=_params(("arbitrary",)),
    )(dxa, dxa, dxa, dxb, dxb, dxb, z, z, z, w4, dz)


def _scan_rows(a, b, h_in, rev, n):
    sub = lax.broadcasted_iota(jnp.int32, (n, 1), 0) & 7
    for sft in (1, 2, 4):
        if rev:
            a_sh, b_sh, valid = pltpu.roll(a, n - sft, 0), pltpu.roll(b, n - sft, 0), sub < 8 - sft
        else:
            a_sh, b_sh, valid = pltpu.roll(a, sft, 0), pltpu.roll(b, sft, 0), sub >= sft
        b = a * jnp.where(valid, b_sh, 0.0) + b
        a = a * jnp.where(valid, a_sh, 1.0)
    out, c = [None] * (n // 8), h_in
    for g in (reversed(range(n // 8)) if rev else range(n // 8)):
        out[g] = a[8 * g:8 * g + 8] * c + b[8 * g:8 * g + 8]
        c = out[g][0:1] if rev else out[g][7:8]
    return jnp.concatenate(out, axis=0)


def _scan_order(i, rev, nT):
    if not rev:
        return i
    return jnp.where(i == 0, 0, nT - i)


def _lru_fwd(xc, wr, wi, br, bi, lam, *, rev, tms, name):
    T, C = xc.shape
    nH, hd, _ = wr.shape
    nT = T // tms
    sw = min(hd, LANES)

    def body(xc_ref, wr_ref, wi_ref, br_ref, bi_ref, lam_ref, h_out, r_out, i_out, carry):
        @pl.when(pl.program_id(0) == 0)
        def _():
            carry[...] = jnp.zeros_like(carry)

        for hx in range(nH):
            xh = xc_ref[:, hx * hd:(hx + 1) * hd]
            xb = xh.astype(MXU_DTYPE)
            pre_r = jnp.dot(xb, wr_ref[hx], preferred_element_type=F32)
            pre_i = jnp.dot(xb, wi_ref[hx], preferred_element_type=F32)
            for a0, a1 in _strips(hd, sw):
                sl = slice(hx * hd + a0, hx * hd + a1)
                x = xh[:, a0:a1]
                r = _sigmoid(pre_r[:, a0:a1] + br_ref[:, sl])
                ig = _sigmoid(pre_i[:, a0:a1] + bi_ref[:, sl])
                log_a = (-RG_C * r) * _softplus(-lam_ref[:, sl])
                a = jnp.exp(log_a)
                b = jnp.sqrt(_one_minus_sq(log_a, a)) * (ig * x)
                h = _scan_rows(a, b, carry[:, sl], rev, tms)
                carry[:, sl] = h[0:1, :] if rev else h[tms - 1:tms, :]
                h_out[:, sl] = h
                r_out[:, sl] = r
                i_out[:, sl] = ig

    tile = pl.BlockSpec((tms, C), lambda i: (_scan_order(i, rev, nT), 0))
    wspec = pl.BlockSpec((nH, hd, hd), lambda i: (0, 0, 0))
    vec = pl.BlockSpec((1, C), lambda i: (0, 0))
    shp = jax.ShapeDtypeStruct((T, C), F32)
    return pl.pallas_call(
        body, name=name, grid=(nT,), in_specs=[tile, wspec, wspec, vec, vec, vec], out_specs=(tile, tile, tile),
        out_shape=(shp, shp, shp), scratch_shapes=[pltpu.VMEM((1, C), F32)], compiler_params=_params(("arbitrary",)),
    )(xc, wr, wi, br, bi, lam)


def _lru_bwd(dh, h, r, ig, xc, wr, wi, lam, *, rev, tms, name):
    T, C = xc.shape
    nH, hd, _ = wr.shape
    nT = T // tms
    r8 = tms // 8
    sw = min(hd, LANES)

    def tile_of(ip):
        return _scan_order(nT - 1 - ip, rev, nT)

    def halo_of(ip):
        i = nT - 1 - ip
        if not rev:
            return jnp.maximum(i * r8 - 1, 0)
        return jnp.where(i <= 1, 0, (nT - i + 1) * r8)

    def body(dh_ref, h_ref, hh_ref, r_ref, i_ref, xc_ref, wr_ref, wi_ref, lam_ref,
             dxc_out, dwr_out, dwi_out, dbr_out, dbi_out, dlam_out, ucarry):
        ip = pl.program_id(0)
        pos = nT - 1 - ip

        @pl.when(ip == 0)
        def _():
            ucarry[...] = jnp.zeros_like(ucarry)
            for o in (dwr_out, dwi_out, dbr_out, dbi_out, dlam_out):
                o[...] = jnp.zeros_like(o)

        rows = lax.broadcasted_iota(jnp.int32, (tms, 1), 0)
        nt = (((1,), (1,)), ((), ()))
        tn = (((0,), (0,)), ((), ()))
        for hx in range(nH):
            dpr_parts, dpi_parts, direct_parts = [], [], []
            for a0, a1 in _strips(hd, sw):
                sl = slice(hx * hd + a0, hx * hd + a1)
                x, rr, ii, hh, dhv = xc_ref[:, sl], r_ref[:, sl], i_ref[:, sl], h_ref[:, sl], dh_ref[:, sl]
                sp = _softplus(-lam_ref[:, sl])
                log_a = (-RG_C * rr) * sp
                a = jnp.exp(log_a)
                s = jnp.sqrt(_one_minus_sq(log_a, a))
                u_in = ucarry[:, sl]
                u = _scan_rows(a, a * dhv, u_in, not rev, tms)
                if rev:
                    u_next = jnp.where(rows == 0, u_in, pltpu.roll(u, 1, 0))
                    ucarry[:, sl] = u[tms - 1:tms, :]
                    h_halo = jnp.where(pos == 0, 0.0, hh_ref[0:1, sl])
                    h_prev = jnp.where(rows == tms - 1, h_halo, pltpu.roll(hh, tms - 1, 0))
                else:
                    u_next = jnp.where(rows == tms - 1, u_in, pltpu.roll(u, tms - 1, 0))
                    ucarry[:, sl] = u[0:1, :]
                    h_halo = jnp.where(pos == 0, 0.0, hh_ref[7:8, sl])
                    h_prev = jnp.where(rows == 0, h_halo, pltpu.roll(hh, 1, 0))
                g = dhv + u_next
                dgated = g * s
                dlog_a = (g * h_prev) * a - (g * (ii * x)) * (a * a) / s
                dlam_out[:, sl] += _colsum(dlog_a * (-RG_C * rr)) * (-_sigmoid(-lam_ref[:, sl]))
                dpr = (dlog_a * (-RG_C * sp)) * rr * (1.0 - rr)
                dpi = (dgated * x) * ii * (1.0 - ii)
                dbr_out[:, sl] += _colsum(dpr)
                dbi_out[:, sl] += _colsum(dpi)
                dpr_parts.append(dpr.astype(MXU_DTYPE))
                dpi_parts.append(dpi.astype(MXU_DTYPE))
                direct_parts.append(dgated * ii)
            cat = (lambda p: p[0] if len(p) == 1 else jnp.concatenate(p, axis=1))
            dprh, dpih = cat(dpr_parts), cat(dpi_parts)
            xh = xc_ref[:, hx * hd:(hx + 1) * hd].astype(MXU_DTYPE)
            dxc_out[:, hx * hd:(hx + 1) * hd] = (cat(direct_parts)
                                                + lax.dot_general(dprh, wr_ref[hx], nt, preferred_element_type=F32)
                                                + lax.dot_general(dpih, wi_ref[hx], nt, preferred_element_type=F32))
            dwr_out[hx] += lax.dot_general(xh, dprh, tn, preferred_element_type=F32)
            dwi_out[hx] += lax.dot_general(xh, dpih, tn, preferred_element_type=F32)

    tile = pl.BlockSpec((tms, C), lambda ip: (tile_of(ip), 0))
    wspec = pl.BlockSpec((nH, hd, hd), lambda ip: (0, 0, 0))
    vec = pl.BlockSpec((1, C), lambda ip: (0, 0))
    wshape = jax.ShapeDtypeStruct((nH, hd, hd), F32)
    vshape = jax.ShapeDtypeStruct((1, C), F32)
    return pl.pallas_call(
        body, name=name, grid=(nT,),
        in_specs=[tile, tile, pl.BlockSpec((8, C), lambda ip: (halo_of(ip), 0)), tile, tile, tile, wspec, wspec, vec],
        out_specs=(tile, wspec, wspec, vec, vec, vec),
        out_shape=(jax.ShapeDtypeStruct((T, C), F32), wshape, wshape, vshape, vshape, vshape),
        scratch_shapes=[pltpu.VMEM((1, C), F32)], compiler_params=_params(("arbitrary",)),
    )(dh, h, h, r, ig, xc, wr, wi, lam)


CONV_PAD = 16


def _shift_bank(u):
    pad = jnp.zeros((CONV_PAD, u.shape[1]), u.dtype)
    ext = jnp.concatenate([pad, u, pad], axis=0)
    return [ext] + [pltpu.roll(ext, r, 0) for r in range(1, 8)]


def _shifted_rows(bank, o, rows):
    a = -((-o) // 8)
    start = CONV_PAD + 8 * a
    return bank[8 * a - o][start:start + rows]


def _convmod_fwd(hf, hb, z, w31, b31, clg, clb, *, tms, name):
    S, T, C = z.shape
    K = w31.shape[0]
    nT = T // tms

    sw = min(C, LANES)
    G = GRID_W

    def body(hf_ref, hb_ref, gr_ref, cv_ref, cg_ref, w_ref, b_ref, g_ref, bb_ref, y_out, uc_out):
        i = pl.program_id(0)
        for a0, a1 in _strips(C, sw):
            y_out[:, a0:a1] = ((hf_ref[:, a0:a1] + hb_ref[:, a0:a1]) * _gelu(gr_ref[:, a0:a1])).astype(MXU_DTYPE)

        def conv(seg):
            for r0 in range(0, tms, seg):
                rs = slice(r0, r0 + seg)
                s1 = jnp.zeros((seg, 1), F32)
                for a0, a1 in _strips(C, sw):
                    bank = _shift_bank(cv_ref[rs, a0:a1] * _sigmoid(cg_ref[rs, a0:a1]))
                    acc = jnp.zeros((seg, sw), F32) + b_ref[:, a0:a1]
                    for k in range(K):
                        acc = acc + w_ref[k:k + 1, a0:a1] * _shifted_rows(bank, k - K // 2, seg)
                    uc_out[rs, a0:a1] = acc
                    s1 = s1 + jnp.sum(acc, axis=-1, keepdims=True)
                mu = s1 / C
                s2 = jnp.zeros((seg, 1), F32)
                for a0, a1 in _strips(C, sw):
                    d = uc_out[rs, a0:a1] - mu
                    s2 = s2 + jnp.sum(d * d, axis=-1, keepdims=True)
                rstd = lax.rsqrt(s2 / C + LN_EPS)
                for a0, a1 in _strips(C, sw):
                    yl = (uc_out[rs, a0:a1] - mu) * rstd * g_ref[:, a0:a1] + bb_ref[:, a0:a1]
                    y_out[rs, C + a0:C + a1] = (yl * _sigmoid(yl)).astype(MXU_DTYPE)

        @pl.when(i == 0)
        def _():
            conv(tms)

        @pl.when(i != 0)
        def _():
            conv(G)

    tile = pl.BlockSpec((tms, C), lambda i: (i, 0))
    vec = pl.BlockSpec((1, C), lambda i: (0, 0))
    zs = [pl.BlockSpec((None, tms, C), functools.partial(lambda i, s: (s, i, 0), s=s)) for s in (1, 2, 3)]
    return pl.pallas_call(
        body, name=name, grid=(nT,),
        in_specs=[tile, tile, *zs, pl.BlockSpec((K, C), lambda i: (0, 0)), vec, vec, vec],
        out_specs=(pl.BlockSpec((tms, 2 * C), lambda i: (i, 0)), tile),
        out_shape=(jax.ShapeDtypeStruct((T, 2 * C), MXU_DTYPE), jax.ShapeDtypeStruct((T, C), F32)),
        compiler_params=_params(("arbitrary",)),
    )(hf, hb, z, z, z, w31, b31, clg, clb)


def _convmod_bwd(dymix, hf, hb, z, uc, w31, clg, clb, *, tms, name):
    S, T, C = z.shape
    K = w31.shape[0]
    nT = T // tms

    sw = min(C, LANES)

    def body(dy_ref, hf_ref, hb_ref, gr_ref, cv_ref, cg_ref, uc_ref, w_ref, g_ref, bb_ref,
             dhs_out, dz_out, dw_out, db_out, dg_out, dbb_out, dxh_buf):
        i = pl.program_id(0)

        @pl.when(i == 0)
        def _():
            for o in (dw_out, db_out, dg_out, dbb_out):
                o[...] = jnp.zeros_like(o)

        for a0, a1 in _strips(C, sw):
            dyr, gr = dy_ref[:, a0:a1], gr_ref[:, a0:a1]
            dhs_out[:, a0:a1] = dyr * _gelu(gr)
            dz_out[0, :, a0:a1] = jnp.zeros((tms, sw), MXU_DTYPE)
            dz_out[1, :, a0:a1] = (dyr * (hf_ref[:, a0:a1] + hb_ref[:, a0:a1]) * _gelu_grad(gr)).astype(MXU_DTYPE)

        def conv(seg):
            for r0 in range(0, tms, seg):
                rs = slice(r0, r0 + seg)
                s1 = jnp.zeros((seg, 1), F32)
                for a0, a1 in _strips(C, sw):
                    s1 = s1 + jnp.sum(uc_ref[rs, a0:a1], axis=-1, keepdims=True)
                mu = s1 / C
                s2 = jnp.zeros((seg, 1), F32)
                for a0, a1 in _strips(C, sw):
                    d = uc_ref[rs, a0:a1] - mu
                    s2 = s2 + jnp.sum(d * d, axis=-1, keepdims=True)
                rstd = lax.rsqrt(s2 / C + LN_EPS)
                m1 = jnp.zeros((seg, 1), F32)
                m2 = jnp.zeros((seg, 1), F32)
                for a0, a1 in _strips(C, sw):
                    xh = (uc_ref[rs, a0:a1] - mu) * rstd
                    yl = xh * g_ref[:, a0:a1] + bb_ref[:, a0:a1]
                    sg = _sigmoid(yl)
                    dyl = dy_ref[rs, C + a0:C + a1] * (sg * (1.0 + yl * (1.0 - sg)))
                    dg_out[:, a0:a1] += _colsum(dyl * xh)
                    dbb_out[:, a0:a1] += _colsum(dyl)
                    dxh = dyl * g_ref[:, a0:a1]
                    dxh_buf[rs, a0:a1] = dxh
                    m1 = m1 + jnp.sum(dxh, axis=-1, keepdims=True)
                    m2 = m2 + jnp.sum(dxh * xh, axis=-1, keepdims=True)
                m1, m2 = m1 / C, m2 / C
                for a0, a1 in _strips(C, sw):
                    xh = (uc_ref[rs, a0:a1] - mu) * rstd
                    duc = rstd * (dxh_buf[rs, a0:a1] - m1 - xh * m2)
                    db_out[:, a0:a1] += _colsum(duc)
                    cv, sc = cv_ref[rs, a0:a1], _sigmoid(cg_ref[rs, a0:a1])
                    bank_u, bank_d = _shift_bank(cv * sc), _shift_bank(duc)
                    du = jnp.zeros((seg, sw), F32)
                    for k in range(K):
                        o = k - K // 2
                        du = du + w_ref[k:k + 1, a0:a1] * _shifted_rows(bank_d, -o, seg)
                        dw_out[k:k + 1, a0:a1] += _colsum(duc * _shifted_rows(bank_u, o, seg))
                    dz_out[2, rs, a0:a1] = (du * sc).astype(MXU_DTYPE)
                    dz_out[3, rs, a0:a1] = (du * cv * sc * (1.0 - sc)).astype(MXU_DTYPE)

        @pl.when(i == 0)
        def _():
            conv(tms)

        @pl.when(i != 0)
        def _():
            conv(GRID_W)

    tile = pl.BlockSpec((tms, C), lambda i: (i, 0))
    vec = pl.BlockSpec((1, C), lambda i: (0, 0))
    kc = pl.BlockSpec((K, C), lambda i: (0, 0))
    zs = [pl.BlockSpec((None, tms, C), functools.partial(lambda i, s: (s, i, 0), s=s)) for s in (1, 2, 3)]
    vshape = jax.ShapeDtypeStruct((1, C), F32)
    return pl.pallas_call(
        body, name=name, grid=(nT,),
        in_specs=[pl.BlockSpec((tms, 2 * C), lambda i: (i, 0)), tile, tile, *zs, tile, kc, vec, vec],
        out_specs=(tile, pl.BlockSpec((S, tms, C), lambda i: (0, i, 0)), kc, vec, vec, vec),
        out_shape=(jax.ShapeDtypeStruct((T, C), F32), jax.ShapeDtypeStruct((S, T, C), MXU_DTYPE),
                   jax.ShapeDtypeStruct((K, C), F32), vshape, vshape, vshape),
        scratch_shapes=[pltpu.VMEM((tms, C), F32)], compiler_params=_params(("arbitrary",)),
    )(dymix, hf, hb, z, z, z, uc, w31, clg, clb)


def _loss_head(xhat, g, b, target, *, tc, tms, name):
    T, D = xhat.shape
    nT = T // tms
    nc = tc // tms

    def body(x_ref, g_ref, b_ref, t_ref, dy_out, loss_out):
        i = pl.program_id(0)

        @pl.when(i == 0)
        def _():
            loss_out[...] = jnp.zeros_like(loss_out)

        err = jnp.where(i < nc, 0.0, x_ref[...] * g_ref[...] + b_ref[...] - t_ref[...])
        dy_out[...] = err / D
        loss_out[...] += 0.5 * jnp.sum(jnp.sum(err * err, axis=-1, keepdims=True) / D)

    vec = pl.BlockSpec((1, D), lambda i: (0, 0))
    tile = pl.BlockSpec((tms, D), lambda i: (i, 0))
    return pl.pallas_call(
        body, name=name, grid=(nT,),
        in_specs=[tile, vec, vec, pl.BlockSpec((tms, D), lambda i: (jnp.maximum(i - nc, 0), 0))],
        out_specs=(tile, pl.BlockSpec((8, 128), lambda i: (0, 0))),
        out_shape=(jax.ShapeDtypeStruct((T, D), F32), jax.ShapeDtypeStruct((8, 128), F32)),
        compiler_params=_params(("arbitrary",)),
    )(xhat, g, b, target)


def _ada_fwd(s16, w_ada, b_cols, *, name):
    L, D, Na = w_ada.shape
    tn = _pick(Na, 512)

    def body(s_ref, w_ref, b_ref, o_ref):
        o_ref[...] = jnp.dot(s_ref[...], w_ref[...].astype(MXU_DTYPE), preferred_element_type=F32) + b_ref[...]

    return pl.pallas_call(
        body, name=name, grid=(L, Na // tn),
        in_specs=[pl.BlockSpec((16, D), lambda l, n: (0, 0)), pl.BlockSpec((None, D, tn), lambda l, n: (l, 0, n)),
                  pl.BlockSpec((None, 1, tn), lambda l, n: (l, 0, n))],
        out_specs=pl.BlockSpec((None, 16, tn), lambda l, n: (l, 0, n)),
        out_shape=jax.ShapeDtypeStruct((L, 16, Na), F32), compiler_params=_params(("arbitrary", "arbitrary")),
    )(s16, w_ada, b_cols.reshape(L, 1, Na))


def _ada_bwd(s16, dm16, w_ada, *, name):
    L, D, Na = w_ada.shape
    tn = _pick(Na, 512)

    def body(s_ref, dm_ref, w_ref, dw_out, ds_out):
        @pl.when((pl.program_id(0) == 0) & (pl.program_id(1) == 0))
        def _():
            ds_out[...] = jnp.zeros_like(ds_out)

        dm = dm_ref[...].astype(MXU_DTYPE)
        dw_out[...] = lax.dot_general(s_ref[...], dm, (((0,), (0,)), ((), ())), preferred_element_type=F32)
        ds_out[...] += lax.dot_general(dm, w_ref[...].astype(MXU_DTYPE), (((1,), (1,)), ((), ())), preferred_element_type=F32)

    return pl.pallas_call(
        body, name=name, grid=(L, Na // tn),
        in_specs=[pl.BlockSpec((16, D), lambda l, n: (0, 0)), pl.BlockSpec((None, 16, tn), lambda l, n: (l, 0, n)),
                  pl.BlockSpec((None, D, tn), lambda l, n: (l, 0, n))],
        out_specs=(pl.BlockSpec((None, D, tn), lambda l, n: (l, 0, n)), pl.BlockSpec((16, D), lambda l, n: (0, 0))),
        out_shape=(jax.ShapeDtypeStruct((L, D, Na), F32), jax.ShapeDtypeStruct((16, D), F32)),
        compiler_params=_params(("arbitrary", "arbitrary")),
    )(s16, dm16, w_ada)


def _silu_rows(cvec, *, name):
    R, D = cvec.shape

    def body(c_ref, s_out, ds_out):
        c = c_ref[...]
        sg = _sigmoid(c)
        s_out[...] = (c * sg).astype(MXU_DTYPE)
        ds_out[...] = sg * (1.0 + c * (1.0 - sg))

    return pl.pallas_call(
        body, name=name, out_shape=(jax.ShapeDtypeStruct((R, D), MXU_DTYPE), jax.ShapeDtypeStruct((R, D), F32)),
    )(cvec)


def _sum_leading(v, order, *, name, scale_by=None):
    N, R, C = v.shape
    tr = _pick(R, max(8, ADAM_BLOCK_ELEMS // C), 8)

    def body(v_ref, *rest):
        acc = v_ref[order[0]]
        for j in order[1:]:
            acc = acc + v_ref[j]
        if scale_by is not None:
            acc = acc * rest[0][...]
        rest[-1][...] = acc

    in_specs = [pl.BlockSpec((N, tr, C), lambda i: (0, i, 0))]
    args = [v]
    if scale_by is not None:
        in_specs.append(pl.BlockSpec((tr, C), lambda i: (i, 0)))
        args.append(scale_by)
    return pl.pallas_call(
        body, name=name, grid=(R // tr,), in_specs=in_specs, out_specs=pl.BlockSpec((tr, C), lambda i: (i, 0)),
        out_shape=jax.ShapeDtypeStruct((R, C), F32), compiler_params=_params(("arbitrary",)),
    )(*args)


def _sum_scattered(g, land, chip_arr, l, n_layers, prev, *, name):
    S, R, C = g.shape
    tr = _pick(R, max(8, ADAM_BLOCK_ELEMS // C), 8)

    def body(ch_ref, g_ref, a_ref, b_ref, c_ref, *rest):
        rest[-1][...] = ((g_ref[...] + a_ref[...]) + b_ref[...]) + c_ref[...]

    def slot(d):
        return pl.BlockSpec((None, tr, C), lambda i, ch: ((ch[0] + d) % S, i, 0))

    in_specs = [slot(0), slot(1), slot(2), slot(3)]
    args, aliases = [chip_arr, g, land, land, land], {}
    if prev is not None:
        in_specs.append(pl.BlockSpec(memory_space=pl.ANY))
        args.append(prev)
        aliases = {5: 0}
    grid_spec = pltpu.PrefetchScalarGridSpec(
        num_scalar_prefetch=1, grid=(R // tr,), in_specs=in_specs,
        out_specs=pl.BlockSpec((None, tr, C), lambda i, ch: (l, i, 0)))
    return pl.pallas_call(
        body, name=name, grid_spec=grid_spec, out_shape=jax.ShapeDtypeStruct((n_layers, R, C), F32),
        input_output_aliases=aliases, compiler_params=_params(("arbitrary",)),
    )(*args)


def _adamw(w, g, m, v, g_other=None, *, name):
    shape = w.shape
    C = shape[-1]
    R = w.size // C
    two = g_other is not None
    ins = [t.reshape(R, C) for t in ((w, g, m, v, g_other) if two else (w, g, m, v))]
    tr = _pick(R, max(8, ADAM_BLOCK_ELEMS // C), 8)

    def body(w_ref, g_ref, m_ref, v_ref, *rest):
        d_out, m_out, v_out = rest[-3:]
        gg = g_ref[...]
        if two:
            gg = gg + rest[0][...]
            rest[1][...] = gg
        mn = ADAM_B1 * m_ref[...] + (1.0 - ADAM_B1) * gg
        vn = ADAM_B2 * v_ref[...] + (1.0 - ADAM_B2) * (gg * gg)
        m_hat = mn / (1.0 - ADAM_B1 ** ADAM_STEP)
        v_hat = vn / (1.0 - ADAM_B2 ** ADAM_STEP)
        d_out[...] = -ADAM_LR * (m_hat / (jnp.sqrt(v_hat) + ADAM_EPS) + ADAM_WD * w_ref[...])
        m_out[...] = mn
        v_out[...] = vn

    blk = pl.BlockSpec((tr, C), lambda i: (i, 0))
    shp = jax.ShapeDtypeStruct((R, C), F32)
    n_out = 4 if two else 3
    outs = pl.pallas_call(
        body, name=name, grid=(R // tr,), in_specs=[blk] * len(ins), out_specs=(blk,) * n_out, out_shape=(shp,) * n_out,
        compiler_params=_params(("arbitrary",)),
    )(*ins)
    return tuple(o.reshape(shape) for o in outs)


def _place():
    x, y, c = lax.axis_index("x"), lax.axis_index("y"), lax.axis_index("c")
    return x, y, c, [(1 - x, y), (x, 1 - y), (1 - x, 1 - y)]


def _allgather_small(v, *, name, after=None):
    m_per, n = v.shape

    def body(x_ref, *rest):
        out_ref, send_sems, recv_sems, local_sem = rest[-4:]
        x, y, c, chips = _place()
        me, sibling = (x, y, c), (x, y, 1 - c)

        def rows(px, py, pc):
            return out_ref.at[pl.ds((4 * px + 2 * py + pc) * m_per, m_per), :]

        def copy(k, block, to, src=None):
            return pltpu.make_async_remote_copy(
                src_ref=rows(*block) if src is None else src, dst_ref=rows(*block),
                send_sem=send_sems.at[k], recv_sem=recv_sems.at[k], device_id=to, device_id_type=MESH)

        mine = pltpu.make_async_copy(x_ref, rows(*me), local_sem)
        mine.start()
        first = [copy(0, me, sibling, src=x_ref)]
        first += [copy(1 + j, me, (*chip, c), src=x_ref) for j, chip in enumerate(chips)]
        for cp in first:
            cp.start()
        passed = [copy(4 + j, (*chip, c), sibling) for j, chip in enumerate(chips)]
        for j, chip in enumerate(chips):
            copy(1 + j, (*chip, c), me).wait_recv()
            passed[j].start()
        copy(0, sibling, me).wait_recv()
        for j, chip in enumerate(chips):
            copy(4 + j, (*chip, 1 - c), me).wait_recv()
        for cp in first + passed:
            cp.wait_send()
        mine.wait()

    in_specs, args = [pl.BlockSpec(memory_space=pltpu.VMEM)], [v]
    if after is not None:
        in_specs.append(pl.BlockSpec(memory_space=pl.ANY))
        args.append(after)
    return pl.pallas_call(
        body, name=name, out_shape=jax.ShapeDtypeStruct((N_DEV * m_per, n), v.dtype),
        in_specs=in_specs, out_specs=pl.BlockSpec(memory_space=pltpu.VMEM),
        scratch_shapes=[pltpu.SemaphoreType.DMA((7,)), pltpu.SemaphoreType.DMA((7,)), pltpu.SemaphoreType.DMA],
        compiler_params=pltpu.CompilerParams(vmem_limit_bytes=VMEM_LIMIT_BYTES),
    )(*args)


_HBM = pl.BlockSpec(memory_space=pltpu.HBM)
_SEM = pl.BlockSpec(memory_space=pltpu.SEMAPHORE)
_ANY = pl.BlockSpec(memory_space=pl.ANY)
_TOKEN = jax.ShapeDtypeStruct((8, 128), F32)


def _in_hbm(a):
    return pltpu.with_memory_space_constraint(a, pltpu.HBM)


def _place_cast(w, l, chip_arr, *, name):
    L, C = w.shape[0], w.shape[-1]
    R = w.size // (L * C)
    tr = _pick(R, max(8, ADAM_BLOCK_ELEMS // C), 16)

    def body(ch_ref, w_ref, o_ref):
        o_ref[...] = w_ref[...].astype(MXU_DTYPE)

    grid_spec = pltpu.PrefetchScalarGridSpec(
        num_scalar_prefetch=1, grid=(R // tr,),
        in_specs=[pl.BlockSpec((None, tr, C), lambda i, ch: (l, i, 0))],
        out_specs=pl.BlockSpec((None, tr, C), lambda i, ch: (ch[0], i, 0)))
    return pl.pallas_call(
        body, name=name, grid_spec=grid_spec, out_shape=jax.ShapeDtypeStruct((N_CHIPS, R, C), MXU_DTYPE),
        compiler_params=_params(("arbitrary",)),
    )(chip_arr, w.reshape(L, R, C))


def _chip_copies(srcs, lands, sends, recvs, k, j, px, py):
    x, y, c, _ = _place()
    me, peer = 2 * x + y, 2 * px + py
    src = srcs[k].at[peer] if srcs else lands[k].at[me]
    return pltpu.make_async_remote_copy(
        src_ref=src, dst_ref=lands[k].at[me], send_sem=sends[k].at[j], recv_sem=recvs[k].at[j],
        device_id=(px, py, c), device_id_type=MESH), peer


def _exchange_start(srcs, lands, after, *, name):
    ns, n = len(srcs), len(lands)
    arrays = (*srcs, *lands)

    def body(*refs):
        outs = refs[-(2 * n + ns + n + 1):]
        chips = _place()[3]
        for k in range(n):
            for j, (px, py) in enumerate(chips):
                _chip_copies(refs[:ns], refs[ns:ns + n], outs[:n], outs[n:2 * n], k, j, px, py)[0].start()
        outs[-1][...] = jnp.zeros_like(outs[-1])

    sem = pltpu.SemaphoreType.DMA((N_CHIPS - 1,))
    args = [_in_hbm(a) for a in arrays]
    in_specs = [_HBM] * (ns + n)
    if after is not None:
        args.append(after)
        in_specs.append(_ANY)
    outs = pl.pallas_call(
        body, name=name,
        out_shape=(*[sem] * (2 * n), *[pltpu.HBM(a.shape, a.dtype) for a in arrays], _TOKEN),
        in_specs=in_specs, out_specs=(*[_SEM] * (2 * n), *[_HBM] * (ns + n), pl.BlockSpec(memory_space=pltpu.VMEM)),
        input_output_aliases={k: 2 * n + k for k in range(ns + n)},
        compiler_params=pltpu.CompilerParams(has_side_effects=pltpu.SideEffectType.DATAFLOW_SIDE_EFFECTING),
    )(*args)
    return outs[:n], outs[n:2 * n], outs[2 * n:2 * n + ns], outs[2 * n + ns:2 * n + ns + n], outs[-1]


def _exchange_wait(sends, recvs, srcs, lands, after, *, name):
    ns, n = len(srcs), len(lands)

    def body(*refs):
        srcs_r, lands_r = refs[:ns], refs[ns:ns + n]
        sends_r, recvs_r = refs[ns + n:ns + 2 * n], refs[ns + 2 * n:ns + 3 * n]
        chips = _place()[3]
        for k in range(n):
            for j, (px, py) in enumerate(chips):
                cp, peer = _chip_copies(srcs_r, lands_r, sends_r, recvs_r, k, j, px, py)
                cp.wait_send()
                pltpu.make_async_remote_copy(
                    src_ref=lands_r[k].at[peer], dst_ref=lands_r[k].at[peer], send_sem=sends_r[k].at[j],
                    recv_sem=recvs_r[k].at[j], device_id=(px, py, _place()[2]), device_id_type=MESH).wait_recv()

    outs = pl.pallas_call(
        body, name=name, out_shape=[pltpu.HBM(a.shape, a.dtype) for a in (*srcs, *lands)],
        in_specs=[*[_HBM] * (ns + n), *[_SEM] * (2 * n), _ANY], out_specs=[_HBM] * (ns + n),
        input_output_aliases={k: k for k in range(ns + n)},
        compiler_params=pltpu.CompilerParams(has_side_effects=pltpu.SideEffectType.DATAFLOW_SIDE_EFFECTING),
    )(*srcs, *lands, *sends, *recvs, after)
    return outs[:ns], outs[ns:]


def _sibling_copy(srcs, lands, sends, recvs, k):
    x, y, c, _ = _place()
    return pltpu.make_async_remote_copy(src_ref=srcs[k], dst_ref=lands[k], send_sem=sends[k], recv_sem=recvs[k],
                                        device_id=(x, y, 1 - c), device_id_type=MESH)


def _sibling_start(ps, after, *, name):
    n = len(ps)
    lands = [lax.empty(p.shape, p.dtype) for p in ps]

    def body(*refs):
        outs = refs[-(4 * n + 1):]
        for k in range(n):
            _sibling_copy(refs[:n], refs[n:2 * n], outs[:n], outs[n:2 * n], k).start()
        outs[-1][...] = jnp.zeros_like(outs[-1])

    sem = pltpu.SemaphoreType.DMA(())
    outs = pl.pallas_call(
        body, name=name,
        out_shape=(*[sem] * (2 * n), *[pltpu.HBM(a.shape, a.dtype) for a in (*ps, *lands)], _TOKEN),
        in_specs=[*[_HBM] * (2 * n), _ANY],
        out_specs=(*[_SEM] * (2 * n), *[_HBM] * (2 * n), pl.BlockSpec(memory_space=pltpu.VMEM)),
        input_output_aliases={k: 2 * n + k for k in range(2 * n)},
        compiler_params=pltpu.CompilerParams(has_side_effects=pltpu.SideEffectType.DATAFLOW_SIDE_EFFECTING),
    )(*[_in_hbm(a) for a in (*ps, *lands)], after)
    return outs[:n], outs[n:2 * n], outs[2 * n:3 * n], outs[3 * n:4 * n], outs[-1]


def _sibling_wait(sends, recvs, ps, lands, after, *, name):
    n = len(ps)

    def body(*refs):
        for k in range(n):
            cp = _sibling_copy(refs[:n], refs[n:2 * n], refs[2 * n:3 * n], refs[3 * n:4 * n], k)
            cp.wait_send()
            cp.wait_recv()

    outs = pl.pallas_call(
        body, name=name, out_shape=[pltpu.HBM(a.shape, a.dtype) for a in (*ps, *lands)],
        in_specs=[*[_HBM] * (2 * n), *[_SEM] * (2 * n), _ANY], out_specs=[_HBM] * (2 * n),
        input_output_aliases={k: k for k in range(2 * n)},
        compiler_params=pltpu.CompilerParams(has_side_effects=pltpu.SideEffectType.DATAFLOW_SIDE_EFFECTING),
    )(*ps, *lands, *sends, *recvs, after)
    return outs[:n], outs[n:]


def kernel(x, c, ctx, c_ctx, w_ada, b_ada, ln_g, ln_b, ff1_in, ff1_out, ff2_in, ff2_out, w_in, conv4_w, conv4_b, w_rg, b_rg, w_ig, b_ig, lam, conv31_w, conv31_b, cln_g, cln_b, w_out, b_out, loss_target, m_c_ctx, m_w_ada, m_b_ada, m_ln_g, m_ln_b, m_ff1_in, m_ff1_out, m_ff2_in, m_ff2_out, m_w_in, m_conv4_w, m_conv4_b, m_w_rg, m_b_rg, m_w_ig, m_b_ig, m_lam, m_conv31_w, m_conv31_b, m_cln_g, m_cln_b, m_w_out, m_b_out, v_c_ctx, v_w_ada, v_b_ada, v_ln_g, v_ln_b, v_ff1_in, v_ff1_out, v_ff2_in, v_ff2_out, v_w_in, v_conv4_w, v_conv4_b, v_w_rg, v_b_rg, v_w_ig, v_b_ig, v_lam, v_conv31_w, v_conv31_b, v_cln_g, v_cln_b, v_w_out, v_b_out):
    weights = dict(c_ctx=c_ctx, w_ada=w_ada, b_ada=b_ada, ln_g=ln_g, ln_b=ln_b, ff1_in=ff1_in, ff1_out=ff1_out,
                   ff2_in=ff2_in, ff2_out=ff2_out, w_in=w_in, conv4_w=conv4_w, conv4_b=conv4_b, w_rg=w_rg, b_rg=b_rg,
                   w_ig=w_ig, b_ig=b_ig, lam=lam, conv31_w=conv31_w, conv31_b=conv31_b, cln_g=cln_g, cln_b=cln_b,
                   w_out=w_out, b_out=b_out)
    m_in = dict(c_ctx=m_c_ctx, w_ada=m_w_ada, b_ada=m_b_ada, ln_g=m_ln_g, ln_b=m_ln_b, ff1_in=m_ff1_in, ff1_out=m_ff1_out,
                ff2_in=m_ff2_in, ff2_out=m_ff2_out, w_in=m_w_in, conv4_w=m_conv4_w, conv4_b=m_conv4_b, w_rg=m_w_rg,
                b_rg=m_b_rg, w_ig=m_w_ig, b_ig=m_b_ig, lam=m_lam, conv31_w=m_conv31_w, conv31_b=m_conv31_b,
                cln_g=m_cln_g, cln_b=m_cln_b, w_out=m_w_out, b_out=m_b_out)
    v_in = dict(c_ctx=v_c_ctx, w_ada=v_w_ada, b_ada=v_b_ada, ln_g=v_ln_g, ln_b=v_ln_b, ff1_in=v_ff1_in, ff1_out=v_ff1_out,
                ff2_in=v_ff2_in, ff2_out=v_ff2_out, w_in=v_w_in, conv4_w=v_conv4_w, conv4_b=v_conv4_b, w_rg=v_w_rg,
                b_rg=v_b_rg, w_ig=v_w_ig, b_ig=v_b_ig, lam=v_lam, conv31_w=v_conv31_w, conv31_b=v_conv31_b,
                cln_g=v_cln_g, cln_b=v_cln_b, w_out=v_w_out, b_out=v_b_out)
    order = list(weights)

    ax, ay, ac = lax.axis_index("x"), lax.axis_index("y"), lax.axis_index("c")
    chip = 2 * ax + ay
    dev = 4 * ax + 2 * ay + ac
    chip_arr = jnp.reshape(chip, (1,)).astype(jnp.int32)

    L, D, Na = w_ada.shape
    Tl, Tc = x.shape[1], ctx.shape[1]
    T = Tc + Tl
    tms = Tc
    C = conv4_w.shape[2] * N_CHIPS
    nH, hds, hd = w_rg.shape[2], w_rg.shape[3], w_rg.shape[4]
    K31 = conv31_w.shape[1]
    assert L == 2 and Tl % tms == 0 and tms % GRID_W == 0 and tms % 8 == 0 and tms & (tms - 1) == 0
    assert hds * N_CHIPS == hd and nH * hd == C and D == 2 * C and K31 // 2 < CONV_PAD
    alpha = (2 * L) ** 0.25
    n_mod = N_CHIPS * Na // D

    def shard_cols(full, width):
        return lax.dynamic_slice_in_dim(full, chip * width, width, axis=full.ndim - 1)

    c8 = jnp.zeros((8, D), F32).at[0].set(c[0]).at[1].set(c_ctx)
    c_all = _allgather_small(c8, name="gather_cond").reshape(N_DEV, 8, D)
    c16 = jnp.concatenate([c_all[:, 0], c_ctx[None], jnp.zeros((7, D), F32)], axis=0)
    s16, ds16 = _silu_rows(c16, name="silu_cond")
    mod_part = _ada_fwd(s16, w_ada, shard_cols(b_ada, Na), name="ada_fwd")
    mod_all = _allgather_small(mod_part.reshape(L * 16, Na), name="gather_mod").reshape(N_DEV, L, 16, Na)
    mod_full = jnp.transpose(mod_all[0::2], (1, 2, 0, 3)).reshape(L, 16, N_CHIPS * Na)
    mod_rows = jnp.stack([mod_full[:, 8], lax.dynamic_index_in_dim(mod_full, dev, axis=1, keepdims=False)], axis=1)
    mod = mod_rows.reshape(L, 2, n_mod, D)

    def mvec(l, k):
        return mod[l, :, k, :]

    def full_gate(g):
        return jnp.transpose(g, (1, 2, 0, 3, 4)).reshape(2, nH, hd, hd)

    small_sharded = ("ln_g", "ln_b", "conv4_w", "b_rg", "b_ig", "lam", "conv31_w")
    pieces = {n: weights[n].reshape(-1, weights[n].shape[-1]) for n in small_sharded}
    widths = {n: p.shape[1] for n, p in pieces.items()}
    rows_of = {n: p.shape[0] for n, p in pieces.items()}
    wcat = max(widths.values())
    cat = jnp.concatenate([jnp.pad(p, ((0, 0), (0, wcat - p.shape[1]))) for p in pieces.values()], axis=0)
    rpad = -cat.shape[0] % 8
    cat_all = _allgather_small(jnp.pad(cat, ((0, rpad), (0, 0))), name="gather_small").reshape(N_DEV, -1, wcat)
    full_small, r0 = {}, 0
    for n in small_sharded:
        blk = cat_all[0::2, r0:r0 + rows_of[n], :widths[n]]
        full_small[n] = jnp.transpose(blk, (1, 0, 2)).reshape(rows_of[n], N_CHIPS * widths[n])
        r0 += rows_of[n]
    ln_g_f = full_small["ln_g"].reshape(L, 3, 1, D)
    ln_b_f = full_small["ln_b"].reshape(L, 3, 1, D)
    conv4_w_f = full_small["conv4_w"].reshape(L, 4, C)
    b_rg_f = full_small["b_rg"].reshape(L, 2, 1, C)
    b_ig_f = full_small["b_ig"].reshape(L, 2, 1, C)
    lam_f = full_small["lam"].reshape(L, 2, 1, C)
    conv31_w_f = full_small["conv31_w"].reshape(L, K31, C)

    ones, zeros = jnp.ones((1, D), F32), jnp.zeros((1, D), F32)

    big_names = ("ff1_in", "ff1_out", "w_in", "w_rg", "w_ig", "w_out", "ff2_in", "ff2_out")
    inflight, tok = [], mod[0, 0, 0, :1] + cat_all[0, 0, :1]
    for l in range(L):
        lands = [_place_cast(weights[n], l, chip_arr, name=f"place_{n}_{l}") for n in big_names]
        sends, recvs, _, lands_t, tok = _exchange_start([], lands, tok, name=f"gather_start_{l}")
        inflight.append({n: (sends[k], recvs[k], lands_t[k]) for k, n in enumerate(big_names)})
    gw = [{} for _ in range(L)]

    def gathered(l, n, after):
        s, r, land = inflight[l][n]
        land = _exchange_wait([s], [r], [], [land], after, name=f"gather_wait_{n}_{l}")[1][0]
        gw[l][n] = land.reshape(N_CHIPS, *weights[n].shape[1:])
        return gw[l][n]

    s0 = jnp.concatenate([ctx[0], x[0]], axis=0)
    cur = (s0, ones, zeros)
    saved = []
    for l in range(L):
        sv = {"in": cur}
        w = gathered(l, "ff1_in", tok if l == 0 else cur[0])
        h1, gu1, act1 = _in_proj(*cur, mvec(l, 0), mvec(l, 1), w, tc=Tc, swiglu=True, name=f"ffn1_in_{l}")
        xh1, rs1, f1 = _out_proj_ln(act1, gathered(l, "ff1_out", act1), zeros, *cur, mvec(l, 2), 0.5, alpha, tc=Tc, name=f"ffn1_out_{l}")
        sv.update(h1=h1, gu1=gu1, act1=act1, xh1=xh1, rs1=rs1, f1=f1)
        cur1 = (xh1, ln_g_f[l, 0], ln_b_f[l, 0])
        h2, z = _in_proj(*cur1, mvec(l, 3), mvec(l, 4), gathered(l, "w_in", xh1), tc=Tc, swiglu=False, name=f"mix_in_{l}")
        xc = _conv4_fwd(z, conv4_w_f[l], conv4_b[l][None], tms=tms, name=f"conv4_{l}")
        wr_l, wi_l = full_gate(gathered(l, "w_rg", xc)), full_gate(gathered(l, "w_ig", xc))
        sv.update(wr=wr_l, wi=wi_l)
        rec = []
        for d in range(2):
            rec.append(_lru_fwd(xc, wr_l[d], wi_l[d], b_rg_f[l, d], b_ig_f[l, d], lam_f[l, d],
                                rev=bool(d), tms=tms, name=f"lru_{l}_{d}"))
        ymix, uc = _convmod_fwd(rec[0][0], rec[1][0], z, conv31_w_f[l], conv31_b[l][None], cln_g[l][None], cln_b[l][None],
                                tms=tms, name=f"convmod_{l}")
        xh2, rs2, f2 = _out_proj_ln(ymix, gathered(l, "w_out", ymix), b_out[l][None], *cur1, mvec(l, 5), 1.0, alpha, tc=Tc, name=f"mix_out_{l}")
        sv.update(h2=h2, z=z, xc=xc, rec=rec, ymix=ymix, uc=uc, xh2=xh2, rs2=rs2, f2=f2)
        cur2 = (xh2, ln_g_f[l, 1], ln_b_f[l, 1])
        h3, gu3, act3 = _in_proj(*cur2, mvec(l, 6), mvec(l, 7), gathered(l, "ff2_in", xh2), tc=Tc, swiglu=True, name=f"ffn2_in_{l}")
        xh3, rs3, f3 = _out_proj_ln(act3, gathered(l, "ff2_out", act3), zeros, *cur2, mvec(l, 8), 0.5, alpha, tc=Tc, name=f"ffn2_out_{l}")
        sv.update(h3=h3, gu3=gu3, act3=act3, xh3=xh3, rs3=rs3, f3=f3)
        cur = (xh3, ln_g_f[l, 2], ln_b_f[l, 2])
        saved.append(sv)

    ds, loss_blk = _loss_head(*cur, loss_target[0], tc=Tc, tms=tms, name="loss_head")

    reduced = {n: None for n in big_names}
    pending = []

    def start_group(l, names, gs):
        gs = [g.reshape(N_CHIPS, -1, g.shape[-1]) for g in gs]
        lands = [lax.empty(g.shape, F32) for g in gs]
        sends, recvs, srcs_t, lands_t, token = _exchange_start(gs, lands, None, name=f"grad_start_{names[0]}_{l}")
        pending.append((sends, recvs, srcs_t, lands_t, names, l))
        return token

    def finish_group(after):
        sends, recvs, srcs_t, lands_t, names, l = pending.pop(0)
        gs, lands = _exchange_wait(sends, recvs, srcs_t, lands_t, after, name=f"grad_wait_{names[0]}_{l}")
        for k, n in enumerate(names):
            reduced[n] = _sum_scattered(gs[k], lands[k], chip_arr, l, L, reduced[n], name=f"grad_sum_{n}_{l}")

    def finish_older(keep, after):
        while len(pending) > keep:
            finish_group(after)

    dmod = [[None] * n_mod for _ in range(L)]
    d_ln_g = [[None] * 3 for _ in range(L)]
    d_ln_b = [[None] * 3 for _ in range(L)]
    small = {n: [None] * L for n in ("conv4_w", "conv4_b", "conv31_w", "conv31_b", "cln_g", "cln_b", "b_out")}
    gate_w = {n: [[None, None] for _ in range(L)] for n in ("w_rg", "w_ig", "b_rg", "b_ig", "lam")}

    def ffn_bwd(ds, l, k, names, sv_in, sfx, after, before_dx=None):
        sv = saved[l]
        dy, dres, d_ln_g[l][k], d_ln_b[l][k], dmod[l][3 * k + 2], _ = _ln_bwd(
            ds, sv["xh" + sfx], sv["rs" + sfx], ln_g_f[l, k], sv["f" + sfx], mvec(l, 3 * k + 2), 0.5, alpha, after,
            tc=Tc, name=f"ffn{sfx}_ln_bwd_{l}")
        dg = _nt(dy, gw[l][names[1]], sv["gu" + sfx], name=f"ffn{sfx}_dact_{l}")
        g_in = _wgrad(sv["h" + sfx], dg, cols_sharded=True, name=f"ffn{sfx}_wgrad_in_{l}")
        t_in = start_group(l, (names[0],), (g_in,))
        g_out = _wgrad(sv["act" + sfx], dy, cols_sharded=False, name=f"ffn{sfx}_wgrad_out_{l}", after=t_in)
        t_out = start_group(l, (names[1],), (g_out,))
        if before_dx is not None:
            t_out = before_dx(t_out)
        ds_new, dmod[l][3 * k + 1], dmod[l][3 * k] = _dx_modbwd(
            dg, gw[l][names[0]], dres, *sv_in, mvec(l, 3 * k + 1), tc=Tc, name=f"ffn{sfx}_dx_{l}", after=t_out)
        return ds_new, t_out

    def gate_slots(g):
        g = g.reshape(2, nH, N_CHIPS, hds, hd)
        return jnp.transpose(g, (2, 0, 1, 3, 4)).reshape(N_CHIPS, 2 * nH * hds, hd)

    delta, new_m, new_v, grads, swaps = {}, {}, {}, {}, {}

    def start_swap(names, tag, after):
        return (names, tag, *_sibling_start([reduced[n] for n in names], after, name=f"grad_swap_start_{tag}"))

    def finish_weights(swap, after):
        names, tag, sends, recvs, own, lands, _ = swap
        own, others = _sibling_wait(sends, recvs, own, lands, after, name=f"grad_swap_wait_{tag}")
        for n, mine, other in zip(names, own, others):
            shp = weights[n].shape
            grads[n], delta[n], new_m[n], new_v[n] = _adamw(weights[n], mine.reshape(shp), m_in[n], v_in[n],
                                                            other.reshape(shp), name=f"adamw_{n}")

    def swap_early(tok):
        finish_older(2, tok)
        swaps["early"] = start_swap(("ff2_in", "ff2_out", "w_in", "w_out", "w_rg", "w_ig"), "early", tok)
        return swaps["early"][-1]

    token = None
    for l in reversed(range(L)):
        sv = saved[l]
        cur1 = (sv["xh1"], ln_g_f[l, 0], ln_b_f[l, 0])
        cur2 = (sv["xh2"], ln_g_f[l, 1], ln_b_f[l, 1])
        ds, token = ffn_bwd(ds, l, 2, ("ff2_in", "ff2_out"), cur2, "3", token)
        finish_older(2, ds)
        dy, dres, d_ln_g[l][1], d_ln_b[l][1], dmod[l][5], small["b_out"][l] = _ln_bwd(
            ds, sv["xh2"], sv["rs2"], ln_g_f[l, 1], sv["f2"], mvec(l, 5), 1.0, alpha, token, tc=Tc, name=f"mix_ln_bwd_{l}")
        dymix = _nt(dy, gw[l]["w_out"], None, name=f"mix_dy_{l}")
        g_w_out = _wgrad(sv["ymix"], dy, cols_sharded=False, name=f"mix_wgrad_out_{l}")
        dhs, dz, small["conv31_w"][l], small["conv31_b"][l], small["cln_g"][l], small["cln_b"][l] = _convmod_bwd(
            dymix, sv["rec"][0][0], sv["rec"][1][0], sv["z"], sv["uc"], conv31_w_f[l], cln_g[l][None], cln_b[l][None],
            tms=tms, name=f"convmod_bwd_{l}")
        dxc = []
        for d in range(2):
            hd_, rd_, id_ = sv["rec"][d]
            o = _lru_bwd(dhs, hd_, rd_, id_, sv["xc"], sv["wr"][d], sv["wi"][d], lam_f[l, d],
                         rev=bool(d), tms=tms, name=f"lru_bwd_{l}_{d}")
            dxc.append(o[0])
            for n, val in zip(("w_rg", "w_ig", "b_rg", "b_ig", "lam"), o[1:]):
                gate_w[n][l][d] = val
        dz, small["conv4_w"][l], small["conv4_b"][l] = _conv4_bwd(dxc[0], dxc[1], sv["z"], conv4_w_f[l], dz, tms=tms, name=f"conv4_bwd_{l}")
        ds, dmod[l][4], dmod[l][3] = _dx_modbwd(dz, gw[l]["w_in"], dres, *cur1, mvec(l, 4), tc=Tc, name=f"mix_dx_{l}")
        g_w_in = _wgrad(sv["h2"], dz, cols_sharded=True, name=f"mix_wgrad_in_{l}")
        token = start_group(l, ("w_out", "w_in", "w_rg", "w_ig"),
                            (g_w_out, g_w_in, gate_slots(jnp.stack(gate_w["w_rg"][l])), gate_slots(jnp.stack(gate_w["w_ig"][l]))))
        finish_older(1, ds)
        ds, token = ffn_bwd(ds, l, 0, ("ff1_in", "ff1_out"), sv["in"], "1", token, before_dx=None if l else swap_early)
        finish_older(2, ds)

    grad_x = ds[Tc:][None]

    finish_weights(swaps["early"], ds)
    finish_older(0, new_v["ff2_in"])
    swaps["late"] = start_swap(("ff1_in", "ff1_out"), "late", new_v["ff2_in"])


    dmod_arr = jnp.stack([jnp.stack(dmod[l], axis=1) for l in range(L)])
    dm_ctx = dmod_arr[:, 0].reshape(L, n_mod * D)
    dm_lat = dmod_arr[:, 1].reshape(L, n_mod * D)
    summed = {
        "loss": loss_blk[0:1, 0:1],
        "dm_ctx": dm_ctx,
        "ln_g": jnp.stack([jnp.concatenate(d_ln_g[l], axis=0) for l in range(L)]),
        "ln_b": jnp.stack([jnp.concatenate(d_ln_b[l], axis=0) for l in range(L)]),
        "conv4_w": jnp.stack(small["conv4_w"]),
        "conv4_b": jnp.concatenate(small["conv4_b"], axis=0),
        "b_rg": jnp.stack([jnp.concatenate(gate_w["b_rg"][l], axis=0) for l in range(L)]),
        "b_ig": jnp.stack([jnp.concatenate(gate_w["b_ig"][l], axis=0) for l in range(L)]),
        "lam": jnp.stack([jnp.concatenate(gate_w["lam"][l], axis=0) for l in range(L)]),
        "conv31_w": jnp.stack(small["conv31_w"]),
        "conv31_b": jnp.concatenate(small["conv31_b"], axis=0),
        "cln_g": jnp.concatenate(small["cln_g"], axis=0),
        "cln_b": jnp.concatenate(small["cln_b"], axis=0),
        "b_out": jnp.concatenate(small["b_out"], axis=0),
        "dm_lat": dm_lat,
    }
    flat = jnp.concatenate([v.reshape(-1) for v in summed.values()])
    n_flat = flat.shape[0]
    n_rows = -(-n_flat // 128)
    n_rows += -n_rows % 8
    vec = jnp.pad(flat, (0, n_rows * 128 - n_flat)).reshape(n_rows, 128)
    vec_all = _allgather_small(vec, name="gather_small_grads", after=swaps["late"][-1]).reshape(N_DEV, n_rows, 128)
    vec_sum = _sum_leading(vec_all, tuple(range(N_DEV)), name="sum_small_grads").reshape(-1)
    tot, off = {}, 0
    for n, v in summed.items():
        tot[n] = vec_sum[off:off + v.size].reshape(v.shape)
        if n == "dm_lat":
            dm_lat_all = vec_all.reshape(N_DEV, -1)[:, off:off + v.size].reshape(N_DEV, L, n_mod * D)
        off += v.size
    loss = tot["loss"].reshape(())

    dm16 = jnp.concatenate([jnp.transpose(dm_lat_all, (1, 0, 2)), tot["dm_ctx"][:, None], jnp.zeros((L, 7, n_mod * D), F32)], axis=1)
    g_w_ada, ds16_part = _ada_bwd(s16, shard_cols(dm16, Na), w_ada, name="ada_bwd")
    ds_all = _allgather_small(ds16_part[8:16], name="gather_dcond").reshape(N_DEV, 8, D)
    g_c_ctx = _sum_leading(ds_all[:, 0:1], (0, 2, 4, 6), name="sum_dcond", scale_by=ds16[8:9]).reshape(D)
    g_b_ada = _sum_leading(jnp.stack([tot["dm_lat"], tot["dm_ctx"]]), (0, 1), name="sum_b_ada")

    finish_weights(swaps["late"], g_b_ada)
    grads.update(c_ctx=g_c_ctx, w_ada=g_w_ada, b_ada=g_b_ada)
    for n in ("ln_g", "ln_b", "conv4_w", "b_rg", "b_ig", "lam", "conv31_w"):
        grads[n] = shard_cols(tot[n], weights[n].shape[-1])
    for n in ("conv4_b", "conv31_b", "cln_g", "cln_b", "b_out"):
        grads[n] = tot[n]
    for n in order:
        if n not in reduced:
            delta[n], new_m[n], new_v[n] = _adamw(weights[n], grads[n], m_in[n], v_in[n], name=f"adamw_{n}")
    return (loss, grad_x, *[grads[n] for n in order], *[delta[n] for n in order],
            *[new_m[n] for n in order], *[new_v[n] for n in order])
```
